```python
import jax, jax.numpy as jnp
from jax import lax
import numpy as np

D_MODEL = 1024
BATCH = 8
SEQ = 4096
DEPTH = 2

GRID_W = 64
CTX_LEN = 256
N_EVEN = (DEPTH + 1) // 2
N_ODD = DEPTH // 2
EPS = 1e-6

SGU_GROUPS = 4
SGU_GROUP_DIM = 128
SGU_DIM = SGU_GROUPS * SGU_GROUP_DIM
SGU_CHUNK = 128
HGRN_HEADS = 4
HGRN_KDIM = 128
HGRN_VDIM = 128
HGRN_DIM = HGRN_HEADS * HGRN_KDIM
HGRN_CHUNK = 64
AB_IN = 2 * SGU_DIM + 5 * HGRN_DIM
AB_OUT = SGU_DIM + HGRN_DIM
ATTN_HEADS = 16
ATTN_KV_HEADS = 4
ATTN_GROUP = ATTN_HEADS // ATTN_KV_HEADS
HEAD_DIM = 64
ATTN_Q_DIM = ATTN_HEADS * HEAD_DIM
QKV_DIM = (ATTN_HEADS + 2 * ATTN_KV_HEADS) * HEAD_DIM
WINDOW = 128
ATTN_BLOCK = 128
ROPE_BASE = 10000.0
N_EXPERTS = 16
CAPACITY_FACTOR = 2
EXPERT_FF = 2048

kernel_name = "hybrid_sgu_hgrn2_swa_ec_moe_diffusion"


def _rmsnorm(x, g):
    xf = x.astype(jnp.float32)
    y = xf * lax.rsqrt(jnp.mean(xf * xf, axis=-1, keepdims=True) + EPS)
    return (y * g.astype(jnp.float32)).astype(x.dtype)


def _modulate(x, g, shift, scale):
    return _rmsnorm(x, g) * (1 + scale) + shift


def _chunk_sgu(zu, zv, w_s, b_s):
    B, N, _ = zu.shape
    nc = N // SGU_CHUNK
    u = jax.nn.gelu(zu, approximate=False)
    v = jax.nn.gelu(zv, approximate=False).reshape(B, nc, SGU_CHUNK, SGU_GROUPS, SGU_GROUP_DIM)
    vf = v.astype(jnp.float32)
    mu = jnp.mean(vf, axis=-1, keepdims=True)
    var = jnp.mean(jnp.square(vf - mu), axis=-1, keepdims=True)
    vn = ((vf - mu) * lax.rsqrt(var + EPS)).astype(zv.dtype)
    s = jnp.einsum('gts,bnsgc->bntgc', w_s, vn) + b_s.T[:, :, None]
    return u * s.reshape(B, N, SGU_DIM)


def _gla_scan(q, k, v, log_f, s0):
    B, H, N, K = q.shape
    nc = N // HGRN_CHUNK

    def to_chunks(a):
        return jnp.moveaxis(a.reshape(B, H, nc, HGRN_CHUNK, a.shape[-1]), 2, 0)

    tri = jnp.tril(jnp.ones((HGRN_CHUNK, HGRN_CHUNK), dtype=bool))[:, :, None]

    def step(S, inp):
        qc, kc, vc, gc = inp
        b = jnp.cumsum(gc, axis=2)
        o_inter = jnp.einsum('bhtk,bhkv->bhtv', qc * jnp.exp(b), S)
        rel = jnp.exp(jnp.where(tri, b[:, :, :, None, :] - b[:, :, None, :, :], -jnp.inf))
        a = jnp.einsum('bhtk,bhsk,bhtsk->bhts', qc, kc, rel)
        o_intra = jnp.einsum('bhts,bhsv->bhtv', a, vc)
        b_end = b[:, :, -1:, :]
        S = jnp.exp(b_end[:, :, 0, :, None]) * S + jnp.einsum('bhsk,bhsv->bhkv', kc * jnp.exp(b_end - b), vc)
        return S, o_inter + o_intra

    s_fin, o = lax.scan(step, s0, (to_chunks(q), to_chunks(k), to_chunks(v), to_chunks(log_f)))
    return jnp.moveaxis(o, 0, 2).reshape(B, H, N, v.shape[-1]), s_fin


def _hgrn2(zq, zf_fwd, zf_bwd, zi, lb, s0):
    B, N, _ = zq.shape

    def heads(a):
        return a.astype(jnp.float32).reshape(B, N, HGRN_HEADS, -1).transpose(0, 2, 1, 3)

    q = heads(jax.nn.silu(zq))
    v = heads(zi)
    outs, finals = [], []
    for d, zf in enumerate((zf_fwd, zf_bwd)):
        f = heads(lb[d] + (1.0 - lb[d]) * jax.nn.sigmoid(zf.astype(jnp.float32)))
        args = (q, 1.0 - f, v, jnp.log(f))
        if d == 1:
            args = tuple(jnp.flip(a, axis=2) for a in args)
        o, s_fin = _gla_scan(*args, s0[d])
        outs.append(jnp.flip(o, axis=2) if d == 1 else o)
        finals.append(s_fin)
    return outs[0] + outs[1], jnp.stack(finals)


def _even_mixer(h, w_in, w_out, sgu_w, sgu_b, lb, norm_g, s0):
    B, N, _ = h.shape
    z = h @ w_in
    cuts = [SGU_DIM, 2 * SGU_DIM] + [2 * SGU_DIM + i * HGRN_DIM for i in range(1, 5)]
    zu, zv, zq, zf_fwd, zf_bwd, zi, zg = jnp.split(z, cuts, axis=-1)
    y_a = _chunk_sgu(zu, zv, sgu_w, sgu_b)
    o, states = _hgrn2(zq, zf_fwd, zf_bwd, zi, lb, s0)
    o = o * lax.rsqrt(jnp.mean(o * o, axis=-1, keepdims=True) + EPS) * norm_g.astype(jnp.float32)[:, None, :]
    y_b = o.transpose(0, 2, 1, 3).reshape(B, N, HGRN_DIM).astype(h.dtype) * jax.nn.silu(zg)
    return jnp.concatenate([y_a, y_b], axis=-1) @ w_out, states


def _rope_2d(x, row, col):
    half = HEAD_DIM // 2
    quarter = half // 2
    inv = ROPE_BASE ** (-jnp.arange(quarter, dtype=jnp.float32) / quarter)

    def rot(xa, pos):
        ang = pos.astype(jnp.float32)[:, None] * inv[None, :]
        cos, sin = jnp.cos(ang), jnp.sin(ang)
        x1, x2 = xa[..., :quarter], xa[..., quarter:]
        return jnp.concatenate([x1 * cos - x2 * sin, x2 * cos + x1 * sin], axis=-1)

    xf = x.astype(jnp.float32)
    return jnp.concatenate([rot(xf[..., :half], row), rot(xf[..., half:], col)], axis=-1).astype(x.dtype)


def _sink_softmax(scores, sink):
    s_sink = jnp.broadcast_to(sink.astype(jnp.float32), scores.shape[:-1] + (1,))
    return jax.nn.softmax(jnp.concatenate([scores, s_sink], axis=-1), axis=-1)[..., :-1]


def _window_attention(h_lat, h_ctx, w_qkv, w_out, sink, need_ctx):
    B, N, _ = h_lat.shape
    Lc = h_ctx.shape[1]
    scale = HEAD_DIM ** -0.5
    sink_b = sink.reshape(ATTN_KV_HEADS, ATTN_GROUP, 1, 1)

    def split_q(zq, L):
        return zq.reshape(B, L, ATTN_KV_HEADS, ATTN_GROUP, HEAD_DIM).transpose(0, 2, 3, 1, 4)

    def split_kv(zkv, L):
        k, v = jnp.split(zkv, 2, axis=-1)
        k = k.reshape(B, L, ATTN_KV_HEADS, HEAD_DIM).transpose(0, 2, 1, 3)
        v = v.reshape(B, L, ATTN_KV_HEADS, HEAD_DIM).transpose(0, 2, 1, 3)
        return k, v

    kc, vc = split_kv(h_ctx @ w_qkv[:, ATTN_Q_DIM:], Lc)
    z = h_lat @ w_qkv
    q = split_q(z[..., :ATTN_Q_DIM], N)
    k, v = split_kv(z[..., ATTN_Q_DIM:], N)
    t = jnp.arange(N)
    row, col = t // GRID_W, t % GRID_W
    q = _rope_2d(q, row, col)
    k = _rope_2d(k, row, col)

    nb = N // ATTN_BLOCK
    pad = [(0, 0), (0, 0), (ATTN_BLOCK, ATTN_BLOCK), (0, 0)]
    kp, vp = jnp.pad(k, pad), jnp.pad(v, pad)
    r_idx = jnp.arange(ATTN_BLOCK)[:, None]
    s_idx = jnp.arange(3 * ATTN_BLOCK)[None, :]
    band = jnp.abs(s_idx - r_idx - ATTN_BLOCK) <= WINDOW

    def block(bi):
        start = bi * ATTN_BLOCK
        qb = lax.dynamic_slice_in_dim(q, start, ATTN_BLOCK, axis=3)
        kb = lax.dynamic_slice_in_dim(kp, start, 3 * ATTN_BLOCK, axis=2)
        vb = lax.dynamic_slice_in_dim(vp, start, 3 * ATTN_BLOCK, axis=2)
        j = start - ATTN_BLOCK + s_idx
        valid = band & (j >= 0) & (j < N)
        s_loc = jnp.einsum('bkgqd,bksd->bkgqs', qb, kb).astype(jnp.float32) * scale
        s_loc = jnp.where(valid, s_loc, -jnp.inf)
        s_ctx = jnp.einsum('bkgqd,bksd->bkgqs', qb, kc).astype(jnp.float32) * scale
        p = _sink_softmax(jnp.concatenate([s_loc, s_ctx], axis=-1), sink_b)
        p_loc = p[..., :3 * ATTN_BLOCK].astype(v.dtype)
        p_ctx = p[..., 3 * ATTN_BLOCK:].astype(v.dtype)
        return jnp.einsum('bkgqs,bksd->bkgqd', p_loc, vb) + jnp.einsum('bkgqs,bksd->bkgqd', p_ctx, vc)

    o = lax.map(block, jnp.arange(nb))
    y_lat = o.transpose(1, 0, 4, 2, 3, 5).reshape(B, N, ATTN_Q_DIM) @ w_out

    y_ctx = None
    if need_ctx:
        qc = split_q(h_ctx @ w_qkv[:, :ATTN_Q_DIM], Lc)
        sc = jnp.einsum('bkgqd,bksd->bkgqs', qc, kc).astype(jnp.float32) * scale
        pc = _sink_softmax(sc, sink_b).astype(vc.dtype)
        oc = jnp.einsum('bkgqs,bksd->bkgqd', pc, vc)
        y_ctx = oc.transpose(0, 3, 1, 2, 4).reshape(B, Lc, ATTN_Q_DIM) @ w_out
    return y_lat, y_ctx


def _expert_choice_ffn(h, w_r, w1, w3, w2):
    B, N, D = h.shape
    cap = CAPACITY_FACTOR * N // N_EXPERTS
    aff = jax.nn.softmax((h @ w_r).astype(jnp.float32), axis=-1)
    gate, idx = lax.top_k(jnp.swapaxes(aff, 1, 2), cap)
    xe = jax.vmap(lambda hb, ib: hb[ib])(h, idx)
    a = jnp.einsum('becd,edf->becf', xe, w1)
    u = jnp.einsum('becd,edf->becf', xe, w3)
    y = jnp.einsum('becf,efd->becd', jax.nn.silu(a) * u, w2) * gate[..., None].astype(h.dtype)
    return jax.vmap(lambda yb, ib: jnp.zeros((N, D), yb.dtype).at[ib.reshape(-1)].add(yb.reshape(-1, D)))(y, idx)


def setup_inputs(seed: int = 0) -> dict:
    key = jax.random.key(seed)
    ks = jax.random.split(key, 24)
    D = D_MODEL

    def nrm(k, shape, s):
        return jax.random.normal(k, shape, jnp.float32) * s

    return {
        "x": nrm(ks[0], (BATCH, SEQ, D), 1.0),
        "c": nrm(ks[1], (BATCH, D), 1.0),
        "ctx": nrm(ks[2], (BATCH, CTX_LEN, D), 1.0),
        "c_ctx": nrm(ks[3], (D,), 1.0),
        "mod_w": nrm(ks[4], (DEPTH, D, 6 * D), 0.5 * D ** -0.5),
        "mod_b": nrm(ks[5], (DEPTH, 6 * D), 0.02),
        "norm1_g": 1.0 + nrm(ks[6], (DEPTH, D), 0.02),
        "norm2_g": 1.0 + nrm(ks[7], (DEPTH, D), 0.02),
        "ab_w_in": nrm(ks[8], (N_EVEN, D, AB_IN), D ** -0.5),
        "ab_w_out": nrm(ks[9], (N_EVEN, AB_OUT, D), AB_OUT ** -0.5),
        "sgu_w": nrm(ks[10], (N_EVEN, SGU_GROUPS, SGU_CHUNK, SGU_CHUNK), SGU_CHUNK ** -0.5),
        "sgu_b": 1.0 + nrm(ks[11], (N_EVEN, SGU_GROUPS, SGU_CHUNK), 0.02),
        "hgrn_lb_logits": nrm(ks[12], (N_EVEN + 1, 2, HGRN_DIM), 1.0),
        "hgrn_norm_g": 1.0 + nrm(ks[13], (N_EVEN, HGRN_HEADS, HGRN_VDIM), 0.02),
        "attn_w_qkv": nrm(ks[14], (N_ODD, D, QKV_DIM), D ** -0.5),
        "attn_w_out": nrm(ks[15], (N_ODD, ATTN_Q_DIM, D), ATTN_Q_DIM ** -0.5),
        "attn_sink": nrm(ks[16], (N_ODD, ATTN_HEADS), 1.0),
        "router_w": nrm(ks[17], (DEPTH, D, N_EXPERTS), D ** -0.5),
        "expert_w1": nrm(ks[18], (DEPTH, N_EXPERTS, D, EXPERT_FF), D ** -0.5),
        "expert_w3": nrm(ks[19], (DEPTH, N_EXPERTS, D, EXPERT_FF), D ** -0.5),
        "expert_w2": nrm(ks[20], (DEPTH, N_EXPERTS, EXPERT_FF, D), EXPERT_FF ** -0.5),
        "final_g": 1.0 + nrm(ks[21], (D,), 0.02),
    }


def reference(x, c, ctx, c_ctx, mod_w, mod_b, norm1_g, norm2_g, ab_w_in, ab_w_out, sgu_w, sgu_b,
              hgrn_lb_logits, hgrn_norm_g, attn_w_qkv, attn_w_out, attn_sink, router_w,
              expert_w1, expert_w3, expert_w2, final_g):
    B = x.shape[0]
    lb_all = jnp.cumsum(jax.nn.softmax(hgrn_lb_logits.astype(jnp.float32), axis=0), axis=0)
    x_lat, x_ctx = x, ctx
    for l in range(DEPTH):
        last = l == DEPTH - 1
        mod_lat = (jax.nn.silu(c) @ mod_w[l] + mod_b[l])[:, None, :]
        mod_ctx = (jax.nn.silu(c_ctx) @ mod_w[l] + mod_b[l])[None, None, :]
        sh1, sc1, g1, sh2, sc2, g2 = jnp.split(mod_lat, 6, axis=-1)
        csh1, csc1, cg1, csh2, csc2, cg2 = jnp.split(mod_ctx, 6, axis=-1)
        h_lat = _modulate(x_lat, norm1_g[l], sh1, sc1)
        h_ctx = _modulate(x_ctx, norm1_g[l], csh1, csc1)
        if l % 2 == 0:
            e = l // 2
            params = (ab_w_in[e], ab_w_out[e], sgu_w[e], sgu_b[e], lb_all[e], hgrn_norm_g[e])
            s0 = jnp.zeros((2, B, HGRN_HEADS, HGRN_KDIM, HGRN_VDIM), jnp.float32)
            y_ctx, s_ctx = _even_mixer(h_ctx, *params, s0)
            y_lat, _ = _even_mixer(h_lat, *params, s_ctx)
        else:
            o = l // 2
            y_lat, y_ctx = _window_attention(h_lat, h_ctx, attn_w_qkv[o], attn_w_out[o], attn_sink[o], not last)
        moe = (router_w[l], expert_w1[l], expert_w3[l], expert_w2[l])
        x_lat = x_lat + g1 * y_lat
        x_lat = x_lat + g2 * _expert_choice_ffn(_modulate(x_lat, norm2_g[l], sh2, sc2), *moe)
        if not last:
            x_ctx = x_ctx + cg1 * y_ctx
            x_ctx = x_ctx + cg2 * _expert_choice_ffn(_modulate(x_ctx, norm2_g[l], csh2, csc2), *moe)
    return _rmsnorm(x_lat, final_g)
```

```python
import functools

import numpy as np
import jax
import jax.numpy as jnp
from jax import lax
from jax.experimental import pallas as pl
from jax.experimental.pallas import tpu as pltpu

F32 = jnp.float32
BF16 = jnp.bfloat16

D_MODEL = 1024
GRID_W = 64
EPS = 1e-6
SGU_GROUPS = 4
SGU_GROUP_DIM = 128
SGU_DIM = SGU_GROUPS * SGU_GROUP_DIM
SGU_CHUNK = 128
HGRN_HEADS = 4
HGRN_KDIM = 128
HGRN_DIM = HGRN_HEADS * HGRN_KDIM
HGRN_CHUNK = 64
HGRN_LEVELS = 6
HGRN_TBLK = 256
AB_IN = 2 * SGU_DIM + 5 * HGRN_DIM
ATTN_HEADS = 16
ATTN_KV_HEADS = 4
ATTN_GROUP = ATTN_HEADS // ATTN_KV_HEADS
HEAD_DIM = 64
ATTN_Q_DIM = ATTN_HEADS * HEAD_DIM
ATTN_KV_DIM = ATTN_KV_HEADS * HEAD_DIM
WINDOW = 128
ATTN_BLOCK = 128
ROPE_BASE = 10000.0
N_EXPERTS = 16
CAPACITY_FACTOR = 2
EXPERT_FF = 2048
NEG_BIG = -1e30

VMEM_LIMIT_BYTES = 56 * 1024 * 1024


def _cparams(sem):
    return pltpu.CompilerParams(dimension_semantics=sem, vmem_limit_bytes=VMEM_LIMIT_BYTES)


def _dot(a, b):
    return jnp.dot(a, b, preferred_element_type=F32)


def _dot_nt(a, b):
    return lax.dot_general(a, b, (((1,), (1,)), ((), ())), preferred_element_type=F32)


def _dot_tn(a, b):
    return lax.dot_general(a, b, (((0,), (0,)), ((), ())), preferred_element_type=F32)


def _split2(a):
    hi = a.astype(BF16)
    lo = (a - hi.astype(F32)).astype(BF16)
    return hi, lo


def _dot_f32x3(a, b):
    a_hi, a_lo = _split2(a)
    b_hi, b_lo = _split2(b)
    return _dot(a_hi, b_hi) + (_dot(a_hi, b_lo) + _dot(a_lo, b_hi))


def _silu(x):
    return x * jax.nn.sigmoid(x)


def _gelu(x):
    return 0.5 * x * (1.0 + lax.erf(x * np.float32(1.0 / np.sqrt(2.0))))


def _rms_modulate(x, g, shift, scale):
    ms = jnp.mean(x * x, axis=-1, keepdims=True)
    return (x * lax.rsqrt(ms + EPS) * g) * (1.0 + scale) + shift


def _mod_kernel(c_ref, w_ref, b_ref, o_ref):
    s = _silu(c_ref[...])
    o_ref[0] = _dot_f32x3(s, w_ref[0]) + b_ref[0]


def _mod_vectors(cvec, mod_w, mod_b):
    depth, d, n6 = mod_w.shape
    rows = cvec.shape[0]
    tn = 1536
    return pl.pallas_call(
        _mod_kernel,
        grid=(depth, n6 // tn),
        in_specs=[
            pl.BlockSpec((rows, d), lambda l, j: (0, 0)),
            pl.BlockSpec((1, d, tn), lambda l, j: (l, 0, j)),
            pl.BlockSpec((1, 1, tn), lambda l, j: (l, 0, j)),
        ],
        out_specs=pl.BlockSpec((1, rows, tn), lambda l, j: (l, 0, j)),
        out_shape=jax.ShapeDtypeStruct((depth, rows, n6), F32),
        compiler_params=_cparams(("arbitrary", "arbitrary")),
        name="mod_vectors",
    )(cvec, mod_w, mod_b.reshape(depth, 1, n6))


def _proj_kernel(x_ref, g_ref, sh_ref, sc_ref, w_ref, z_ref):
    h = _rms_modulate(x_ref[0], g_ref[...], sh_ref[0], sc_ref[0])
    z_ref[0] = _dot(h.astype(BF16), w_ref[...])


def _proj_res_kernel(x_ref, m_ref, g2_ref, g_ref, sh_ref, sc_ref, w_ref, z_ref, xo_ref):
    x = x_ref[0] + g2_ref[0] * m_ref[0]
    xo_ref[0] = x
    h = _rms_modulate(x, g_ref[...], sh_ref[0], sc_ref[0])
    z_ref[0] = _dot(h.astype(BF16), w_ref[...])


def _row_spec(tm, d):
    return pl.BlockSpec((1, tm, d), lambda b, i: (b, i, 0))


def _vec_spec(d):
    return pl.BlockSpec((1, 1, d), lambda b, i: (b, 0, 0))


def _const_spec(shape):
    nd = len(shape)
    return pl.BlockSpec(shape, lambda b, i: (0,) * nd)


def _proj(x, g, shift, scale, w, tm=256):
    B, L, D = x.shape
    n = w.shape[1]
    return pl.pallas_call(
        _proj_kernel,
        grid=(B, L // tm),
        in_specs=[_row_spec(tm, D), _const_spec((1, D)), _vec_spec(D), _vec_spec(D), _const_spec((D, n))],
        out_specs=_row_spec(tm, n),
        out_shape=jax.ShapeDtypeStruct((B, L, n), F32),
        compiler_params=_cparams(("parallel", "parallel")),
        name="norm_mod_proj",
    )(x, g.reshape(1, D), shift, scale, w)


def _proj_res(x, moe, g2, g, shift, scale, w, tm=256):
    B, L, D = x.shape
    n = w.shape[1]
    return pl.pallas_call(
        _proj_res_kernel,
        grid=(B, L // tm),
        in_specs=[_row_spec(tm, D), _row_spec(tm, D), _vec_spec(D), _const_spec((1, D)), _vec_spec(D),
                  _vec_spec(D), _const_spec((D, n))],
        out_specs=[_row_spec(tm, n), _row_spec(tm, D)],
        out_shape=[jax.ShapeDtypeStruct((B, L, n), F32), jax.ShapeDtypeStruct((B, L, D), F32)],
        compiler_params=_cparams(("parallel", "parallel")),
        name="res_norm_mod_proj",
    )(x, moe, g2, g.reshape(1, D), shift, scale, w)


def _sgu_kernel(zu_ref, zv_ref, w_ref, b_ref, o_ref, *, tc):
    for ch in range(tc // SGU_CHUNK):
        rows = slice(ch * SGU_CHUNK, (ch + 1) * SGU_CHUNK)
        u = _gelu(zu_ref[0, rows, :])
        v = _gelu(zv_ref[0, rows, :])
        for g in range(SGU_GROUPS):
            cols = slice(g * SGU_GROUP_DIM, (g + 1) * SGU_GROUP_DIM)
            vg = v[:, cols]
            mu = jnp.mean(vg, axis=-1, keepdims=True)
            vc = vg - mu
            var = jnp.mean(vc * vc, axis=-1, keepdims=True)
            vn = (vc * lax.rsqrt(var + EPS)).astype(BF16)
            s = _dot(w_ref[g], vn) + b_ref[:, g:g + 1]
            o_ref[0, rows, cols] = (u[:, cols] * s).astype(BF16)


def _sgu(z, sgu_w_bf16, sgu_b_t, tc=256):
    B, L, _ = z.shape
    return pl.pallas_call(
        functools.partial(_sgu_kernel, tc=tc),
        grid=(B, L // tc),
        in_specs=[
            pl.BlockSpec((1, tc, SGU_DIM), lambda b, i: (b, i, 0)),
            pl.BlockSpec((1, tc, SGU_DIM), lambda b, i: (b, i, 1)),
            _const_spec((SGU_GROUPS, SGU_CHUNK, SGU_CHUNK)),
            _const_spec((SGU_CHUNK, SGU_GROUPS)),
        ],
        out_specs=pl.BlockSpec((1, tc, SGU_DIM), lambda b, i: (b, i, 0)),
        out_shape=jax.ShapeDtypeStruct((B, L, SGU_DIM), BF16),
        compiler_params=_cparams(("parallel", "parallel")),
        name="sgu",
    )(z, z, sgu_w_bf16, sgu_b_t)


def _hgrn_level_matrix(reverse):
    C = HGRN_CHUNK
    t = np.arange(C)[:, None]
    s = np.arange(C)[None, :]
    blocks = []
    if not reverse:
        blocks.append((s <= t).astype(np.float32))
        blocks.append((s > t).astype(np.float32))
    else:
        blocks.append((s >= t).astype(np.float32))
        blocks.append((s < t).astype(np.float32))
    for l in range(HGRN_LEVELS):
        m = C >> (l + 1)
        base = (t // (2 * m)) * (2 * m)
        if not reverse:
            r = base + m - 1
            qrow = (t % (2 * m)) >= m
            mq = (s > r) & (s <= t)
            mk = (s > t) & (s <= r)
        else:
            r = base + m
            qrow = (t % (2 * m)) < m
            mq = (s >= t) & (s < r)
            mk = (s >= r) & (s < t)
        blocks.append(np.where(qrow, mq, mk).astype(np.float32))
    return np.concatenate(blocks, axis=0)


def _hgrn_masks(reverse):
    C = HGRN_CHUNK
    t = lax.broadcasted_iota(jnp.int32, (C, C), 0)
    s = lax.broadcasted_iota(jnp.int32, (C, C), 1)
    masks = []
    for l in range(HGRN_LEVELS):
        m = C >> (l + 1)
        same = (t // (2 * m)) == (s // (2 * m))
        if not reverse:
            mk = same & ((t % (2 * m)) >= m) & ((s % (2 * m)) < m)
        else:
            mk = same & ((t % (2 * m)) < m) & ((s % (2 * m)) >= m)
        masks.append(mk)
    return t == s, masks


def _hgrn_chunk(zq, zf, zi, lb, st, lmat, eye, masks, reverse):
    C = HGRN_CHUNK
    f = lb + (1.0 - lb) * jax.nn.sigmoid(zf)
    g = jnp.log(f)
    kk = 1.0 - f
    q = _silu(zq)
    vb = zi.astype(BF16)
    g_hi, g_lo = _split2(g)
    d2 = _dot(lmat, jnp.concatenate([g_hi, g_lo], axis=1))
    e = jnp.exp(d2[:, :HGRN_KDIM] + d2[:, HGRN_KDIM:])
    qb = (q * e[0:C]).astype(BF16)
    kb = (kk * e[C:2 * C]).astype(BF16)
    dec = e[0:1] if reverse else e[C - 1:C]
    o = _dot_nt(qb, st.astype(BF16))
    a = jnp.where(eye, _dot_nt(q.astype(BF16), kk.astype(BF16)), 0.0)
    for l in range(HGRN_LEVELS):
        el = e[(2 + l) * C:(3 + l) * C]
        a = a + jnp.where(masks[l], _dot_nt((q * el).astype(BF16), (kk * el).astype(BF16)), 0.0)
    o = o + _dot(a.astype(BF16), vb)
    st_new = st * dec + _dot_tn(vb, kb)
    return o, st_new


def _hgrn_kernel(zqf_ref, zff_ref, zif_ref, zqb_ref, zfb_ref, zib_ref, lb_ref, s0_ref, lf_ref, lr_ref,
                 of_ref, ob_ref, sfin_ref, st_ref, *, tb):
    i = pl.program_id(1)
    nb = pl.num_programs(1)

    @pl.when(i == 0)
    def _():
        st_ref[...] = s0_ref[:, 0]

    C = HGRN_CHUNK
    nch = tb // C
    for d in range(2):
        reverse = d == 1
        eye, masks = _hgrn_masks(reverse)
        lmat = (lr_ref if reverse else lf_ref)[...]
        zq_ref, zf_ref, zi_ref, o_ref = ((zqb_ref, zfb_ref, zib_ref, ob_ref) if reverse
                                         else (zqf_ref, zff_ref, zif_ref, of_ref))
        for h in range(HGRN_HEADS):
            cols = slice(h * HGRN_KDIM, (h + 1) * HGRN_KDIM)
            lb = lb_ref[d:d + 1, cols]
            st = st_ref[d, h]
            order = range(nch - 1, -1, -1) if reverse else range(nch)
            for c in order:
                rows = slice(c * C, (c + 1) * C)
                o, st = _hgrn_chunk(zq_ref[0, rows, cols], zf_ref[0, rows, cols], zi_ref[0, rows, cols],
                                    lb, st, lmat, eye, masks, reverse)
                o_ref[0, rows, cols] = o
            st_ref[d, h] = st

    @pl.when(i == nb - 1)
    def _():
        sfin_ref[:, 0] = st_ref[...]


def _hgrn(z, lb, s0, lmat_f, lmat_r):
    B, L, _ = z.shape
    tb = min(HGRN_TBLK, L)
    nb = L // tb
    nl = lmat_f.shape[0]

    def zspec(col, rev):
        if rev:
            return pl.BlockSpec((1, tb, HGRN_DIM), lambda b, i: (b, nb - 1 - i, col))
        return pl.BlockSpec((1, tb, HGRN_DIM), lambda b, i: (b, i, col))

    st_spec = pl.BlockSpec((2, 1, HGRN_HEADS, HGRN_KDIM, HGRN_KDIM), lambda b, i: (0, b, 0, 0, 0))
    return pl.pallas_call(
        functools.partial(_hgrn_kernel, tb=tb),
        grid=(B, nb),
        in_specs=[zspec(2, False), zspec(3, False), zspec(5, False),
                  zspec(2, True), zspec(4, True), zspec(5, True),
                  _const_spec((2, HGRN_DIM)), st_spec,
                  _const_spec((nl, HGRN_CHUNK)), _const_spec((nl, HGRN_CHUNK))],
        out_specs=[pl.BlockSpec((1, tb, HGRN_DIM), lambda b, i: (b, i, 0)),
                   pl.BlockSpec((1, tb, HGRN_DIM), lambda b, i: (b, nb - 1 - i, 0)),
                   st_spec],
        out_shape=[jax.ShapeDtypeStruct((B, L, HGRN_DIM), F32),
                   jax.ShapeDtypeStruct((B, L, HGRN_DIM), F32),
                   jax.ShapeDtypeStruct(s0.shape, F32)],
        scratch_shapes=[pltpu.VMEM((2, HGRN_HEADS, HGRN_KDIM, HGRN_KDIM), F32)],
        compiler_params=_cparams(("parallel", "arbitrary")),
        name="hgrn2_scan",
    )(z, z, z, z, z, z, lb, s0, lmat_f, lmat_r)


def _residual_router_tail(x, y, g1, n2g, sh2, sc2, wr, xo_ref, h2_ref, aff_ref):
    xn = x + g1 * y
    xo_ref[0] = xn
    h2 = _rms_modulate(xn, n2g, sh2, sc2)
    h2_ref[0] = h2.astype(BF16)
    logits = _dot_f32x3(h2, wr)
    mx = jnp.max(logits, axis=-1, keepdims=True)
    p = jnp.exp(logits - mx)
    aff_ref[0] = p / jnp.sum(p, axis=-1, keepdims=True)


def _even_post_kernel(ya_ref, of_ref, ob_ref, zg_ref, hg_ref, w_ref, x_ref, g1_ref, n2g_ref, sh2_ref,
                      sc2_ref, wr_ref, xo_ref, h2_ref, aff_ref):
    o = of_ref[0] + ob_ref[0]
    zg = zg_ref[0]
    parts = []
    for h in range(HGRN_HEADS):
        cols = slice(h * HGRN_KDIM, (h + 1) * HGRN_KDIM)
        oh = o[:, cols]
        ms = jnp.mean(oh * oh, axis=-1, keepdims=True)
        parts.append(oh * lax.rsqrt(ms + EPS) * hg_ref[:, cols])
    yb = (jnp.concatenate(parts, axis=1) * _silu(zg)).astype(BF16)
    y = _dot(ya_ref[0], w_ref[:SGU_DIM, :]) + _dot(yb, w_ref[SGU_DIM:, :])
    _residual_router_tail(x_ref[0], y, g1_ref[0], n2g_ref[...], sh2_ref[0], sc2_ref[0], wr_ref[...],
                          xo_ref, h2_ref, aff_ref)


def _odd_post_kernel(o_ref, w_ref, x_ref, g1_ref, n2g_ref, sh2_ref, sc2_ref, wr_ref, xo_ref, h2_ref,
                     aff_ref):
    y = _dot(o_ref[0], w_ref[...])
    _residual_router_tail(x_ref[0], y, g1_ref[0], n2g_ref[...], sh2_ref[0], sc2_ref[0], wr_ref[...],
                          xo_ref, h2_ref, aff_ref)


def _post_out_specs(B, L, tm):
    D = D_MODEL
    specs = [_row_spec(tm, D), _row_spec(tm, D), _row_spec(tm, N_EXPERTS)]
    shapes = [jax.ShapeDtypeStruct((B, L, D), F32), jax.ShapeDtypeStruct((B, L, D), BF16),
              jax.ShapeDtypeStruct((B, L, N_EXPERTS), F32)]
    return specs, shapes


def _even_post(ya, o_f, o_b, z, hg, w_out, x, g1, n2g, sh2, sc2, wr, tm=256):
    B, L, D = x.shape
    specs, shapes = _post_out_specs(B, L, tm)
    return pl.pallas_call(
        _even_post_kernel,
        grid=(B, L // tm),
        in_specs=[_row_spec(tm, SGU_DIM), _row_spec(tm, HGRN_DIM), _row_spec(tm, HGRN_DIM),
                  pl.BlockSpec((1, tm, HGRN_DIM), lambda b, i: (b, i, 6)),
                  _const_spec((1, HGRN_DIM)), _const_spec((D, D)), _row_spec(tm, D), _vec_spec(D),
                  _const_spec((1, D)), _vec_spec(D), _vec_spec(D), _const_spec((D, N_EXPERTS))],
        out_specs=specs,
        out_shape=shapes,
        compiler_params=_cparams(("parallel", "parallel")),
        name="even_post_router",
    )(ya, o_f, o_b, z, hg, w_out, x, g1, n2g.reshape(1, D), sh2, sc2, wr)


def _odd_post(o, w_out, x, g1, n2g, sh2, sc2, wr, tm=256):
    B, L, D = x.shape
    specs, shapes = _post_out_specs(B, L, tm)
    return pl.pallas_call(
        _odd_post_kernel,
        grid=(B, L // tm),
        in_specs=[_row_spec(tm, ATTN_Q_DIM), _const_spec((ATTN_Q_DIM, D)), _row_spec(tm, D), _vec_spec(D),
                  _const_spec((1, D)), _vec_spec(D), _vec_spec(D), _const_spec((D, N_EXPERTS))],
        out_specs=specs,
        out_shape=shapes,
        compiler_params=_cparams(("parallel", "parallel")),
        name="odd_post_router",
    )(o, w_out, x, g1, n2g.reshape(1, D), sh2, sc2, wr)


def _rope(x, cos, sin_signed):
    lane = lax.broadcasted_iota(jnp.int32, (x.shape[0], 128), 1)
    first = (lane % 32) < 16
    outs = []
    for j in range(x.shape[1] // 128):
        xb = x[:, j * 128:(j + 1) * 128]
        sw = jnp.where(first, pltpu.roll(xb, 112, 1), pltpu.roll(xb, 16, 1))
        outs.append(xb * cos + sw * sin_signed)
    return jnp.concatenate(outs, axis=1)


def _qkv_lat_kernel(x_ref, m_ref, g2_ref, g_ref, sh_ref, sc_ref, w_ref, cos_ref, sin_ref,
                    q_ref, k_ref, v_ref, xo_ref):
    x = x_ref[0] + g2_ref[0] * m_ref[0]
    xo_ref[0] = x
    h = _rms_modulate(x, g_ref[...], sh_ref[0], sc_ref[0])
    z = _dot(h.astype(BF16), w_ref[...])
    cos = cos_ref[...]
    sin = sin_ref[...]
    scale = np.float32(HEAD_DIM ** -0.5)
    q_ref[0] = (_rope(z[:, :ATTN_Q_DIM], cos, sin) * scale).astype(BF16)
    k_ref[0] = _rope(z[:, ATTN_Q_DIM:ATTN_Q_DIM + ATTN_KV_DIM], cos, sin).astype(BF16)
    v_ref[0] = z[:, ATTN_Q_DIM + ATTN_KV_DIM:].astype(BF16)


def _qkv_lat(x, moe, g2, g, shift, scale, w, cos, sin, tm=256):
    B, L, D = x.shape
    n = w.shape[1]
    tab = pl.BlockSpec((tm, 128), lambda b, i: (i, 0))
    return pl.pallas_call(
        _qkv_lat_kernel,
        grid=(B, L // tm),
        in_specs=[_row_spec(tm, D), _row_spec(tm, D), _vec_spec(D), _const_spec((1, D)), _vec_spec(D),
                  _vec_spec(D), _const_spec((D, n)), tab, tab],
        out_specs=[_row_spec(tm, ATTN_Q_DIM), _row_spec(tm, ATTN_KV_DIM), _row_spec(tm, ATTN_KV_DIM),
                   _row_spec(tm, D)],
        out_shape=[jax.ShapeDtypeStruct((B, L, ATTN_Q_DIM), BF16),
                   jax.ShapeDtypeStruct((B, L, ATTN_KV_DIM), BF16),
                   jax.ShapeDtypeStruct((B, L, ATTN_KV_DIM), BF16),
                   jax.ShapeDtypeStruct((B, L, D), F32)],
        compiler_params=_cparams(("parallel", "parallel")),
        name="res_norm_qkv_rope",
    )(x, moe, g2, g.reshape(1, D), shift, scale, w, cos, sin)


def _attn_kernel(q_ref, kp_ref, kc_ref, kn_ref, vp_ref, vc_ref, vn_ref, kx_ref, vx_ref, sink_ref, o_ref,
                 *, n_tokens):
    bi = pl.program_id(1)
    blk = ATTN_BLOCK
    r_idx = lax.broadcasted_iota(jnp.int32, (blk, 3 * blk), 0)
    s_idx = lax.broadcasted_iota(jnp.int32, (blk, 3 * blk), 1)
    j = bi * blk - blk + s_idx
    valid = (jnp.abs(s_idx - r_idx - blk) <= WINDOW) & (j >= 0) & (j < n_tokens)
    for kvh in range(ATTN_KV_HEADS):
        kc = slice(kvh * HEAD_DIM, (kvh + 1) * HEAD_DIM)
        k_loc = jnp.concatenate([kp_ref[0, :, kc], kc_ref[0, :, kc], kn_ref[0, :, kc]], axis=0)
        v_loc = jnp.concatenate([vp_ref[0, :, kc], vc_ref[0, :, kc], vn_ref[0, :, kc]], axis=0)
        k_ctx = kx_ref[0, :, kc]
        v_ctx = vx_ref[0, :, kc]
        for gq in range(ATTN_GROUP):
            hd = kvh * ATTN_GROUP + gq
            qc = slice(hd * HEAD_DIM, (hd + 1) * HEAD_DIM)
            qh = q_ref[0, :, qc]
            s_loc = jnp.where(valid, _dot_nt(qh, k_loc), NEG_BIG)
            s_ctx = _dot_nt(qh, k_ctx)
            sink = sink_ref[:, hd:hd + 1]
            m = jnp.maximum(jnp.maximum(jnp.max(s_loc, axis=-1, keepdims=True),
                                        jnp.max(s_ctx, axis=-1, keepdims=True)), sink)
            p_loc = jnp.exp(s_loc - m)
            p_ctx = jnp.exp(s_ctx - m)
            den = (jnp.sum(p_loc, axis=-1, keepdims=True) + jnp.sum(p_ctx, axis=-1, keepdims=True)
                   + jnp.exp(sink - m))
            acc = _dot(p_loc.astype(BF16), v_loc) + _dot(p_ctx.astype(BF16), v_ctx)
            o_ref[0, :, qc] = (acc / den).astype(BF16)


def _attention(q, k, v, k_ctx, v_ctx, sink):
    B, N, _ = q.shape
    Lc = k_ctx.shape[1]
    nb = N // ATTN_BLOCK
    blk = ATTN_BLOCK

    def kv_spec(off):
        return pl.BlockSpec((1, blk, ATTN_KV_DIM),
                            lambda b, i: (b, jnp.clip(i + off, 0, nb - 1), 0))

    ctx_spec = pl.BlockSpec((1, Lc, ATTN_KV_DIM), lambda b, i: (b, 0, 0))
    return pl.pallas_call(
        functools.partial(_attn_kernel, n_tokens=N),
        grid=(B, nb),
        in_specs=[_row_spec(blk, ATTN_Q_DIM), kv_spec(-1), kv_spec(0), kv_spec(1),
                  kv_spec(-1), kv_spec(0), kv_spec(1), ctx_spec, ctx_spec,
                  _const_spec((1, ATTN_HEADS))],
        out_specs=_row_spec(blk, ATTN_Q_DIM),
        out_shape=jax.ShapeDtypeStruct((B, N, ATTN_Q_DIM), BF16),
        compiler_params=_cparams(("parallel", "parallel")),
        name="window_attention",
    )(q, k, k, k, v, v, v, k_ctx, v_ctx, sink.reshape(1, ATTN_HEADS))


def _expert_kernel(x_ref, w1_ref, w3_ref, w2_ref, gate_ref, o_ref):
    f = pl.program_id(2)
    x = x_ref[0]
    a = _dot(x, w1_ref[0])
    u = _dot(x, w3_ref[0])
    y = _dot((_silu(a) * u).astype(BF16), w2_ref[0])

    @pl.when(f == 0)
    def _():
        o_ref[0] = y

    @pl.when(f > 0)
    def _():
        o_ref[0] += y

    @pl.when(f == pl.num_programs(2) - 1)
    def _():
        o_ref[0] = o_ref[0] * gate_ref[0]


def _experts(xe, w1, w3, w2, gate, tm, tf=512):
    E, M, D = xe.shape
    F = w1.shape[2]
    return pl.pallas_call(
        _expert_kernel,
        grid=(E, M // tm, F // tf),
        in_specs=[pl.BlockSpec((1, tm, D), lambda e, m, f: (e, m, 0)),
                  pl.BlockSpec((1, D, tf), lambda e, m, f: (e, 0, f)),
                  pl.BlockSpec((1, D, tf), lambda e, m, f: (e, 0, f)),
                  pl.BlockSpec((1, tf, D), lambda e, m, f: (e, f, 0)),
                  pl.BlockSpec((1, tm, 1), lambda e, m, f: (e, m, 0))],
        out_specs=pl.BlockSpec((1, tm, D), lambda e, m, f: (e, m, 0)),
        out_shape=jax.ShapeDtypeStruct((E, M, D), F32),
        compiler_params=_cparams(("parallel", "parallel", "arbitrary")),
        name="expert_ffn",
    )(xe, w1, w3, w2, gate)


def _moe(h2, aff, w1, w3, w2):
    B, L, D = h2.shape
    cap = CAPACITY_FACTOR * L // N_EXPERTS
    gate, idx = lax.top_k(jnp.swapaxes(aff, 1, 2), cap)
    xe = jax.vmap(lambda hb, ib: hb[ib])(h2, idx)
    xe = xe.transpose(1, 0, 2, 3).reshape(N_EXPERTS, B * cap, D)
    ge = gate.transpose(1, 0, 2).reshape(N_EXPERTS, B * cap, 1)
    y = _experts(xe, w1, w3, w2, ge, tm=min(1024, B * cap))
    y = y.reshape(N_EXPERTS, B, cap, D).transpose(1, 0, 2, 3)
    return jax.vmap(lambda yb, ib: jnp.zeros((L, D), yb.dtype).at[ib.reshape(-1)].add(yb.reshape(-1, D)))(y, idx)


def _final_kernel(x_ref, m_ref, g2_ref, g_ref, o_ref):
    x = x_ref[0] + g2_ref[0] * m_ref[0]
    ms = jnp.mean(x * x, axis=-1, keepdims=True)
    o_ref[0] = x * lax.rsqrt(ms + EPS) * g_ref[...]


def _final(x, moe, g2, g, tm=512):
    B, L, D = x.shape
    return pl.pallas_call(
        _final_kernel,
        grid=(B, L // tm),
        in_specs=[_row_spec(tm, D), _row_spec(tm, D), _vec_spec(D), _const_spec((1, D))],
        out_specs=_row_spec(tm, D),
        out_shape=jax.ShapeDtypeStruct((B, L, D), F32),
        compiler_params=_cparams(("parallel", "parallel")),
        name="final_norm",
    )(x, moe, g2, g.reshape(1, D))


def _rope_tables(n):
    half = HEAD_DIM // 2
    quarter = half // 2
    inv = ROPE_BASE ** (-jnp.arange(quarter, dtype=F32) / quarter)
    t = jnp.arange(n)
    row = (t // GRID_W).astype(F32)
    col = (t % GRID_W).astype(F32)
    lane = np.arange(128)
    within = lane % HEAD_DIM
    use_col = jnp.asarray((within >= half)[None, :])
    freq = inv[jnp.asarray(within % quarter)][None, :]
    ang = jnp.where(use_col, col[:, None], row[:, None]) * freq
    sign = jnp.asarray(np.where((lane % half) < quarter, -1.0, 1.0).astype(np.float32))[None, :]
    return jnp.cos(ang), jnp.sin(ang) * sign


def kernel(x, c, ctx, c_ctx, mod_w, mod_b, norm1_g, norm2_g, ab_w_in, ab_w_out, sgu_w, sgu_b, hgrn_lb_logits,
           hgrn_norm_g, attn_w_qkv, attn_w_out, attn_sink, router_w, expert_w1, expert_w3, expert_w2, final_g):
    B, N, D = x.shape
    Lc = ctx.shape[1]
    depth = mod_w.shape[0]

    rows = ((B + 1 + 7) // 8) * 8
    cvec = jnp.zeros((rows, D), F32).at[:B].set(c).at[B].set(c_ctx)
    mod = _mod_vectors(cvec, mod_w, mod_b)

    def mods(l):
        lat = [mod[l, :B, k * D:(k + 1) * D].reshape(B, 1, D) for k in range(6)]
        cx = [jnp.broadcast_to(mod[l, B, k * D:(k + 1) * D].reshape(1, 1, D), (B, 1, D)) for k in range(6)]
        return lat, cx

    lb_all = jnp.cumsum(jax.nn.softmax(hgrn_lb_logits.astype(F32), axis=0), axis=0)
    lmat_f = jnp.asarray(_hgrn_level_matrix(False), BF16)
    lmat_r = jnp.asarray(_hgrn_level_matrix(True), BF16)

    x_lat, x_ctx = x, ctx
    moe_lat = moe_ctx = None
    g2_lat = g2_ctx = None
    for l in range(depth):
        last = l == depth - 1
        (sh1, sc1, g1, sh2, sc2, g2), (csh1, csc1, cg1, csh2, csc2, cg2) = mods(l)
        w1 = expert_w1[l].astype(BF16)
        w3 = expert_w3[l].astype(BF16)
        w2 = expert_w2[l].astype(BF16)
        wr = router_w[l]
        if l % 2 == 0:
            e = l // 2
            w_in = ab_w_in[e].astype(BF16)
            w_out = ab_w_out[e].astype(BF16)
            sw = sgu_w[e].astype(BF16)
            sb_t = sgu_b[e].T
            hg = hgrn_norm_g[e].reshape(1, HGRN_DIM)
            s0 = jnp.zeros((2, B, HGRN_HEADS, HGRN_KDIM, HGRN_KDIM), F32)

            def even(xs, moe, g2p, shift, scale, gate1, shift2, scale2, s_in):
                if moe is None:
                    z = _proj(xs, norm1_g[l], shift, scale, w_in)
                else:
                    z, xs = _proj_res(xs, moe, g2p, norm1_g[l], shift, scale, w_in)
                ya = _sgu(z, sw, sb_t)
                o_f, o_b, s_fin = _hgrn(z, lb_all[e], s_in, lmat_f, lmat_r)
                xn, h2, aff = _even_post(ya, o_f, o_b, z, hg, w_out, xs, gate1, norm2_g[l], shift2, scale2, wr)
                return xn, h2, aff, s_fin

            x_ctx, h2c, affc, s_ctx = even(x_ctx, moe_ctx, g2_ctx, csh1, csc1, cg1, csh2, csc2, s0)
            x_lat, h2l, affl, _ = even(x_lat, moe_lat, g2_lat, sh1, sc1, g1, sh2, sc2, s_ctx)
        else:
            o_idx = l // 2
            w_qkv = attn_w_qkv[o_idx].astype(BF16)
            w_out = attn_w_out[o_idx].astype(BF16)
            cos, sin = _rope_tables(N)
            if moe_ctx is None:
                zc = _proj(x_ctx, norm1_g[l], csh1, csc1, w_qkv)
            else:
                zc, x_ctx = _proj_res(x_ctx, moe_ctx, g2_ctx, norm1_g[l], csh1, csc1, w_qkv)
            k_ctx = zc[..., ATTN_Q_DIM:ATTN_Q_DIM + ATTN_KV_DIM].astype(BF16)
            v_ctx = zc[..., ATTN_Q_DIM + ATTN_KV_DIM:].astype(BF16)
            if moe_lat is None:
                moe_in, g2_in = jnp.zeros_like(x_lat), jnp.zeros((B, 1, D), F32)
            else:
                moe_in, g2_in = moe_lat, g2_lat
            q, k, v, x_lat = _qkv_lat(x_lat, moe_in, g2_in, norm1_g[l], sh1, sc1, w_qkv, cos, sin)
            o = _attention(q, k, v, k_ctx, v_ctx, attn_sink[o_idx])
            x_lat, h2l, affl = _odd_post(o, w_out, x_lat, g1, norm2_g[l], sh2, sc2, wr)
            if not last:
                raise NotImplementedError("context update after an attention layer")
        moe_lat, g2_lat = _moe(h2l, affl, w1, w3, w2), g2
        if not last:
            if l % 2 == 0:
                moe_ctx, g2_ctx = _moe(h2c, affc, w1, w3, w2), cg2
    return _final(x_lat, moe_lat, g2_lat, final_g)
```

```python
import functools

import numpy as np
import jax
import jax.numpy as jnp
from jax import lax
from jax.experimental import pallas as pl
from jax.experimental.pallas import tpu as pltpu

F32 = jnp.float32
BF16 = jnp.bfloat16

D_MODEL = 1024
GRID_W = 64
EPS = 1e-6
SGU_GROUPS = 4
SGU_GROUP_DIM = 128
SGU_DIM = SGU_GROUPS * SGU_GROUP_DIM
SGU_CHUNK = 128
HGRN_HEADS = 4
HGRN_KDIM = 128
HGRN_DIM = HGRN_HEADS * HGRN_KDIM
HGRN_CHUNK = 64
HGRN_LEVELS = 6
HGRN_TBLK = 256
AB_IN = 2 * SGU_DIM + 5 * HGRN_DIM
ATTN_HEADS = 16
ATTN_KV_HEADS = 4
ATTN_GROUP = ATTN_HEADS // ATTN_KV_HEADS
HEAD_DIM = 64
ATTN_Q_DIM = ATTN_HEADS * HEAD_DIM
ATTN_KV_DIM = ATTN_KV_HEADS * HEAD_DIM
WINDOW = 128
ATTN_BLOCK = 128
ROPE_BASE = 10000.0
N_EXPERTS = 16
CAPACITY_FACTOR = 2
EXPERT_FF = 2048
NEG_BIG = -1e30

VMEM_LIMIT_BYTES = 56 * 1024 * 1024


def _cparams(sem):
    return pltpu.CompilerParams(dimension_semantics=sem, vmem_limit_bytes=VMEM_LIMIT_BYTES)


def _dot(a, b):
    return jnp.dot(a, b, preferred_element_type=F32)


def _dot_nt(a, b):
    return lax.dot_general(a, b, (((1,), (1,)), ((), ())), preferred_element_type=F32)


def _dot_tn(a, b):
    return lax.dot_general(a, b, (((0,), (0,)), ((), ())), preferred_element_type=F32)


def _split2(a):
    hi = a.astype(BF16)
    lo = (a - hi.astype(F32)).astype(BF16)
    return hi, lo


def _dot_f32x3(a, b):
    a_hi, a_lo = _split2(a)
    b_hi, b_lo = _split2(b)
    return _dot(a_hi, b_hi) + (_dot(a_hi, b_lo) + _dot(a_lo, b_hi))


def _silu(x):
    return x * jax.nn.sigmoid(x)


def _gelu(x):
    return 0.5 * x * (1.0 + lax.erf(x * np.float32(1.0 / np.sqrt(2.0))))


def _rms_modulate(x, g, shift, scale):
    ms = jnp.mean(x * x, axis=-1, keepdims=True)
    return (x * lax.rsqrt(ms + EPS) * g) * (1.0 + scale) + shift


def _mod_kernel(c_ref, w_ref, b_ref, o_ref):
    s = _silu(c_ref[...])
    o_ref[0] = _dot_f32x3(s, w_ref[0]) + b_ref[0]


def _mod_vectors(cvec, mod_w, mod_b):
    depth, d, n6 = mod_w.shape
    rows = cvec.shape[0]
    tn = 1536
    return pl.pallas_call(
        _mod_kernel,
        grid=(depth, n6 // tn),
        in_specs=[
            pl.BlockSpec((rows, d), lambda l, j: (0, 0)),
            pl.BlockSpec((1, d, tn), lambda l, j: (l, 0, j)),
            pl.BlockSpec((1, 1, tn), lambda l, j: (l, 0, j)),
        ],
        out_specs=pl.BlockSpec((1, rows, tn), lambda l, j: (l, 0, j)),
        out_shape=jax.ShapeDtypeStruct((depth, rows, n6), F32),
        compiler_params=_cparams(("arbitrary", "arbitrary")),
        name="mod_vectors",
    )(cvec, mod_w, mod_b.reshape(depth, 1, n6))


def _proj_kernel(x_ref, g_ref, sh_ref, sc_ref, w_ref, z_ref):
    h = _rms_modulate(x_ref[0], g_ref[...], sh_ref[0], sc_ref[0])
    z_ref[0] = _dot(h.astype(BF16), w_ref[...])


def _row_spec(tm, d):
    return pl.BlockSpec((1, tm, d), lambda b, i: (b, i, 0))


def _vec_spec(d):
    return pl.BlockSpec((1, 1, d), lambda b, i: (b, 0, 0))


def _const_spec(shape):
    nd = len(shape)
    return pl.BlockSpec(shape, lambda b, i: (0,) * nd)


def _proj(x, g, shift, scale, w, tm=256):
    B, L, D = x.shape
    n = w.shape[1]
    return pl.pallas_call(
        _proj_kernel,
        grid=(B, L // tm),
        in_specs=[_row_spec(tm, D), _const_spec((1, D)), _vec_spec(D), _vec_spec(D), _const_spec((D, n))],
        out_specs=_row_spec(tm, n),
        out_shape=jax.ShapeDtypeStruct((B, L, n), F32),
        compiler_params=_cparams(("parallel", "parallel")),
        name="norm_mod_proj",
    )(x, g.reshape(1, D), shift, scale, w)


def _sgu_kernel(zu_ref, zv_ref, w_ref, b_ref, o_ref, *, tc):
    for ch in range(tc // SGU_CHUNK):
        rows = slice(ch * SGU_CHUNK, (ch + 1) * SGU_CHUNK)
        u = _gelu(zu_ref[0, rows, :])
        v = _gelu(zv_ref[0, rows, :])
        for g in range(SGU_GROUPS):
            cols = slice(g * SGU_GROUP_DIM, (g + 1) * SGU_GROUP_DIM)
            vg = v[:, cols]
            mu = jnp.mean(vg, axis=-1, keepdims=True)
            vc = vg - mu
            var = jnp.mean(vc * vc, axis=-1, keepdims=True)
            vn = (vc * lax.rsqrt(var + EPS)).astype(BF16)
            s = _dot(w_ref[g], vn) + b_ref[:, g:g + 1]
            o_ref[0, rows, cols] = (u[:, cols] * s).astype(BF16)


def _sgu(z, sgu_w_bf16, sgu_b_t, tc=256):
    B, L, _ = z.shape
    return pl.pallas_call(
        functools.partial(_sgu_kernel, tc=tc),
        grid=(B, L // tc),
        in_specs=[
            pl.BlockSpec((1, tc, SGU_DIM), lambda b, i: (b, i, 0)),
            pl.BlockSpec((1, tc, SGU_DIM), lambda b, i: (b, i, 1)),
            _const_spec((SGU_GROUPS, SGU_CHUNK, SGU_CHUNK)),
            _const_spec((SGU_CHUNK, SGU_GROUPS)),
        ],
        out_specs=pl.BlockSpec((1, tc, SGU_DIM), lambda b, i: (b, i, 0)),
        out_shape=jax.ShapeDtypeStruct((B, L, SGU_DIM), BF16),
        compiler_params=_cparams(("parallel", "parallel")),
        name="sgu",
    )(z, z, sgu_w_bf16, sgu_b_t)


def _hgrn_level_matrix(reverse):
    C = HGRN_CHUNK
    t = np.arange(C)[:, None]
    s = np.arange(C)[None, :]
    blocks = []
    if not reverse:
        blocks.append((s <= t).astype(np.float32))
        blocks.append((s > t).astype(np.float32))
    else:
        blocks.append((s >= t).astype(np.float32))
        blocks.append((s < t).astype(np.float32))
    for l in range(HGRN_LEVELS):
        m = C >> (l + 1)
        base = (t // (2 * m)) * (2 * m)
        if not reverse:
            r = base + m - 1
            qrow = (t % (2 * m)) >= m
            mq = (s > r) & (s <= t)
            mk = (s > t) & (s <= r)
        else:
            r = base + m
            qrow = (t % (2 * m)) < m
            mq = (s >= t) & (s < r)
            mk = (s >= r) & (s < t)
        blocks.append(np.where(qrow, mq, mk).astype(np.float32))
    return np.concatenate(blocks, axis=0)


def _hgrn_masks(reverse):
    C = HGRN_CHUNK
    t = lax.broadcasted_iota(jnp.int32, (C, C), 0)
    s = lax.broadcasted_iota(jnp.int32, (C, C), 1)
    r = lax.broadcasted_iota(jnp.int32, (C, 1), 0)
    masks, qrows = [], []
    for l in range(HGRN_LEVELS):
        m = C >> (l + 1)
        same = (t // (2 * m)) == (s // (2 * m))
        if not reverse:
            mk = same & ((t % (2 * m)) >= m) & ((s % (2 * m)) < m)
            qr = (r % (2 * m)) >= m
        else:
            mk = same & ((t % (2 * m)) < m) & ((s % (2 * m)) >= m)
            qr = (r % (2 * m)) < m
        masks.append(mk)
        qrows.append(qr)
    return t == s, masks, qrows


def _hgrn_chunk(zq, zf, zi, lb, st, lmat, eye, masks, qrows, reverse):
    C = HGRN_CHUNK
    f = lb + (1.0 - lb) * jax.nn.sigmoid(zf)
    g = jnp.log(f)
    kk = 1.0 - f
    q = _silu(zq)
    vb = zi.astype(BF16)
    g_hi, g_lo = _split2(g)
    d2 = _dot(lmat, jnp.concatenate([g_hi, g_lo], axis=1))
    e = jnp.exp(d2[:, :HGRN_KDIM] + d2[:, HGRN_KDIM:])
    qb = (q * e[0:C]).astype(BF16)
    kb = (kk * e[C:2 * C]).astype(BF16)
    dec = e[0:1] if reverse else e[C - 1:C]
    o = _dot_nt(qb, st.astype(BF16))
    a = jnp.where(eye, _dot_nt(q.astype(BF16), kk.astype(BF16)), 0.0)
    for l in range(HGRN_LEVELS):
        mix = (jnp.where(qrows[l], q, kk) * e[(2 + l) * C:(3 + l) * C]).astype(BF16)
        a = a + jnp.where(masks[l], _dot_nt(mix, mix), 0.0)
    o = o + _dot(a.astype(BF16), vb)
    st_new = st * dec + _dot_tn(vb, kb)
    return o, st_new


def _hgrn_kernel(zqf_ref, zff_ref, zif_ref, zqb_ref, zfb_ref, zib_ref, lb_ref, s0_ref, lf_ref, lr_ref,
                 of_ref, ob_ref, sfin_ref, st_ref, *, tb):
    i = pl.program_id(1)
    nb = pl.num_programs(1)

    @pl.when(i == 0)
    def _():
        st_ref[...] = s0_ref[:, 0]

    C = HGRN_CHUNK
    nch = tb // C
    for d in range(2):
        reverse = d == 1
        eye, masks, qrows = _hgrn_masks(reverse)
        lmat = (lr_ref if reverse else lf_ref)[...]
        zq_ref, zf_ref, zi_ref, o_ref = ((zqb_ref, zfb_ref, zib_ref, ob_ref) if reverse
                                         else (zqf_ref, zff_ref, zif_ref, of_ref))
        for h in range(HGRN_HEADS):
            cols = slice(h * HGRN_KDIM, (h + 1) * HGRN_KDIM)
            lb = lb_ref[d:d + 1, cols]
            st = st_ref[d, h]
            order = range(nch - 1, -1, -1) if reverse else range(nch)
            for c in order:
                rows = slice(c * C, (c + 1) * C)
                o, st = _hgrn_chunk(zq_ref[0, rows, cols], zf_ref[0, rows, cols], zi_ref[0, rows, cols],
                                    lb, st, lmat, eye, masks, qrows, reverse)
                o_ref[0, rows, cols] = o
            st_ref[d, h] = st

    @pl.when(i == nb - 1)
    def _():
        sfin_ref[:, 0] = st_ref[...]


def _hgrn(z, lb, s0, lmat_f, lmat_r):
    B, L, _ = z.shape
    tb = min(HGRN_TBLK, L)
    nb = L // tb
    nl = lmat_f.shape[0]

    def zspec(col, rev):
        if rev:
            return pl.BlockSpec((1, tb, HGRN_DIM), lambda b, i: (b, nb - 1 - i, col))
        return pl.BlockSpec((1, tb, HGRN_DIM), lambda b, i: (b, i, col))

    st_spec = pl.BlockSpec((2, 1, HGRN_HEADS, HGRN_KDIM, HGRN_KDIM), lambda b, i: (0, b, 0, 0, 0))
    return pl.pallas_call(
        functools.partial(_hgrn_kernel, tb=tb),
        grid=(B, nb),
        in_specs=[zspec(2, False), zspec(3, False), zspec(5, False),
                  zspec(2, True), zspec(4, True), zspec(5, True),
                  _const_spec((2, HGRN_DIM)), st_spec,
                  _const_spec((nl, HGRN_CHUNK)), _const_spec((nl, HGRN_CHUNK))],
        out_specs=[pl.BlockSpec((1, tb, HGRN_DIM), lambda b, i: (b, i, 0)),
                   pl.BlockSpec((1, tb, HGRN_DIM), lambda b, i: (b, nb - 1 - i, 0)),
                   st_spec],
        out_shape=[jax.ShapeDtypeStruct((B, L, HGRN_DIM), F32),
                   jax.ShapeDtypeStruct((B, L, HGRN_DIM), F32),
                   jax.ShapeDtypeStruct(s0.shape, F32)],
        scratch_shapes=[pltpu.VMEM((2, HGRN_HEADS, HGRN_KDIM, HGRN_KDIM), F32)],
        compiler_params=_cparams(("parallel", "arbitrary")),
        name="hgrn2_scan",
    )(z, z, z, z, z, z, lb, s0, lmat_f, lmat_r)


def _residual_router_tail(x, y, g1, n2g, sh2, sc2, wr, xo_ref, h2_ref, aff_ref):
    xn = x + g1 * y
    xo_ref[0] = xn
    h2 = _rms_modulate(xn, n2g, sh2, sc2)
    h2_ref[0] = h2.astype(BF16)
    w_hi, w_lo = _split2(wr)
    h_hi, h_lo = _split2(h2)
    logits = _dot_nt(w_hi, h_hi) + (_dot_nt(w_hi, h_lo) + _dot_nt(w_lo, h_hi))
    mx = jnp.max(logits, axis=0, keepdims=True)
    p = jnp.exp(logits - mx)
    aff_ref[0] = p / jnp.sum(p, axis=0, keepdims=True)


def _even_post_kernel(ya_ref, of_ref, ob_ref, zg_ref, hg_ref, w_ref, x_ref, g1_ref, n2g_ref, sh2_ref,
                      sc2_ref, wr_ref, xo_ref, h2_ref, aff_ref):
    o = of_ref[0] + ob_ref[0]
    zg = zg_ref[0]
    parts = []
    for h in range(HGRN_HEADS):
        cols = slice(h * HGRN_KDIM, (h + 1) * HGRN_KDIM)
        oh = o[:, cols]
        ms = jnp.mean(oh * oh, axis=-1, keepdims=True)
        parts.append(oh * lax.rsqrt(ms + EPS) * hg_ref[:, cols])
    yb = (jnp.concatenate(parts, axis=1) * _silu(zg)).astype(BF16)
    y = _dot(ya_ref[0], w_ref[:SGU_DIM, :]) + _dot(yb, w_ref[SGU_DIM:, :])
    _residual_router_tail(x_ref[0], y, g1_ref[0], n2g_ref[...], sh2_ref[0], sc2_ref[0], wr_ref[...],
                          xo_ref, h2_ref, aff_ref)


def _odd_post_kernel(o_ref, w_ref, x_ref, g1_ref, n2g_ref, sh2_ref, sc2_ref, wr_ref, xo_ref, h2_ref,
                     aff_ref):
    y = _dot(o_ref[0], w_ref[...])
    _residual_router_tail(x_ref[0], y, g1_ref[0], n2g_ref[...], sh2_ref[0], sc2_ref[0], wr_ref[...],
                          xo_ref, h2_ref, aff_ref)


def _post_out_specs(B, L, tm):
    D = D_MODEL
    specs = [_row_spec(tm, D), _row_spec(tm, D), pl.BlockSpec((1, N_EXPERTS, tm), lambda b, i: (b, 0, i))]
    shapes = [jax.ShapeDtypeStruct((B, L, D), F32), jax.ShapeDtypeStruct((B, L, D), BF16),
              jax.ShapeDtypeStruct((B, N_EXPERTS, L), F32)]
    return specs, shapes


def _even_post(ya, o_f, o_b, z, hg, w_out, x, g1, n2g, sh2, sc2, wr, tm=256):
    B, L, D = x.shape
    specs, shapes = _post_out_specs(B, L, tm)
    return pl.pallas_call(
        _even_post_kernel,
        grid=(B, L // tm),
        in_specs=[_row_spec(tm, SGU_DIM), _row_spec(tm, HGRN_DIM), _row_spec(tm, HGRN_DIM),
                  pl.BlockSpec((1, tm, HGRN_DIM), lambda b, i: (b, i, 6)),
                  _const_spec((1, HGRN_DIM)), _const_spec((D, D)), _row_spec(tm, D), _vec_spec(D),
                  _const_spec((1, D)), _vec_spec(D), _vec_spec(D), _const_spec((N_EXPERTS, D))],
        out_specs=specs,
        out_shape=shapes,
        compiler_params=_cparams(("parallel", "parallel")),
        name="even_post_router",
    )(ya, o_f, o_b, z, hg, w_out, x, g1, n2g.reshape(1, D), sh2, sc2, wr)


def _odd_post(o, w_out, x, g1, n2g, sh2, sc2, wr, tm=256):
    B, L, D = x.shape
    specs, shapes = _post_out_specs(B, L, tm)
    return pl.pallas_call(
        _odd_post_kernel,
        grid=(B, L // tm),
        in_specs=[_row_spec(tm, ATTN_Q_DIM), _const_spec((ATTN_Q_DIM, D)), _row_spec(tm, D), _vec_spec(D),
                  _const_spec((1, D)), _vec_spec(D), _vec_spec(D), _const_spec((N_EXPERTS, D))],
        out_specs=specs,
        out_shape=shapes,
        compiler_params=_cparams(("parallel", "parallel")),
        name="odd_post_router",
    )(o, w_out, x, g1, n2g.reshape(1, D), sh2, sc2, wr)


def _rope(x, cos, sin_signed):
    lane = lax.broadcasted_iota(jnp.int32, (x.shape[0], 128), 1)
    first = (lane % 32) < 16
    outs = []
    for j in range(x.shape[1] // 128):
        xb = x[:, j * 128:(j + 1) * 128]
        sw = jnp.where(first, pltpu.roll(xb, 112, 1), pltpu.roll(xb, 16, 1))
        outs.append(xb * cos + sw * sin_signed)
    return jnp.concatenate(outs, axis=1)


def _qkv_lat_kernel(x_ref, g_ref, sh_ref, sc_ref, w_ref, cos_ref, sin_ref, q_ref, k_ref, v_ref):
    h = _rms_modulate(x_ref[0], g_ref[...], sh_ref[0], sc_ref[0])
    z = _dot(h.astype(BF16), w_ref[...])
    cos = cos_ref[...]
    sin = sin_ref[...]
    scale = np.float32(HEAD_DIM ** -0.5)
    q_ref[0] = (_rope(z[:, :ATTN_Q_DIM], cos, sin) * scale).astype(BF16)
    k_ref[0] = _rope(z[:, ATTN_Q_DIM:ATTN_Q_DIM + ATTN_KV_DIM], cos, sin).astype(BF16)
    v_ref[0] = z[:, ATTN_Q_DIM + ATTN_KV_DIM:].astype(BF16)


def _qkv_lat(x, g, shift, scale, w, cos, sin, tm=256):
    B, L, D = x.shape
    n = w.shape[1]
    tab = pl.BlockSpec((tm, 128), lambda b, i: (i, 0))
    return pl.pallas_call(
        _qkv_lat_kernel,
        grid=(B, L // tm),
        in_specs=[_row_spec(tm, D), _const_spec((1, D)), _vec_spec(D), _vec_spec(D), _const_spec((D, n)), tab, tab],
        out_specs=[_row_spec(tm, ATTN_Q_DIM), _row_spec(tm, ATTN_KV_DIM), _row_spec(tm, ATTN_KV_DIM)],
        out_shape=[jax.ShapeDtypeStruct((B, L, ATTN_Q_DIM), BF16),
                   jax.ShapeDtypeStruct((B, L, ATTN_KV_DIM), BF16),
                   jax.ShapeDtypeStruct((B, L, ATTN_KV_DIM), BF16)],
        compiler_params=_cparams(("parallel", "parallel")),
        name="norm_qkv_rope",
    )(x, g.reshape(1, D), shift, scale, w, cos, sin)


def _attn_kernel(q_ref, kp_ref, kc_ref, kn_ref, vp_ref, vc_ref, vn_ref, kx_ref, vx_ref, sink_ref, o_ref,
                 *, n_tokens):
    bi = pl.program_id(1)
    blk = ATTN_BLOCK
    n_loc = 3 * blk
    nk = n_loc + kx_ref.shape[1]
    n_pairs = ATTN_GROUP // 2
    nq = n_pairs * blk
    pad = 16
    s_idx = lax.broadcasted_iota(jnp.int32, (n_loc, nq), 0)
    r_idx = lax.broadcasted_iota(jnp.int32, (n_loc, nq), 1) % blk
    j = bi * blk - blk + s_idx
    valid = (jnp.abs(s_idx - r_idx - blk) <= WINDOW) & (j >= 0) & (j < n_tokens)
    lane = lax.broadcasted_iota(jnp.int32, (nk, 128), 1)
    qcol = lax.broadcasted_iota(jnp.int32, (1, nq), 1)
    prow = lax.broadcasted_iota(jnp.int32, (pad, nq), 0)
    er = lax.broadcasted_iota(jnp.int32, (2 * nk + pad, 128), 0)
    el = lax.broadcasted_iota(jnp.int32, (2 * nk + pad, 128), 1)
    lo_rows = (er < nk) | (er == 2 * nk)
    hi_rows = ((er >= nk) & (er < 2 * nk)) | (er == 2 * nk + 1)
    ind = jnp.where((lo_rows & (el < HEAD_DIM)) | (hi_rows & (el >= HEAD_DIM)), 1.0, 0.0).astype(BF16)
    zpad = jnp.zeros((pad, 128), BF16)
    for kvh in range(ATTN_KV_HEADS):
        cols = slice((kvh // 2) * 128, (kvh // 2 + 1) * 128)
        own = (lane < HEAD_DIM) if kvh % 2 == 0 else (lane >= HEAD_DIM)
        k_all = jnp.concatenate([kp_ref[0, :, cols], kc_ref[0, :, cols], kn_ref[0, :, cols], kx_ref[0, :, cols]],
                                axis=0)
        v_all = jnp.concatenate([vp_ref[0, :, cols], vc_ref[0, :, cols], vn_ref[0, :, cols], vx_ref[0, :, cols]],
                                axis=0)
        k_own = jnp.where(own, k_all, jnp.zeros_like(k_all))
        v_own = jnp.where(own, v_all, jnp.zeros_like(v_all))
        k_oth = pltpu.roll(k_own, HEAD_DIM, 1)
        v_oth = pltpu.roll(v_own, HEAD_DIM, 1)
        if kvh % 2 == 0:
            k_cat = jnp.concatenate([k_own, k_oth], axis=0)
            v_cat = jnp.concatenate([v_own, v_oth, zpad], axis=0)
        else:
            k_cat = jnp.concatenate([k_oth, k_own], axis=0)
            v_cat = jnp.concatenate([v_oth, v_own, zpad], axis=0)
        q2 = jnp.concatenate([q_ref[0, :, (kvh * n_pairs + p) * 128:(kvh * n_pairs + p + 1) * 128]
                              for p in range(n_pairs)], axis=0)
        st = _dot_nt(k_cat, q2)
        ps, es = [], []
        for sub in range(2):
            sh = st[sub * nk:(sub + 1) * nk]
            sh = jnp.concatenate([jnp.where(valid, sh[:n_loc], NEG_BIG), sh[n_loc:]], axis=0)
            hd0 = kvh * ATTN_GROUP + sub
            sink = sink_ref[:, hd0:hd0 + 1]
            for p in range(1, n_pairs):
                sink = jnp.where(qcol < p * blk, sink, sink_ref[:, hd0 + 2 * p:hd0 + 2 * p + 1])
            m = jnp.maximum(jnp.max(sh, axis=0, keepdims=True), sink)
            ps.append(jnp.exp(sh - m).astype(BF16))
            es.append(jnp.exp(sink - m))
        sink_rows = jnp.where(prow == 0, es[0], jnp.where(prow == 1, es[1], 0.0)).astype(BF16)
        pt = jnp.concatenate(ps + [sink_rows], axis=0)
        od = _dot_tn(pt, jnp.concatenate([v_cat, ind], axis=1))
        o = (od[:, :128] / od[:, 128:]).astype(BF16)
        for p in range(n_pairs):
            o_ref[0, :, (kvh * n_pairs + p) * 128:(kvh * n_pairs + p + 1) * 128] = o[p * blk:(p + 1) * blk]


def _attention(q, k, v, k_ctx, v_ctx, sink):
    B, N, _ = q.shape
    Lc = k_ctx.shape[1]
    nb = N // ATTN_BLOCK
    blk = ATTN_BLOCK

    def kv_spec(off):
        return pl.BlockSpec((1, blk, ATTN_KV_DIM),
                            lambda b, i: (b, jnp.clip(i + off, 0, nb - 1), 0))

    ctx_spec = pl.BlockSpec((1, Lc, ATTN_KV_DIM), lambda b, i: (b, 0, 0))
    return pl.pallas_call(
        functools.partial(_attn_kernel, n_tokens=N),
        grid=(B, nb),
        in_specs=[_row_spec(blk, ATTN_Q_DIM), kv_spec(-1), kv_spec(0), kv_spec(1),
                  kv_spec(-1), kv_spec(0), kv_spec(1), ctx_spec, ctx_spec,
                  _const_spec((1, ATTN_HEADS))],
        out_specs=_row_spec(blk, ATTN_Q_DIM),
        out_shape=jax.ShapeDtypeStruct((B, N, ATTN_Q_DIM), BF16),
        compiler_params=_cparams(("parallel", "parallel")),
        name="window_attention",
    )(q, k, k, k, v, v, v, k_ctx, v_ctx, sink.reshape(1, ATTN_HEADS))


MOE_TILE = 256
MOE_WIN = 64
SLOT_ALIGN = 16
EXPERT_SUB_ROWS = 256


def _select_kernel(aff_ref, slot_ref, s0_ref, *, cap):
    x = aff_ref[0]
    E, L = x.shape
    T = min(MOE_TILE, L)

    def count(mask):
        return jnp.sum(jnp.where(mask, 1.0, 0.0), axis=1, keepdims=True)

    def as_f32(bits):
        return lax.bitcast_convert_type(bits, F32)

    def body(_, c):
        lo, hi = c
        mid = lo + lax.shift_right_logical(hi - lo, 1)
        ok = count(x >= as_f32(mid)) >= cap
        return jnp.where(ok, mid, lo), jnp.where(ok, hi, mid)

    lo0 = jnp.zeros((E, 1), jnp.int32)
    hi0 = jnp.full((E, 1), 0x7F800000, jnp.int32)
    lo, hi = lax.fori_loop(0, 31, body, (lo0, hi0))
    gt = x >= as_f32(hi)
    eq = (x >= as_f32(lo)) & jnp.logical_not(gt)
    need = cap - count(gt)
    ti = lax.broadcasted_iota(jnp.int32, (T, T), 0)
    tj = lax.broadcasted_iota(jnp.int32, (T, T), 1)
    upper = jnp.where(ti < tj, 1.0, 0.0).astype(BF16)

    def excl_cumsum(mask):
        carry = jnp.zeros((E, 1), F32)
        outs, starts = [], []
        for t in range(L // T):
            v = jnp.where(mask[:, t * T:(t + 1) * T], 1.0, 0.0)
            starts.append(carry)
            outs.append(_dot(v.astype(BF16), upper) + carry)
            carry = carry + jnp.sum(v, axis=1, keepdims=True)
        starts.append(carry)
        return jnp.concatenate(outs, axis=1), starts

    eq_rank, _ = excl_cumsum(eq)
    sel = gt | (eq & (eq_rank < need))
    slot, starts = excl_cumsum(sel)
    slot_ref[0] = jnp.where(sel, slot, -1.0).astype(jnp.int32)
    lane = lax.broadcasted_iota(jnp.int32, (E, 128), 1)
    s0 = jnp.zeros((E, 128), F32)
    for t, st in enumerate(starts):
        s0 = jnp.where(lane == t, st, s0)
    s0_ref[0] = s0.astype(jnp.int32)


def _select(aff_t, cap):
    B, E, L = aff_t.shape
    assert L // min(MOE_TILE, L) + 1 <= 128
    return pl.pallas_call(
        functools.partial(_select_kernel, cap=cap),
        grid=(B,),
        in_specs=[pl.BlockSpec((1, E, L), lambda b: (b, 0, 0))],
        out_specs=[pl.BlockSpec((1, E, L), lambda b: (b, 0, 0)), pl.BlockSpec((1, E, 128), lambda b: (b, 0, 0))],
        out_shape=[jax.ShapeDtypeStruct((B, E, L), jnp.int32), jax.ShapeDtypeStruct((B, E, 128), jnp.int32)],
        compiler_params=_cparams(("parallel",)),
        name="moe_select",
    )(aff_t)


def _window_start(s0_ref, b, e, i, nt, cap, win):
    base = (b * N_EXPERTS + e) * (nt + 1) + i
    a = (s0_ref[base] // SLOT_ALIGN) * SLOT_ALIGN
    return a, s0_ref[base + 1], jnp.minimum(a, cap - win)


def _n_windows(a, end, cap, win):
    return jnp.where(a + win >= cap, 1, jnp.maximum((end - a + win - 1) // win, 1))


def _slot_hits(slot_ref, e, a, w, cap, win, jrow):
    lo_s = a + w * win
    r0 = jnp.minimum(lo_s, cap - win)
    se = slot_ref[0, e:e + 1, :]
    return r0, ((se - r0) == jrow) & (se >= lo_s)


def _first_windows(s0_ref, slot_ref, b, i, nt, cap, win):
    T = slot_ref.shape[2]
    jrow = lax.broadcasted_iota(jnp.int32, (win, T), 0)
    first = []
    n_extra = 0
    for e in range(N_EXPERTS):
        a, end, _ = _window_start(s0_ref, b, e, i, nt, cap, win)
        r0, p = _slot_hits(slot_ref, e, a, 0, cap, win, jrow)
        first.append((a, end, r0, p))
        n_extra = n_extra + (_n_windows(a, end, cap, win) - 1)
    onehot_t = jnp.where(jnp.concatenate([f[3] for f in first], axis=0), 1.0, 0.0).astype(BF16)
    return first, onehot_t, n_extra, jrow


def _dispatch_kernel(s0_ref, h_ref, slot_ref, aff_ref, xe_ref, ge_ref, *, cap, win, nt):
    b = pl.program_id(0)
    i = pl.program_id(1)

    @pl.when(i == 0)
    def _():
        xe_ref[...] = jnp.zeros_like(xe_ref)
        ge_ref[...] = jnp.zeros_like(ge_ref)

    h = h_ref[0]

    def place(e, r0, p, rows):
        r0 = pl.multiple_of(r0, SLOT_ALIGN)
        gate = jnp.sum(jnp.where(p, aff_ref[0, e:e + 1, :], 0.0), axis=1, keepdims=True)
        xe_ref[e, 0, pl.ds(r0, win), :] += rows.astype(BF16)
        ge_ref[e, 0, pl.ds(r0, win), :] += gate

    first, onehot_t, n_extra, jrow = _first_windows(s0_ref, slot_ref, b, i, nt, cap, win)
    rows = _dot(onehot_t, h)
    for e in range(N_EXPERTS):
        _, _, r0, p = first[e]
        place(e, r0, p, rows[e * win:(e + 1) * win])

    @pl.when(n_extra > 0)
    def _():
        for e in range(N_EXPERTS):
            a, end, _, _ = first[e]

            def extra(w, carry, e=e, a=a):
                r0w, pw = _slot_hits(slot_ref, e, a, w, cap, win, jrow)
                place(e, r0w, pw, _dot(jnp.where(pw, 1.0, 0.0).astype(BF16), h))
                return carry

            lax.fori_loop(1, _n_windows(a, end, cap, win), extra, 0)


def _dispatch(s0, h2, slot_t, aff_t, cap, win):
    B, L, D = h2.shape
    E = N_EXPERTS
    T = min(MOE_TILE, L)
    nt = L // T
    grid_spec = pltpu.PrefetchScalarGridSpec(
        num_scalar_prefetch=1,
        grid=(B, nt),
        in_specs=[pl.BlockSpec((1, T, D), lambda b, i, s: (b, i, 0)),
                  pl.BlockSpec((1, E, T), lambda b, i, s: (b, 0, i)),
                  pl.BlockSpec((1, E, T), lambda b, i, s: (b, 0, i))],
        out_specs=[pl.BlockSpec((E, 1, cap, D), lambda b, i, s: (0, b, 0, 0)),
                   pl.BlockSpec((E, 1, cap, 1), lambda b, i, s: (0, b, 0, 0))],
    )
    return pl.pallas_call(
        functools.partial(_dispatch_kernel, cap=cap, win=win, nt=nt),
        grid_spec=grid_spec,
        out_shape=[jax.ShapeDtypeStruct((E, B, cap, D), BF16), jax.ShapeDtypeStruct((E, B, cap, 1), F32)],
        compiler_params=_cparams(("parallel", "arbitrary")),
        name="moe_dispatch",
    )(s0, h2, slot_t, aff_t)


def _expert_kernel(x_ref, w1_ref, w3_ref, w2_ref, gate_ref, yhi_ref, ylo_ref, wb1, wb3, wb2, acc_ref, *, tm, nf):
    f = pl.program_id(1)
    m = pl.program_id(2)

    @pl.when(m == 0)
    def _():
        wb1[...] = w1_ref[0, 0].astype(BF16)
        wb3[...] = w3_ref[0, 0].astype(BF16)
        wb2[...] = w2_ref[0, 0].astype(BF16)

    tile = pl.ds(pl.multiple_of(m * tm, tm), tm)

    @pl.when(f == 0)
    def _():
        acc_ref[tile, :] = jnp.zeros((tm, acc_ref.shape[1]), F32)

    sub = min(EXPERT_SUB_ROWS, tm)
    for j in range(tm // sub):
        x = x_ref[0, j * sub:(j + 1) * sub, :]
        a = _dot(x, wb1[...])
        u = _dot(x, wb3[...])
        rows = pl.ds(pl.multiple_of(m * tm + j * sub, sub), sub)
        acc_ref[rows, :] += _dot((_silu(a) * u).astype(BF16), wb2[...])

    @pl.when(f == nf - 1)
    def _():
        out = acc_ref[tile, :] * gate_ref[0]
        hi = out.astype(BF16)
        yhi_ref[0] = hi
        ylo_ref[0] = (out - hi.astype(F32)).astype(BF16)


def _experts(xe, w1, w3, w2, layer, gate, tm, tf=512):
    E, M, D = xe.shape
    F = w1.shape[3]
    nf = F // tf
    out = pl.BlockSpec((1, tm, D), lambda e, f, m: (e, jnp.where(f == nf - 1, m, 0), 0))
    return pl.pallas_call(
        functools.partial(_expert_kernel, tm=tm, nf=nf),
        grid=(E, nf, M // tm),
        in_specs=[pl.BlockSpec((1, tm, D), lambda e, f, m: (e, m, 0)),
                  pl.BlockSpec((1, 1, D, tf), lambda e, f, m: (layer, e, 0, f)),
                  pl.BlockSpec((1, 1, D, tf), lambda e, f, m: (layer, e, 0, f)),
                  pl.BlockSpec((1, 1, tf, D), lambda e, f, m: (layer, e, f, 0)),
                  pl.BlockSpec((1, tm, 1), lambda e, f, m: (e, m, 0))],
        out_specs=[out, out],
        out_shape=[jax.ShapeDtypeStruct((E, M, D), BF16), jax.ShapeDtypeStruct((E, M, D), BF16)],
        scratch_shapes=[pltpu.VMEM((D, tf), BF16), pltpu.VMEM((D, tf), BF16), pltpu.VMEM((tf, D), BF16),
                        pltpu.VMEM((M, D), F32)],
        compiler_params=_cparams(("parallel", "arbitrary", "arbitrary")),
        name="expert_ffn",
    )(xe, w1, w3, w2, gate)


def _combine_kernel(s0_ref, slot_ref, x_ref, g2_ref, fin_ref, yhi_hbm, ylo_hbm, o_ref, stage, extra_stage, acc_ref,
                    sem, *, cap, win, nt, final_norm):
    E = N_EXPERTS
    D = x_ref.shape[2]
    b = pl.program_id(0)
    i = pl.program_id(1)
    step = b * nt + i
    total = pl.num_programs(0) * nt
    buf = step % 2

    def window_copies(src_row, dst, e, s):
        src_row = pl.multiple_of(src_row, SLOT_ALIGN)
        return (pltpu.make_async_copy(yhi_hbm.at[e, pl.ds(src_row, win)], dst.at[:, pl.ds(0, D)], s),
                pltpu.make_async_copy(ylo_hbm.at[e, pl.ds(src_row, win)], dst.at[:, pl.ds(D, D)], s))

    def tile_copies(bb, ii, bf):
        cps = []
        for e in range(E):
            _, _, r0 = _window_start(s0_ref, bb, e, ii, nt, cap, win)
            cps.extend(window_copies(bb * cap + r0, stage.at[bf, pl.ds(e * win, win)], e, sem.at[bf]))
        return cps

    @pl.when(step == 0)
    def _():
        for cp in tile_copies(b, i, 0):
            cp.start()

    nxt = step + 1

    @pl.when(nxt < total)
    def _():
        for cp in tile_copies(nxt // nt, nxt % nt, 1 - buf):
            cp.start()

    for cp in tile_copies(b, i, buf):
        cp.wait()

    def gathered(onehot_t, rows):
        both = _dot_tn(onehot_t, rows)
        return both[:, :D] + both[:, D:]

    first, onehot_t, n_extra, jrow = _first_windows(s0_ref, slot_ref, b, i, nt, cap, win)
    acc_ref[...] = gathered(onehot_t, stage[buf])

    @pl.when(n_extra > 0)
    def _():
        for e in range(E):
            a, end, _, _ = first[e]

            def extra(w, carry, e=e, a=a):
                r0w, pw = _slot_hits(slot_ref, e, a, w, cap, win, jrow)
                cps = window_copies(b * cap + r0w, extra_stage, e, sem.at[2])
                for cp in cps:
                    cp.start()
                for cp in cps:
                    cp.wait()
                acc_ref[...] += gathered(jnp.where(pw, 1.0, 0.0).astype(BF16), extra_stage[...])
                return carry

            lax.fori_loop(1, _n_windows(a, end, cap, win), extra, 0)

    out = x_ref[0] + g2_ref[0] * acc_ref[...]
    if final_norm:
        ms = jnp.mean(out * out, axis=-1, keepdims=True)
        out = out * lax.rsqrt(ms + EPS) * fin_ref[...]
    o_ref[0] = out


def _combine(s0, slot_t, x, g2, yhi, ylo, cap, win, final_g=None):
    B, L, D = x.shape
    E = N_EXPERTS
    T = min(MOE_TILE, L)
    nt = L // T
    grid_spec = pltpu.PrefetchScalarGridSpec(
        num_scalar_prefetch=1,
        grid=(B, nt),
        in_specs=[pl.BlockSpec((1, E, T), lambda b, i, s: (b, 0, i)),
                  pl.BlockSpec((1, T, D), lambda b, i, s: (b, i, 0)),
                  pl.BlockSpec((1, 1, D), lambda b, i, s: (b, 0, 0)),
                  pl.BlockSpec((1, D), lambda b, i, s: (0, 0)),
                  pl.BlockSpec(memory_space=pl.ANY),
                  pl.BlockSpec(memory_space=pl.ANY)],
        out_specs=pl.BlockSpec((1, T, D), lambda b, i, s: (b, i, 0)),
        scratch_shapes=[pltpu.VMEM((2, E * win, 2 * D), BF16), pltpu.VMEM((win, 2 * D), BF16),
                        pltpu.VMEM((T, D), F32), pltpu.SemaphoreType.DMA((3,))],
    )
    return pl.pallas_call(
        functools.partial(_combine_kernel, cap=cap, win=win, nt=nt, final_norm=final_g is not None),
        grid_spec=grid_spec,
        out_shape=jax.ShapeDtypeStruct((B, L, D), F32),
        compiler_params=_cparams(("arbitrary", "arbitrary")),
        name="moe_combine",
    )(s0, slot_t, x, g2, (jnp.ones((D,), F32) if final_g is None else final_g).reshape(1, D), yhi, ylo)


def _moe(x, g2, h2, aff_t, w1, w3, w2, layer, final_g=None):
    B, L, D = h2.shape
    E = N_EXPERTS
    cap = CAPACITY_FACTOR * L // E
    win = min(MOE_WIN, cap)
    nt = L // min(MOE_TILE, L)
    slot_t, s0_pad = _select(aff_t, cap)
    s0 = s0_pad[:, :, :nt + 1].reshape(-1)
    xe, ge = _dispatch(s0, h2, slot_t, aff_t, cap, win)
    yhi, ylo = _experts(xe.reshape(E, B * cap, D), w1, w3, w2, layer, ge.reshape(E, B * cap, 1),
                        tm=min(1024, B * cap))
    return _combine(s0, slot_t, x, g2, yhi, ylo, cap, win, final_g)


def _rope_tables(n):
    half = HEAD_DIM // 2
    quarter = half // 2
    inv = ROPE_BASE ** (-jnp.arange(quarter, dtype=F32) / quarter)
    t = jnp.arange(n)
    row = (t // GRID_W).astype(F32)
    col = (t % GRID_W).astype(F32)
    lane = np.arange(128)
    within = lane % HEAD_DIM
    use_col = jnp.asarray((within >= half)[None, :])
    freq = inv[jnp.asarray(within % quarter)][None, :]
    ang = jnp.where(use_col, col[:, None], row[:, None]) * freq
    sign = jnp.asarray(np.where((lane % half) < quarter, -1.0, 1.0).astype(np.float32))[None, :]
    return jnp.cos(ang), jnp.sin(ang) * sign


def kernel(x, c, ctx, c_ctx, mod_w, mod_b, norm1_g, norm2_g, ab_w_in, ab_w_out, sgu_w, sgu_b, hgrn_lb_logits,
           hgrn_norm_g, attn_w_qkv, attn_w_out, attn_sink, router_w, expert_w1, expert_w3, expert_w2, final_g):
    B, N, D = x.shape
    Lc = ctx.shape[1]
    depth = mod_w.shape[0]

    rows = ((B + 1 + 7) // 8) * 8
    cvec = jnp.zeros((rows, D), F32).at[:B].set(c).at[B].set(c_ctx)
    mod = _mod_vectors(cvec, mod_w, mod_b)

    def mods(l):
        lat = [mod[l, :B, k * D:(k + 1) * D].reshape(B, 1, D) for k in range(6)]
        cx = [jnp.broadcast_to(mod[l, B, k * D:(k + 1) * D].reshape(1, 1, D), (B, 1, D)) for k in range(6)]
        return lat, cx

    lb_all = jnp.cumsum(jax.nn.softmax(hgrn_lb_logits.astype(F32), axis=0), axis=0)
    lmat_f = jnp.asarray(_hgrn_level_matrix(False), BF16)
    lmat_r = jnp.asarray(_hgrn_level_matrix(True), BF16)

    x_lat, x_ctx = x, ctx
    for l in range(depth):
        last = l == depth - 1
        (sh1, sc1, g1, sh2, sc2, g2), (csh1, csc1, cg1, csh2, csc2, cg2) = mods(l)
        wr_t = router_w[l].T
        if l % 2 == 0:
            e = l // 2
            w_in = ab_w_in[e].astype(BF16)
            w_out = ab_w_out[e].astype(BF16)
            sw = sgu_w[e].astype(BF16)
            sb_t = sgu_b[e].T
            hg = hgrn_norm_g[e].reshape(1, HGRN_DIM)
            s0 = jnp.zeros((2, B, HGRN_HEADS, HGRN_KDIM, HGRN_KDIM), F32)

            def even(xs, shift, scale, gate1, shift2, scale2, s_in):
                z = _proj(xs, norm1_g[l], shift, scale, w_in)
                ya = _sgu(z, sw, sb_t)
                o_f, o_b, s_fin = _hgrn(z, lb_all[e], s_in, lmat_f, lmat_r)
                xn, h2, aff = _even_post(ya, o_f, o_b, z, hg, w_out, xs, gate1, norm2_g[l], shift2, scale2, wr_t)
                return xn, h2, aff, s_fin

            x_ctx, h2c, affc, s_ctx = even(x_ctx, csh1, csc1, cg1, csh2, csc2, s0)
            x_lat, h2l, affl, _ = even(x_lat, sh1, sc1, g1, sh2, sc2, s_ctx)
        else:
            if not last:
                raise NotImplementedError("context update after an attention layer")
            o_idx = l // 2
            w_qkv = attn_w_qkv[o_idx].astype(BF16)
            w_out = attn_w_out[o_idx].astype(BF16)
            cos, sin = _rope_tables(N)
            zc = _proj(x_ctx, norm1_g[l], csh1, csc1, w_qkv[:, ATTN_Q_DIM:])
            k_ctx = zc[..., :ATTN_KV_DIM].astype(BF16)
            v_ctx = zc[..., ATTN_KV_DIM:].astype(BF16)
            q, k, v = _qkv_lat(x_lat, norm1_g[l], sh1, sc1, w_qkv, cos, sin)
            o = _attention(q, k, v, k_ctx, v_ctx, attn_sink[o_idx])
            x_lat, h2l, affl = _odd_post(o, w_out, x_lat, g1, norm2_g[l], sh2, sc2, wr_t)
        x_lat = _moe(x_lat, g2, h2l, affl, expert_w1, expert_w3, expert_w2, l, final_g if last else None)
        if not last:
            x_ctx = _moe(x_ctx, cg2, h2c, affc, expert_w1, expert_w3, expert_w2, l)
    return x_lat
```

```python
import functools

import numpy as np
import jax
import jax.numpy as jnp
from jax import lax
from jax.experimental import pallas as pl
from jax.experimental.pallas import tpu as pltpu

F32 = jnp.float32
BF16 = jnp.bfloat16

D_MODEL = 1024
GRID_W = 64
EPS = 1e-6
SGU_GROUPS = 4
SGU_GROUP_DIM = 128
SGU_DIM = SGU_GROUPS * SGU_GROUP_DIM
SGU_CHUNK = 128
HGRN_HEADS = 4
HGRN_KDIM = 128
HGRN_DIM = HGRN_HEADS * HGRN_KDIM
HGRN_CHUNK = 64
HGRN_LEVELS = 6
HGRN_COARSE = 3
HGRN_TBLK = 256
AB_IN = 2 * SGU_DIM + 5 * HGRN_DIM
ATTN_HEADS = 16
ATTN_KV_HEADS = 4
ATTN_GROUP = ATTN_HEADS // ATTN_KV_HEADS
HEAD_DIM = 64
ATTN_Q_DIM = ATTN_HEADS * HEAD_DIM
ATTN_KV_DIM = ATTN_KV_HEADS * HEAD_DIM
WINDOW = 128
ATTN_BLOCK = 128
ROPE_BASE = 10000.0
N_EXPERTS = 16
CAPACITY_FACTOR = 2
EXPERT_FF = 2048
NEG_BIG = -1e30

VMEM_LIMIT_BYTES = 56 * 1024 * 1024


ROW_TILE = 512


def _row_tile(n_rows):
    return min(ROW_TILE, n_rows)


def _cparams(sem):
    return pltpu.CompilerParams(dimension_semantics=sem, vmem_limit_bytes=VMEM_LIMIT_BYTES)


def _dot(a, b):
    return jnp.dot(a, b, preferred_element_type=F32)


def _dot_nt(a, b):
    return lax.dot_general(a, b, (((1,), (1,)), ((), ())), preferred_element_type=F32)


def _dot_tn(a, b):
    return lax.dot_general(a, b, (((0,), (0,)), ((), ())), preferred_element_type=F32)


def _split2(a):
    hi = a.astype(BF16)
    lo = (a - hi.astype(F32)).astype(BF16)
    return hi, lo


def _dot_f32x3(a, b):
    a_hi, a_lo = _split2(a)
    b_hi, b_lo = _split2(b)
    return _dot(a_hi, b_hi) + (_dot(a_hi, b_lo) + _dot(a_lo, b_hi))


def _silu(x):
    return x * jax.nn.sigmoid(x)


def _gelu(x):
    return 0.5 * x * (1.0 + lax.erf(x * np.float32(1.0 / np.sqrt(2.0))))


def _rms_modulate(x, g, shift, scale):
    ms = jnp.mean(x * x, axis=-1, keepdims=True)
    return (x * lax.rsqrt(ms + EPS) * g) * (1.0 + scale) + shift


def _mod_kernel(c_ref, w_ref, b_ref, o_ref):
    s = _silu(c_ref[...])
    o_ref[0] = _dot_f32x3(s, w_ref[0]) + b_ref[0]


def _mod_vectors(cvec, mod_w, mod_b):
    depth, d, n6 = mod_w.shape
    rows = cvec.shape[0]
    tn = 1536
    return pl.pallas_call(
        _mod_kernel,
        grid=(depth, n6 // tn),
        in_specs=[
            pl.BlockSpec((rows, d), lambda l, j: (0, 0)),
            pl.BlockSpec((1, d, tn), lambda l, j: (l, 0, j)),
            pl.BlockSpec((1, 1, tn), lambda l, j: (l, 0, j)),
        ],
        out_specs=pl.BlockSpec((1, rows, tn), lambda l, j: (l, 0, j)),
        out_shape=jax.ShapeDtypeStruct((depth, rows, n6), F32),
        compiler_params=_cparams(("arbitrary", "arbitrary")),
        name="mod_vectors",
    )(cvec, mod_w, mod_b.reshape(depth, 1, n6))


def _proj_kernel(x_ref, g_ref, sh_ref, sc_ref, w_ref, z_ref):
    h = _rms_modulate(x_ref[0], g_ref[...], sh_ref[0], sc_ref[0])
    z_ref[0] = _dot(h.astype(BF16), w_ref[...])


def _row_spec(tm, d):
    return pl.BlockSpec((1, tm, d), lambda b, i: (b, i, 0))


def _vec_spec(d):
    return pl.BlockSpec((1, 1, d), lambda b, i: (b, 0, 0))


def _const_spec(shape):
    nd = len(shape)
    return pl.BlockSpec(shape, lambda b, i: (0,) * nd)


def _proj(x, g, shift, scale, w):
    B, L, D = x.shape
    tm = _row_tile(L)
    n = w.shape[1]
    return pl.pallas_call(
        _proj_kernel,
        grid=(B, L // tm),
        in_specs=[_row_spec(tm, D), _const_spec((1, D)), _vec_spec(D), _vec_spec(D), _const_spec((D, n))],
        out_specs=_row_spec(tm, n),
        out_shape=jax.ShapeDtypeStruct((B, L, n), F32),
        compiler_params=_cparams(("parallel", "parallel")),
        name="norm_mod_proj",
    )(x, g.reshape(1, D), shift, scale, w)


def _sgu_kernel(zu_ref, zv_ref, w_ref, b_ref, o_ref, *, tc):
    for ch in range(tc // SGU_CHUNK):
        rows = slice(ch * SGU_CHUNK, (ch + 1) * SGU_CHUNK)
        u = _gelu(zu_ref[0, rows, :])
        v = _gelu(zv_ref[0, rows, :])
        for g in range(SGU_GROUPS):
            cols = slice(g * SGU_GROUP_DIM, (g + 1) * SGU_GROUP_DIM)
            vg = v[:, cols]
            mu = jnp.mean(vg, axis=-1, keepdims=True)
            vc = vg - mu
            var = jnp.mean(vc * vc, axis=-1, keepdims=True)
            vn = (vc * lax.rsqrt(var + EPS)).astype(BF16)
            s = _dot(w_ref[g], vn) + b_ref[:, g:g + 1]
            o_ref[0, rows, cols] = (u[:, cols] * s).astype(BF16)


def _sgu(z, sgu_w_bf16, sgu_b_t):
    B, L, _ = z.shape
    tc = _row_tile(L)
    return pl.pallas_call(
        functools.partial(_sgu_kernel, tc=tc),
        grid=(B, L // tc),
        in_specs=[
            pl.BlockSpec((1, tc, SGU_DIM), lambda b, i: (b, i, 0)),
            pl.BlockSpec((1, tc, SGU_DIM), lambda b, i: (b, i, 1)),
            _const_spec((SGU_GROUPS, SGU_CHUNK, SGU_CHUNK)),
            _const_spec((SGU_CHUNK, SGU_GROUPS)),
        ],
        out_specs=pl.BlockSpec((1, tc, SGU_DIM), lambda b, i: (b, i, 0)),
        out_shape=jax.ShapeDtypeStruct((B, L, SGU_DIM), BF16),
        compiler_params=_cparams(("parallel", "parallel")),
        name="sgu",
    )(z, z, sgu_w_bf16, sgu_b_t)


def _hgrn_level_matrix(reverse):
    C = HGRN_CHUNK
    t = np.arange(C)[:, None]
    s = np.arange(C)[None, :]
    blocks = []
    blocks.append(((s <= t) if not reverse else (s >= t)).astype(np.float32))
    for l in range(HGRN_COARSE, HGRN_LEVELS):
        m = C >> (l + 1)
        base = (t // (2 * m)) * (2 * m)
        if not reverse:
            r = base + m - 1
            qrow = (t % (2 * m)) >= m
            mq = (s > r) & (s <= t)
            mk = (s > t) & (s <= r)
        else:
            r = base + m
            qrow = (t % (2 * m)) < m
            mq = (s >= t) & (s < r)
            mk = (s >= r) & (s < t)
        blocks.append(np.where(qrow, mq, mk).astype(np.float32))
    return np.concatenate(blocks, axis=0)


def _hgrn_masks(reverse):
    C = HGRN_CHUNK
    t = lax.broadcasted_iota(jnp.int32, (C, C), 0)
    s = lax.broadcasted_iota(jnp.int32, (C, C), 1)
    r = lax.broadcasted_iota(jnp.int32, (C, 1), 0)
    masks, qrows = [], []
    for l in range(HGRN_LEVELS):
        m = C >> (l + 1)
        same = (t // (2 * m)) == (s // (2 * m))
        if not reverse:
            mk = same & ((t % (2 * m)) >= m) & ((s % (2 * m)) < m)
            qr = (r % (2 * m)) >= m
        else:
            mk = same & ((t % (2 * m)) < m) & ((s % (2 * m)) >= m)
            qr = (r % (2 * m)) < m
        masks.append(mk)
        qrows.append(qr)
    return masks, qrows


def _hgrn_chunk(zq, zf, zi, lb, st, lmat, masks, qrows, reverse):
    C = HGRN_CHUNK
    f = lb + (1.0 - lb) * jax.nn.sigmoid(zf)
    g = jnp.log(f)
    kk = 1.0 - f
    q = _silu(zq)
    vb = zi.astype(BF16)
    g_hi, g_lo = _split2(g)
    d2 = _dot(lmat, jnp.concatenate([g_hi, g_lo], axis=1))
    d = d2[:, :HGRN_KDIM] + d2[:, HGRN_KDIM:]
    b = d[0:C]
    b_end = b[0:1] if reverse else b[C - 1:C]
    dec = jnp.exp(b_end)
    qb = (q * jnp.exp(b)).astype(BF16)
    kb = (kk * jnp.exp(b_end - b)).astype(BF16)
    o = _dot_nt(qb, st.astype(BF16))
    o = o + jnp.sum(q * kk, axis=1, keepdims=True) * zi
    a = None
    for l in range(HGRN_LEVELS):
        if l < HGRN_COARSE:
            m = C >> (l + 1)
            r = m if reverse else m - 1
            ref = jnp.broadcast_to(b.reshape(C // (2 * m), 2 * m, HGRN_KDIM)[:, r:r + 1, :],
                                   (C // (2 * m), 2 * m, HGRN_KDIM)).reshape(C, HGRN_KDIM)
            dl = jnp.where(qrows[l], b - ref, ref - b)
        else:
            dl = d[(1 + l - HGRN_COARSE) * C:(2 + l - HGRN_COARSE) * C]
        mix = (jnp.where(qrows[l], q, kk) * jnp.exp(dl)).astype(BF16)
        p = jnp.where(masks[l], _dot_nt(mix, mix), 0.0)
        a = p if a is None else a + p
    o = o + _dot(a.astype(BF16), vb)
    st_new = st * dec + _dot_tn(vb, kb)
    return o, st_new


def _hgrn_kernel(zqf_ref, zff_ref, zif_ref, zqb_ref, zfb_ref, zib_ref, lb_ref, s0_ref, lf_ref, lr_ref,
                 of_ref, ob_ref, sfin_ref, st_ref, *, tb):
    i = pl.program_id(1)
    nb = pl.num_programs(1)

    @pl.when(i == 0)
    def _():
        st_ref[...] = s0_ref[:, 0]

    C = HGRN_CHUNK
    nch = tb // C
    for d in range(2):
        reverse = d == 1
        masks, qrows = _hgrn_masks(reverse)
        lmat = (lr_ref if reverse else lf_ref)[...]
        zq_ref, zf_ref, zi_ref, o_ref = ((zqb_ref, zfb_ref, zib_ref, ob_ref) if reverse
                                         else (zqf_ref, zff_ref, zif_ref, of_ref))
        for h in range(HGRN_HEADS):
            cols = slice(h * HGRN_KDIM, (h + 1) * HGRN_KDIM)
            lb = lb_ref[d:d + 1, cols]
            st = st_ref[d, h]
            order = range(nch - 1, -1, -1) if reverse else range(nch)
            for c in order:
                rows = slice(c * C, (c + 1) * C)
                o, st = _hgrn_chunk(zq_ref[0, rows, cols], zf_ref[0, rows, cols], zi_ref[0, rows, cols],
                                    lb, st, lmat, masks, qrows, reverse)
                o_ref[0, rows, cols] = o
            st_ref[d, h] = st

    @pl.when(i == nb - 1)
    def _():
        sfin_ref[:, 0] = st_ref[...]


def _hgrn(z, lb, s0, lmat_f, lmat_r):
    B, L, _ = z.shape
    tb = min(HGRN_TBLK, L)
    nb = L // tb
    nl = lmat_f.shape[0]

    def zspec(col, rev):
        if rev:
            return pl.BlockSpec((1, tb, HGRN_DIM), lambda b, i: (b, nb - 1 - i, col))
        return pl.BlockSpec((1, tb, HGRN_DIM), lambda b, i: (b, i, col))

    st_spec = pl.BlockSpec((2, 1, HGRN_HEADS, HGRN_KDIM, HGRN_KDIM), lambda b, i: (0, b, 0, 0, 0))
    return pl.pallas_call(
        functools.partial(_hgrn_kernel, tb=tb),
        grid=(B, nb),
        in_specs=[zspec(2, False), zspec(3, False), zspec(5, False),
                  zspec(2, True), zspec(4, True), zspec(5, True),
                  _const_spec((2, HGRN_DIM)), st_spec,
                  _const_spec((nl, HGRN_CHUNK)), _const_spec((nl, HGRN_CHUNK))],
        out_specs=[pl.BlockSpec((1, tb, HGRN_DIM), lambda b, i: (b, i, 0)),
                   pl.BlockSpec((1, tb, HGRN_DIM), lambda b, i: (b, nb - 1 - i, 0)),
                   st_spec],
        out_shape=[jax.ShapeDtypeStruct((B, L, HGRN_DIM), F32),
                   jax.ShapeDtypeStruct((B, L, HGRN_DIM), F32),
                   jax.ShapeDtypeStruct(s0.shape, F32)],
        scratch_shapes=[pltpu.VMEM((2, HGRN_HEADS, HGRN_KDIM, HGRN_KDIM), F32)],
        compiler_params=_cparams(("parallel", "arbitrary")),
        name="hgrn2_scan",
    )(z, z, z, z, z, z, lb, s0, lmat_f, lmat_r)


def _residual_router_tail(x, y, g1, n2g, sh2, sc2, wr, xo_ref, h2_ref, aff_ref):
    xn = x + g1 * y
    xo_ref[0] = xn
    h2 = _rms_modulate(xn, n2g, sh2, sc2)
    h2_ref[0] = h2.astype(BF16)
    w_hi, w_lo = _split2(wr)
    h_hi, h_lo = _split2(h2)
    logits = _dot_nt(w_hi, h_hi) + (_dot_nt(w_hi, h_lo) + _dot_nt(w_lo, h_hi))
    mx = jnp.max(logits, axis=0, keepdims=True)
    p = jnp.exp(logits - mx)
    aff_ref[0] = p / jnp.sum(p, axis=0, keepdims=True)


def _even_post_kernel(ya_ref, of_ref, ob_ref, zg_ref, hg_ref, w_ref, x_ref, g1_ref, n2g_ref, sh2_ref,
                      sc2_ref, wr_ref, xo_ref, h2_ref, aff_ref):
    o = of_ref[0] + ob_ref[0]
    zg = zg_ref[0]
    parts = []
    for h in range(HGRN_HEADS):
        cols = slice(h * HGRN_KDIM, (h + 1) * HGRN_KDIM)
        oh = o[:, cols]
        ms = jnp.mean(oh * oh, axis=-1, keepdims=True)
        parts.append(oh * lax.rsqrt(ms + EPS) * hg_ref[:, cols])
    yb = (jnp.concatenate(parts, axis=1) * _silu(zg)).astype(BF16)
    y = _dot(ya_ref[0], w_ref[:SGU_DIM, :]) + _dot(yb, w_ref[SGU_DIM:, :])
    _residual_router_tail(x_ref[0], y, g1_ref[0], n2g_ref[...], sh2_ref[0], sc2_ref[0], wr_ref[...],
                          xo_ref, h2_ref, aff_ref)


def _odd_post_kernel(o_ref, w_ref, x_ref, g1_ref, n2g_ref, sh2_ref, sc2_ref, wr_ref, xo_ref, h2_ref,
                     aff_ref):
    y = _dot(o_ref[0], w_ref[...])
    _residual_router_tail(x_ref[0], y, g1_ref[0], n2g_ref[...], sh2_ref[0], sc2_ref[0], wr_ref[...],
                          xo_ref, h2_ref, aff_ref)


def _post_out_specs(B, L, tm):
    D = D_MODEL
    specs = [_row_spec(tm, D), _row_spec(tm, D), pl.BlockSpec((1, N_EXPERTS, tm), lambda b, i: (b, 0, i))]
    shapes = [jax.ShapeDtypeStruct((B, L, D), F32), jax.ShapeDtypeStruct((B, L, D), BF16),
              jax.ShapeDtypeStruct((B, N_EXPERTS, L), F32)]
    return specs, shapes


def _even_post(ya, o_f, o_b, z, hg, w_out, x, g1, n2g, sh2, sc2, wr):
    B, L, D = x.shape
    tm = _row_tile(L)
    specs, shapes = _post_out_specs(B, L, tm)
    return pl.pallas_call(
        _even_post_kernel,
        grid=(B, L // tm),
        in_specs=[_row_spec(tm, SGU_DIM), _row_spec(tm, HGRN_DIM), _row_spec(tm, HGRN_DIM),
                  pl.BlockSpec((1, tm, HGRN_DIM), lambda b, i: (b, i, 6)),
                  _const_spec((1, HGRN_DIM)), _const_spec((D, D)), _row_spec(tm, D), _vec_spec(D),
                  _const_spec((1, D)), _vec_spec(D), _vec_spec(D), _const_spec((N_EXPERTS, D))],
        out_specs=specs,
        out_shape=shapes,
        compiler_params=_cparams(("parallel", "parallel")),
        name="even_post_router",
    )(ya, o_f, o_b, z, hg, w_out, x, g1, n2g.reshape(1, D), sh2, sc2, wr)


def _odd_post(o, w_out, x, g1, n2g, sh2, sc2, wr):
    B, L, D = x.shape
    tm = _row_tile(L)
    specs, shapes = _post_out_specs(B, L, tm)
    return pl.pallas_call(
        _odd_post_kernel,
        grid=(B, L // tm),
        in_specs=[_row_spec(tm, ATTN_Q_DIM), _const_spec((ATTN_Q_DIM, D)), _row_spec(tm, D), _vec_spec(D),
                  _const_spec((1, D)), _vec_spec(D), _vec_spec(D), _const_spec((N_EXPERTS, D))],
        out_specs=specs,
        out_shape=shapes,
        compiler_params=_cparams(("parallel", "parallel")),
        name="odd_post_router",
    )(o, w_out, x, g1, n2g.reshape(1, D), sh2, sc2, wr)


def _rope(x, cos, sin_signed):
    lane = lax.broadcasted_iota(jnp.int32, (x.shape[0], 128), 1)
    first = (lane % 32) < 16
    outs = []
    for j in range(x.shape[1] // 128):
        xb = x[:, j * 128:(j + 1) * 128]
        sw = jnp.where(first, pltpu.roll(xb, 112, 1), pltpu.roll(xb, 16, 1))
        outs.append(xb * cos + sw * sin_signed)
    return jnp.concatenate(outs, axis=1)


def _qkv_lat_kernel(x_ref, g_ref, sh_ref, sc_ref, w_ref, cos_ref, sin_ref, q_ref, k_ref, v_ref):
    h = _rms_modulate(x_ref[0], g_ref[...], sh_ref[0], sc_ref[0])
    z = _dot(h.astype(BF16), w_ref[...])
    cos = cos_ref[...]
    sin = sin_ref[...]
    scale = np.float32(HEAD_DIM ** -0.5)
    q_ref[0] = (_rope(z[:, :ATTN_Q_DIM], cos, sin) * scale).astype(BF16)
    k_ref[0] = _rope(z[:, ATTN_Q_DIM:ATTN_Q_DIM + ATTN_KV_DIM], cos, sin).astype(BF16)
    v_ref[0] = z[:, ATTN_Q_DIM + ATTN_KV_DIM:].astype(BF16)


def _qkv_lat(x, g, shift, scale, w, cos, sin):
    B, L, D = x.shape
    tm = _row_tile(L)
    n = w.shape[1]
    tab = pl.BlockSpec((tm, 128), lambda b, i: (i, 0))
    return pl.pallas_call(
        _qkv_lat_kernel,
        grid=(B, L // tm),
        in_specs=[_row_spec(tm, D), _const_spec((1, D)), _vec_spec(D), _vec_spec(D), _const_spec((D, n)), tab, tab],
        out_specs=[_row_spec(tm, ATTN_Q_DIM), _row_spec(tm, ATTN_KV_DIM), _row_spec(tm, ATTN_KV_DIM)],
        out_shape=[jax.ShapeDtypeStruct((B, L, ATTN_Q_DIM), BF16),
                   jax.ShapeDtypeStruct((B, L, ATTN_KV_DIM), BF16),
                   jax.ShapeDtypeStruct((B, L, ATTN_KV_DIM), BF16)],
        compiler_params=_cparams(("parallel", "parallel")),
        name="norm_qkv_rope",
    )(x, g.reshape(1, D), shift, scale, w, cos, sin)


def _attn_kernel(q_ref, kp_ref, kc_ref, kn_ref, vp_ref, vc_ref, vn_ref, kx_ref, vx_ref, sink_ref, o_ref,
                 *, n_tokens):
    bi = pl.program_id(1)
    blk = ATTN_BLOCK
    n_loc = 3 * blk
    nk = n_loc + kx_ref.shape[1]
    n_pairs = ATTN_GROUP // 2
    nq = n_pairs * blk
    pad = 16
    s_idx = lax.broadcasted_iota(jnp.int32, (n_loc, nq), 0)
    r_idx = lax.broadcasted_iota(jnp.int32, (n_loc, nq), 1) % blk
    j = bi * blk - blk + s_idx
    valid = (jnp.abs(s_idx - r_idx - blk) <= WINDOW) & (j >= 0) & (j < n_tokens)
    lane = lax.broadcasted_iota(jnp.int32, (nk, 128), 1)
    qcol = lax.broadcasted_iota(jnp.int32, (1, nq), 1)
    prow = lax.broadcasted_iota(jnp.int32, (pad, nq), 0)
    er = lax.broadcasted_iota(jnp.int32, (2 * nk + pad, 128), 0)
    el = lax.broadcasted_iota(jnp.int32, (2 * nk + pad, 128), 1)
    lo_rows = (er < nk) | (er == 2 * nk)
    hi_rows = ((er >= nk) & (er < 2 * nk)) | (er == 2 * nk + 1)
    ind = jnp.where((lo_rows & (el < HEAD_DIM)) | (hi_rows & (el >= HEAD_DIM)), 1.0, 0.0).astype(BF16)
    zpad = jnp.zeros((pad, 128), BF16)
    scores, values = [], []
    for kvh in range(ATTN_KV_HEADS):
        cols = slice((kvh // 2) * 128, (kvh // 2 + 1) * 128)
        own = (lane < HEAD_DIM) if kvh % 2 == 0 else (lane >= HEAD_DIM)
        k_all = jnp.concatenate([kp_ref[0, :, cols], kc_ref[0, :, cols], kn_ref[0, :, cols], kx_ref[0, :, cols]],
                                axis=0)
        v_all = jnp.concatenate([vp_ref[0, :, cols], vc_ref[0, :, cols], vn_ref[0, :, cols], vx_ref[0, :, cols]],
                                axis=0)
        k_own = jnp.where(own, k_all, jnp.zeros_like(k_all))
        v_own = jnp.where(own, v_all, jnp.zeros_like(v_all))
        k_oth = pltpu.roll(k_own, HEAD_DIM, 1)
        v_oth = pltpu.roll(v_own, HEAD_DIM, 1)
        if kvh % 2 == 0:
            k_cat = jnp.concatenate([k_own, k_oth], axis=0)
            v_cat = jnp.concatenate([v_own, v_oth, zpad], axis=0)
        else:
            k_cat = jnp.concatenate([k_oth, k_own], axis=0)
            v_cat = jnp.concatenate([v_oth, v_own, zpad], axis=0)
        q2 = jnp.concatenate([q_ref[0, :, (kvh * n_pairs + p) * 128:(kvh * n_pairs + p + 1) * 128]
                              for p in range(n_pairs)], axis=0)
        scores.append(_dot_nt(k_cat, q2))
        values.append(jnp.concatenate([v_cat, ind], axis=1))
    probs = []
    for kvh in range(ATTN_KV_HEADS):
        st = scores[kvh]
        ps, es = [], []
        for sub in range(2):
            sh = st[sub * nk:(sub + 1) * nk]
            sh = jnp.concatenate([jnp.where(valid, sh[:n_loc], NEG_BIG), sh[n_loc:]], axis=0)
            hd0 = kvh * ATTN_GROUP + sub
            sink = sink_ref[:, hd0:hd0 + 1]
            for p in range(1, n_pairs):
                sink = jnp.where(qcol < p * blk, sink, sink_ref[:, hd0 + 2 * p:hd0 + 2 * p + 1])
            m = jnp.maximum(jnp.max(sh, axis=0, keepdims=True), sink)
            ps.append(jnp.exp(sh - m).astype(BF16))
            es.append(jnp.exp(sink - m))
        sink_rows = jnp.where(prow == 0, es[0], jnp.where(prow == 1, es[1], 0.0)).astype(BF16)
        probs.append(jnp.concatenate(ps + [sink_rows], axis=0))
    for kvh in range(ATTN_KV_HEADS):
        od = _dot_tn(probs[kvh], values[kvh])
        o = (od[:, :128] / od[:, 128:]).astype(BF16)
        for p in range(n_pairs):
            o_ref[0, :, (kvh * n_pairs + p) * 128:(kvh * n_pairs + p + 1) * 128] = o[p * blk:(p + 1) * blk]


def _attention(q, k, v, k_ctx, v_ctx, sink):
    B, N, _ = q.shape
    Lc = k_ctx.shape[1]
    nb = N // ATTN_BLOCK
    blk = ATTN_BLOCK

    def kv_spec(off):
        return pl.BlockSpec((1, blk, ATTN_KV_DIM),
                            lambda b, i: (b, jnp.clip(i + off, 0, nb - 1), 0))

    ctx_spec = pl.BlockSpec((1, Lc, ATTN_KV_DIM), lambda b, i: (b, 0, 0))
    return pl.pallas_call(
        functools.partial(_attn_kernel, n_tokens=N),
        grid=(B, nb),
        in_specs=[_row_spec(blk, ATTN_Q_DIM), kv_spec(-1), kv_spec(0), kv_spec(1),
                  kv_spec(-1), kv_spec(0), kv_spec(1), ctx_spec, ctx_spec,
                  _const_spec((1, ATTN_HEADS))],
        out_specs=_row_spec(blk, ATTN_Q_DIM),
        out_shape=jax.ShapeDtypeStruct((B, N, ATTN_Q_DIM), BF16),
        compiler_params=_cparams(("parallel", "parallel")),
        name="window_attention",
    )(q, k, k, k, v, v, v, k_ctx, v_ctx, sink.reshape(1, ATTN_HEADS))


MOE_TILE = 256
MOE_WIN = 64
SLOT_ALIGN = 16
EXPERT_SUB_ROWS = 1024


def _select_kernel(aff_ref, slot_ref, s0_ref, *, cap):
    x = aff_ref[0]
    E, L = x.shape
    T = min(MOE_TILE, L)

    def count(mask):
        return jnp.sum(jnp.where(mask, 1.0, 0.0), axis=1, keepdims=True)

    def as_f32(bits):
        return lax.bitcast_convert_type(bits, F32)

    def body(_, c):
        lo, hi = c
        mid = lo + lax.shift_right_logical(hi - lo, 1)
        ok = count(x >= as_f32(mid)) >= cap
        return jnp.where(ok, mid, lo), jnp.where(ok, hi, mid)

    lo0 = jnp.zeros((E, 1), jnp.int32)
    hi0 = jnp.full((E, 1), 0x7F800000, jnp.int32)
    lo, hi = lax.fori_loop(0, 31, body, (lo0, hi0))
    gt = x >= as_f32(hi)
    eq = (x >= as_f32(lo)) & jnp.logical_not(gt)
    need = cap - count(gt)
    ti = lax.broadcasted_iota(jnp.int32, (T, T), 0)
    tj = lax.broadcasted_iota(jnp.int32, (T, T), 1)
    upper = jnp.where(ti < tj, 1.0, 0.0).astype(BF16)

    def excl_cumsum(mask):
        carry = jnp.zeros((E, 1), F32)
        outs, starts = [], []
        for t in range(L // T):
            v = jnp.where(mask[:, t * T:(t + 1) * T], 1.0, 0.0)
            starts.append(carry)
            outs.append(_dot(v.astype(BF16), upper) + carry)
            carry = carry + jnp.sum(v, axis=1, keepdims=True)
        starts.append(carry)
        return jnp.concatenate(outs, axis=1), starts

    eq_rank, _ = excl_cumsum(eq)
    sel = gt | (eq & (eq_rank < need))
    slot, starts = excl_cumsum(sel)
    slot_ref[0] = jnp.where(sel, slot, -1.0).astype(jnp.int32)
    lane = lax.broadcasted_iota(jnp.int32, (E, 128), 1)
    s0 = jnp.zeros((E, 128), F32)
    for t, st in enumerate(starts):
        s0 = jnp.where(lane == t, st, s0)
    s0_ref[0] = s0.astype(jnp.int32)


def _select(aff_t, cap):
    B, E, L = aff_t.shape
    assert L // min(MOE_TILE, L) + 1 <= 128
    return pl.pallas_call(
        functools.partial(_select_kernel, cap=cap),
        grid=(B,),
        in_specs=[pl.BlockSpec((1, E, L), lambda b: (b, 0, 0))],
        out_specs=[pl.BlockSpec((1, E, L), lambda b: (b, 0, 0)), pl.BlockSpec((1, E, 128), lambda b: (b, 0, 0))],
        out_shape=[jax.ShapeDtypeStruct((B, E, L), jnp.int32), jax.ShapeDtypeStruct((B, E, 128), jnp.int32)],
        compiler_params=_cparams(("parallel",)),
        name="moe_select",
    )(aff_t)


def _window_start(s0_ref, b, e, i, nt, cap, win):
    base = (b * N_EXPERTS + e) * (nt + 1) + i
    a = (s0_ref[base] // SLOT_ALIGN) * SLOT_ALIGN
    return a, s0_ref[base + 1], jnp.minimum(a, cap - win)


def _n_windows(a, end, cap, win):
    return jnp.where(a + win >= cap, 1, jnp.maximum((end - a + win - 1) // win, 1))


def _slot_hits(slot_ref, e, a, w, cap, win, jrow):
    lo_s = a + w * win
    r0 = jnp.minimum(lo_s, cap - win)
    se = slot_ref[0, e:e + 1, :]
    return r0, ((se - r0) == jrow) & (se >= lo_s)


def _first_windows(s0_ref, slot_ref, b, i, nt, cap, win):
    T = slot_ref.shape[2]
    jrow = lax.broadcasted_iota(jnp.int32, (win, T), 0)
    first = []
    n_extra = 0
    for e in range(N_EXPERTS):
        a, end, _ = _window_start(s0_ref, b, e, i, nt, cap, win)
        r0, p = _slot_hits(slot_ref, e, a, 0, cap, win, jrow)
        first.append((a, end, r0, p))
        n_extra = n_extra + (_n_windows(a, end, cap, win) - 1)
    onehot_t = jnp.where(jnp.concatenate([f[3] for f in first], axis=0), 1.0, 0.0).astype(BF16)
    return first, onehot_t, n_extra, jrow


def _dispatch_kernel(s0_ref, h_ref, slot_ref, aff_ref, xe_ref, ge_ref, *, cap, win, nt):
    b = pl.program_id(0)
    i = pl.program_id(1)

    @pl.when(i == 0)
    def _():
        xe_ref[...] = jnp.zeros_like(xe_ref)
        ge_ref[...] = jnp.zeros_like(ge_ref)

    h = h_ref[0]

    def place(e, r0, p, rows):
        r0 = pl.multiple_of(r0, SLOT_ALIGN)
        gate = jnp.sum(jnp.where(p, aff_ref[0, e:e + 1, :], 0.0), axis=1, keepdims=True)
        xe_ref[e, 0, pl.ds(r0, win), :] += rows.astype(BF16)
        ge_ref[e, 0, pl.ds(r0, win), :] += gate

    first, onehot_t, n_extra, jrow = _first_windows(s0_ref, slot_ref, b, i, nt, cap, win)
    rows = _dot(onehot_t, h)
    for e in range(N_EXPERTS):
        _, _, r0, p = first[e]
        place(e, r0, p, rows[e * win:(e + 1) * win])

    @pl.when(n_extra > 0)
    def _():
        for e in range(N_EXPERTS):
            a, end, _, _ = first[e]

            def extra(w, carry, e=e, a=a):
                r0w, pw = _slot_hits(slot_ref, e, a, w, cap, win, jrow)
                place(e, r0w, pw, _dot(jnp.where(pw, 1.0, 0.0).astype(BF16), h))
                return carry

            lax.fori_loop(1, _n_windows(a, end, cap, win), extra, 0)


def _dispatch(s0, h2, slot_t, aff_t, cap, win):
    B, L, D = h2.shape
    E = N_EXPERTS
    T = min(MOE_TILE, L)
    nt = L // T
    grid_spec = pltpu.PrefetchScalarGridSpec(
        num_scalar_prefetch=1,
        grid=(B, nt),
        in_specs=[pl.BlockSpec((1, T, D), lambda b, i, s: (b, i, 0)),
                  pl.BlockSpec((1, E, T), lambda b, i, s: (b, 0, i)),
                  pl.BlockSpec((1, E, T), lambda b, i, s: (b, 0, i))],
        out_specs=[pl.BlockSpec((E, 1, cap, D), lambda b, i, s: (0, b, 0, 0)),
                   pl.BlockSpec((E, 1, cap, 1), lambda b, i, s: (0, b, 0, 0))],
    )
    return pl.pallas_call(
        functools.partial(_dispatch_kernel, cap=cap, win=win, nt=nt),
        grid_spec=grid_spec,
        out_shape=[jax.ShapeDtypeStruct((E, B, cap, D), BF16), jax.ShapeDtypeStruct((E, B, cap, 1), F32)],
        compiler_params=_cparams(("parallel", "arbitrary")),
        name="moe_dispatch",
    )(s0, h2, slot_t, aff_t)


def _expert_kernel(x_ref, w1_ref, w3_ref, w2_ref, gate_ref, y_ref, wb1, wb3, wb2, acc_ref, *, tm, nf):
    f = pl.program_id(1)
    m = pl.program_id(2)

    @pl.when(m == 0)
    def _():
        wb1[...] = w1_ref[0, 0].astype(BF16)
        wb3[...] = w3_ref[0, 0].astype(BF16)
        wb2[...] = w2_ref[0, 0].astype(BF16)

    tile = pl.ds(pl.multiple_of(m * tm, tm), tm)

    @pl.when(f == 0)
    def _():
        acc_ref[tile, :] = jnp.zeros((tm, acc_ref.shape[1]), F32)

    sub = min(EXPERT_SUB_ROWS, tm)
    for j in range(tm // sub):
        x = x_ref[0, j * sub:(j + 1) * sub, :]
        a = _dot(x, wb1[...])
        u = _dot(x, wb3[...])
        rows = pl.ds(pl.multiple_of(m * tm + j * sub, sub), sub)
        acc_ref[rows, :] += _dot((_silu(a) * u).astype(BF16), wb2[...])

    @pl.when(f == nf - 1)
    def _():
        y_ref[0] = (acc_ref[tile, :] * gate_ref[0]).astype(BF16)


def _experts(xe, w1, w3, w2, layer, gate, tm, tf=512):
    E, M, D = xe.shape
    F = w1.shape[3]
    nf = F // tf
    out = pl.BlockSpec((1, tm, D), lambda e, f, m: (e, jnp.where(f == nf - 1, m, 0), 0))
    return pl.pallas_call(
        functools.partial(_expert_kernel, tm=tm, nf=nf),
        grid=(E, nf, M // tm),
        in_specs=[pl.BlockSpec((1, tm, D), lambda e, f, m: (e, m, 0)),
                  pl.BlockSpec((1, 1, D, tf), lambda e, f, m: (layer, e, 0, f)),
                  pl.BlockSpec((1, 1, D, tf), lambda e, f, m: (layer, e, 0, f)),
                  pl.BlockSpec((1, 1, tf, D), lambda e, f, m: (layer, e, f, 0)),
                  pl.BlockSpec((1, tm, 1), lambda e, f, m: (e, m, 0))],
        out_specs=out,
        out_shape=jax.ShapeDtypeStruct((E, M, D), BF16),
        scratch_shapes=[pltpu.VMEM((D, tf), BF16), pltpu.VMEM((D, tf), BF16), pltpu.VMEM((tf, D), BF16),
                        pltpu.VMEM((M, D), F32)],
        compiler_params=_cparams(("parallel", "arbitrary", "arbitrary")),
        name="expert_ffn",
    )(xe, w1, w3, w2, gate)


def _combine_kernel(s0_ref, slot_ref, x_ref, g2_ref, fin_ref, y_hbm, o_ref, stage, extra_stage, acc_ref,
                    sem, *, cap, win, nt, final_norm):
    E = N_EXPERTS
    b = pl.program_id(0)
    i = pl.program_id(1)
    step = b * nt + i
    total = pl.num_programs(0) * nt
    buf = step % 2

    def window_copies(src_row, dst, e, s):
        src_row = pl.multiple_of(src_row, SLOT_ALIGN)
        return (pltpu.make_async_copy(y_hbm.at[e, pl.ds(src_row, win)], dst, s),)

    def tile_copies(bb, ii, bf):
        cps = []
        for e in range(E):
            _, _, r0 = _window_start(s0_ref, bb, e, ii, nt, cap, win)
            cps.extend(window_copies(bb * cap + r0, stage.at[bf, pl.ds(e * win, win)], e, sem.at[bf]))
        return cps

    @pl.when(step == 0)
    def _():
        for cp in tile_copies(b, i, 0):
            cp.start()

    nxt = step + 1

    @pl.when(nxt < total)
    def _():
        for cp in tile_copies(nxt // nt, nxt % nt, 1 - buf):
            cp.start()

    for cp in tile_copies(b, i, buf):
        cp.wait()

    first, onehot_t, n_extra, jrow = _first_windows(s0_ref, slot_ref, b, i, nt, cap, win)
    acc_ref[...] = _dot_tn(onehot_t, stage[buf])

    @pl.when(n_extra > 0)
    def _():
        for e in range(E):
            a, end, _, _ = first[e]

            def extra(w, carry, e=e, a=a):
                r0w, pw = _slot_hits(slot_ref, e, a, w, cap, win, jrow)
                cps = window_copies(b * cap + r0w, extra_stage, e, sem.at[2])
                for cp in cps:
                    cp.start()
                for cp in cps:
                    cp.wait()
                acc_ref[...] += _dot_tn(jnp.where(pw, 1.0, 0.0).astype(BF16), extra_stage[...])
                return carry

            lax.fori_loop(1, _n_windows(a, end, cap, win), extra, 0)

    out = x_ref[0] + g2_ref[0] * acc_ref[...]
    if final_norm:
        ms = jnp.mean(out * out, axis=-1, keepdims=True)
        out = out * lax.rsqrt(ms + EPS) * fin_ref[...]
    o_ref[0] = out


def _combine(s0, slot_t, x, g2, y, cap, win, final_g=None):
    B, L, D = x.shape
    E = N_EXPERTS
    T = min(MOE_TILE, L)
    nt = L // T
    grid_spec = pltpu.PrefetchScalarGridSpec(
        num_scalar_prefetch=1,
        grid=(B, nt),
        in_specs=[pl.BlockSpec((1, E, T), lambda b, i, s: (b, 0, i)),
                  pl.BlockSpec((1, T, D), lambda b, i, s: (b, i, 0)),
                  pl.BlockSpec((1, 1, D), lambda b, i, s: (b, 0, 0)),
                  pl.BlockSpec((1, D), lambda b, i, s: (0, 0)),
                  pl.BlockSpec(memory_space=pl.ANY)],
        out_specs=pl.BlockSpec((1, T, D), lambda b, i, s: (b, i, 0)),
        scratch_shapes=[pltpu.VMEM((2, E * win, D), BF16), pltpu.VMEM((win, D), BF16),
                        pltpu.VMEM((T, D), F32), pltpu.SemaphoreType.DMA((3,))],
    )
    return pl.pallas_call(
        functools.partial(_combine_kernel, cap=cap, win=win, nt=nt, final_norm=final_g is not None),
        grid_spec=grid_spec,
        out_shape=jax.ShapeDtypeStruct((B, L, D), F32),
        compiler_params=_cparams(("arbitrary", "arbitrary")),
        name="moe_combine",
    )(s0, slot_t, x, g2, (jnp.ones((D,), F32) if final_g is None else final_g).reshape(1, D), y)


def _moe(x, g2, h2, aff_t, w1, w3, w2, layer, final_g=None):
    B, L, D = h2.shape
    E = N_EXPERTS
    cap = CAPACITY_FACTOR * L // E
    win = min(MOE_WIN, cap)
    nt = L // min(MOE_TILE, L)
    slot_t, s0_pad = _select(aff_t, cap)
    s0 = s0_pad[:, :, :nt + 1].reshape(-1)
    xe, ge = _dispatch(s0, h2, slot_t, aff_t, cap, win)
    y = _experts(xe.reshape(E, B * cap, D), w1, w3, w2, layer, ge.reshape(E, B * cap, 1), tm=min(1024, B * cap))
    return _combine(s0, slot_t, x, g2, y, cap, win, final_g)


def _rope_tables(n):
    half = HEAD_DIM // 2
    quarter = half // 2
    inv = ROPE_BASE ** (-jnp.arange(quarter, dtype=F32) / quarter)
    t = jnp.arange(n)
    row = (t // GRID_W).astype(F32)
    col = (t % GRID_W).astype(F32)
    lane = np.arange(128)
    within = lane % HEAD_DIM
    use_col = jnp.asarray((within >= half)[None, :])
    freq = inv[jnp.asarray(within % quarter)][None, :]
    ang = jnp.where(use_col, col[:, None], row[:, None]) * freq
    sign = jnp.asarray(np.where((lane % half) < quarter, -1.0, 1.0).astype(np.float32))[None, :]
    return jnp.cos(ang), jnp.sin(ang) * sign


def kernel(x, c, ctx, c_ctx, mod_w, mod_b, norm1_g, norm2_g, ab_w_in, ab_w_out, sgu_w, sgu_b, hgrn_lb_logits,
           hgrn_norm_g, attn_w_qkv, attn_w_out, attn_sink, router_w, expert_w1, expert_w3, expert_w2, final_g):
    B, N, D = x.shape
    Lc = ctx.shape[1]
    depth = mod_w.shape[0]

    rows = ((B + 1 + 7) // 8) * 8
    cvec = jnp.zeros((rows, D), F32).at[:B].set(c).at[B].set(c_ctx)
    mod = _mod_vectors(cvec, mod_w, mod_b)

    def mods(l):
        lat = [mod[l, :B, k * D:(k + 1) * D].reshape(B, 1, D) for k in range(6)]
        cx = [jnp.broadcast_to(mod[l, B, k * D:(k + 1) * D].reshape(1, 1, D), (B, 1, D)) for k in range(6)]
        return lat, cx

    lb_all = jnp.cumsum(jax.nn.softmax(hgrn_lb_logits.astype(F32), axis=0), axis=0)
    lmat_f = jnp.asarray(_hgrn_level_matrix(False), BF16)
    lmat_r = jnp.asarray(_hgrn_level_matrix(True), BF16)

    x_lat, x_ctx = x, ctx
    for l in range(depth):
        last = l == depth - 1
        (sh1, sc1, g1, sh2, sc2, g2), (csh1, csc1, cg1, csh2, csc2, cg2) = mods(l)
        wr_t = router_w[l].T
        if l % 2 == 0:
            e = l // 2
            w_in = ab_w_in[e].astype(BF16)
            w_out = ab_w_out[e].astype(BF16)
            sw = sgu_w[e].astype(BF16)
            sb_t = sgu_b[e].T
            hg = hgrn_norm_g[e].reshape(1, HGRN_DIM)
            s0 = jnp.zeros((2, B, HGRN_HEADS, HGRN_KDIM, HGRN_KDIM), F32)

            def even(xs, shift, scale, gate1, shift2, scale2, s_in):
                z = _proj(xs, norm1_g[l], shift, scale, w_in)
                ya = _sgu(z, sw, sb_t)
                o_f, o_b, s_fin = _hgrn(z, lb_all[e], s_in, lmat_f, lmat_r)
                xn, h2, aff = _even_post(ya, o_f, o_b, z, hg, w_out, xs, gate1, norm2_g[l], shift2, scale2, wr_t)
                return xn, h2, aff, s_fin

            x_ctx, h2c, affc, s_ctx = even(x_ctx, csh1, csc1, cg1, csh2, csc2, s0)
            x_lat, h2l, affl, _ = even(x_lat, sh1, sc1, g1, sh2, sc2, s_ctx)
        else:
            if not last:
                raise NotImplementedError("context update after an attention layer")
            o_idx = l // 2
            w_qkv = attn_w_qkv[o_idx].astype(BF16)
            w_out = attn_w_out[o_idx].astype(BF16)
            cos, sin = _rope_tables(N)
            zc = _proj(x_ctx, norm1_g[l], csh1, csc1, w_qkv[:, ATTN_Q_DIM:])
            k_ctx = zc[..., :ATTN_KV_DIM].astype(BF16)
            v_ctx = zc[..., ATTN_KV_DIM:].astype(BF16)
            q, k, v = _qkv_lat(x_lat, norm1_g[l], sh1, sc1, w_qkv, cos, sin)
            o = _attention(q, k, v, k_ctx, v_ctx, attn_sink[o_idx])
            x_lat, h2l, affl = _odd_post(o, w_out, x_lat, g1, norm2_g[l], sh2, sc2, wr_t)
        x_lat = _moe(x_lat, g2, h2l, affl, expert_w1, expert_w3, expert_w2, l, final_g if last else None)
        if not last:
            x_ctx = _moe(x_ctx, cg2, h2c, affc, expert_w1, expert_w3, expert_w2, l)
    return x_lat
```

```python
import functools

import numpy as np
import jax
import jax.numpy as jnp
from jax import lax
from jax.experimental import pallas as pl
from jax.experimental.pallas import tpu as pltpu

F32 = jnp.float32
BF16 = jnp.bfloat16

D_MODEL = 1024
GRID_W = 64
EPS = 1e-6
SGU_GROUPS = 4
SGU_GROUP_DIM = 128
SGU_DIM = SGU_GROUPS * SGU_GROUP_DIM
SGU_CHUNK = 128
HGRN_HEADS = 4
HGRN_KDIM = 128
HGRN_DIM = HGRN_HEADS * HGRN_KDIM
HGRN_CHUNK = 64
HGRN_LEVELS = 6
HGRN_COARSE = 3
HGRN_TBLK = 256
AB_IN = 2 * SGU_DIM + 5 * HGRN_DIM
ATTN_HEADS = 16
ATTN_KV_HEADS = 4
ATTN_GROUP = ATTN_HEADS // ATTN_KV_HEADS
HEAD_DIM = 64
ATTN_Q_DIM = ATTN_HEADS * HEAD_DIM
ATTN_KV_DIM = ATTN_KV_HEADS * HEAD_DIM
WINDOW = 128
ATTN_BLOCK = 128
ROPE_BASE = 10000.0
N_EXPERTS = 16
CAPACITY_FACTOR = 2
EXPERT_FF = 2048
NEG_BIG = -1e30
LOG2_E = 1.4426950408889634

VMEM_LIMIT_BYTES = 56 * 1024 * 1024


ROW_TILE = 512


def _row_tile(n_rows):
    return min(ROW_TILE, n_rows)


def _cparams(sem):
    return pltpu.CompilerParams(dimension_semantics=sem, vmem_limit_bytes=VMEM_LIMIT_BYTES)


def _dot(a, b):
    return jnp.dot(a, b, preferred_element_type=F32)


def _dot_nt(a, b):
    return lax.dot_general(a, b, (((1,), (1,)), ((), ())), preferred_element_type=F32)


def _dot_tn(a, b):
    return lax.dot_general(a, b, (((0,), (0,)), ((), ())), preferred_element_type=F32)


def _split2(a):
    hi = a.astype(BF16)
    lo = (a - hi.astype(F32)).astype(BF16)
    return hi, lo


def _dot_f32x3(a, b):
    a_hi, a_lo = _split2(a)
    b_hi, b_lo = _split2(b)
    return _dot(a_hi, b_hi) + (_dot(a_hi, b_lo) + _dot(a_lo, b_hi))


def _silu(x):
    return x * jax.nn.sigmoid(x)


def _gelu(x):
    return 0.5 * x * (1.0 + lax.erf(x * np.float32(1.0 / np.sqrt(2.0))))


def _rms_modulate(x, g, shift, scale):
    ms = jnp.mean(x * x, axis=-1, keepdims=True)
    return (x * lax.rsqrt(ms + EPS) * g) * (1.0 + scale) + shift


def _mod_kernel(c_ref, w_ref, b_ref, o_ref):
    s = _silu(c_ref[...])
    o_ref[0] = _dot_f32x3(s, w_ref[0]) + b_ref[0]


def _mod_vectors(cvec, mod_w, mod_b):
    depth, d, n6 = mod_w.shape
    rows = cvec.shape[0]
    tn = 1536
    return pl.pallas_call(
        _mod_kernel,
        grid=(depth, n6 // tn),
        in_specs=[
            pl.BlockSpec((rows, d), lambda l, j: (0, 0)),
            pl.BlockSpec((1, d, tn), lambda l, j: (l, 0, j)),
            pl.BlockSpec((1, 1, tn), lambda l, j: (l, 0, j)),
        ],
        out_specs=pl.BlockSpec((1, rows, tn), lambda l, j: (l, 0, j)),
        out_shape=jax.ShapeDtypeStruct((depth, rows, n6), F32),
        compiler_params=_cparams(("arbitrary", "arbitrary")),
        name="mod_vectors",
    )(cvec, mod_w, mod_b.reshape(depth, 1, n6))


def _proj_kernel(x_ref, g_ref, sh_ref, sc_ref, w_ref, z_ref):
    h = _rms_modulate(x_ref[0], g_ref[...], sh_ref[0], sc_ref[0])
    z_ref[0] = _dot(h.astype(BF16), w_ref[...])


def _row_spec(tm, d):
    return pl.BlockSpec((1, tm, d), lambda b, i: (b, i, 0))


def _vec_spec(d):
    return pl.BlockSpec((1, 1, d), lambda b, i: (b, 0, 0))


def _const_spec(shape):
    nd = len(shape)
    return pl.BlockSpec(shape, lambda b, i: (0,) * nd)


def _proj(x, g, shift, scale, w):
    B, L, D = x.shape
    tm = _row_tile(L)
    n = w.shape[1]
    return pl.pallas_call(
        _proj_kernel,
        grid=(B, L // tm),
        in_specs=[_row_spec(tm, D), _const_spec((1, D)), _vec_spec(D), _vec_spec(D), _const_spec((D, n))],
        out_specs=_row_spec(tm, n),
        out_shape=jax.ShapeDtypeStruct((B, L, n), F32),
        compiler_params=_cparams(("parallel", "parallel")),
        name="norm_mod_proj",
    )(x, g.reshape(1, D), shift, scale, w)


def _sgu_rows(zu, zv, w_ref, b_ref, o_ref, rows):
    u = _gelu(zu)
    v = _gelu(zv)
    for g in range(SGU_GROUPS):
        cols = slice(g * SGU_GROUP_DIM, (g + 1) * SGU_GROUP_DIM)
        vg = v[:, cols]
        mu = jnp.mean(vg, axis=-1, keepdims=True)
        vc = vg - mu
        var = jnp.mean(vc * vc, axis=-1, keepdims=True)
        vn = (vc * lax.rsqrt(var + EPS)).astype(BF16)
        s = _dot(w_ref[g], vn) + b_ref[:, g:g + 1]
        o_ref[0, rows, cols] = (u[:, cols] * s).astype(BF16)


def _proj_sgu_kernel(x_ref, g_ref, sh_ref, sc_ref, w_ref, sw_ref, sb_ref, ya_ref, zr_ref, *, tm):
    h = _rms_modulate(x_ref[0], g_ref[...], sh_ref[0], sc_ref[0]).astype(BF16)
    z_uv = _dot(h, w_ref[:, :2 * SGU_DIM])
    zr_ref[0] = _dot(h, w_ref[:, 2 * SGU_DIM:])
    for ch in range(tm // SGU_CHUNK):
        rows = slice(ch * SGU_CHUNK, (ch + 1) * SGU_CHUNK)
        _sgu_rows(z_uv[rows, :SGU_DIM], z_uv[rows, SGU_DIM:], sw_ref, sb_ref, ya_ref, rows)


def _proj_sgu(x, g, shift, scale, w, sgu_w_bf16, sgu_b_t):
    B, L, D = x.shape
    tm = _row_tile(L)
    n_rest = w.shape[1] - 2 * SGU_DIM
    return pl.pallas_call(
        functools.partial(_proj_sgu_kernel, tm=tm),
        grid=(B, L // tm),
        in_specs=[_row_spec(tm, D), _const_spec((1, D)), _vec_spec(D), _vec_spec(D), _const_spec((D, w.shape[1])),
                  _const_spec((SGU_GROUPS, SGU_CHUNK, SGU_CHUNK)), _const_spec((SGU_CHUNK, SGU_GROUPS))],
        out_specs=[_row_spec(tm, SGU_DIM), _row_spec(tm, n_rest)],
        out_shape=[jax.ShapeDtypeStruct((B, L, SGU_DIM), BF16), jax.ShapeDtypeStruct((B, L, n_rest), F32)],
        compiler_params=_cparams(("parallel", "parallel")),
        name="norm_mod_proj_sgu",
    )(x, g.reshape(1, D), shift, scale, w, sgu_w_bf16, sgu_b_t)


def _hgrn_level_matrix(reverse):
    C = HGRN_CHUNK
    t = np.arange(C)[:, None]
    s = np.arange(C)[None, :]
    blocks = []
    blocks.append(((s <= t) if not reverse else (s >= t)).astype(np.float32))
    for l in range(HGRN_COARSE, HGRN_LEVELS):
        m = C >> (l + 1)
        base = (t // (2 * m)) * (2 * m)
        if not reverse:
            r = base + m - 1
            qrow = (t % (2 * m)) >= m
            mq = (s > r) & (s <= t)
            mk = (s > t) & (s <= r)
        else:
            r = base + m
            qrow = (t % (2 * m)) < m
            mq = (s >= t) & (s < r)
            mk = (s >= r) & (s < t)
        blocks.append(np.where(qrow, mq, mk).astype(np.float32))
    return np.concatenate(blocks, axis=0)


def _hgrn_masks(reverse):
    C = HGRN_CHUNK
    t = lax.broadcasted_iota(jnp.int32, (C, C), 0)
    s = lax.broadcasted_iota(jnp.int32, (C, C), 1)
    r = lax.broadcasted_iota(jnp.int32, (C, 1), 0)
    masks, qrows = [], []
    for l in range(HGRN_LEVELS):
        m = C >> (l + 1)
        same = (t // (2 * m)) == (s // (2 * m))
        if not reverse:
            mk = same & ((t % (2 * m)) >= m) & ((s % (2 * m)) < m)
            qr = (r % (2 * m)) >= m
        else:
            mk = same & ((t % (2 * m)) < m) & ((s % (2 * m)) >= m)
            qr = (r % (2 * m)) < m
        masks.append(mk)
        qrows.append(qr)
    return masks, qrows


def _hgrn_chunk(zq, zf, zi, lb, st, lmat, masks, qrows, reverse):
    C = HGRN_CHUNK
    f = lb + (1.0 - lb) * jax.nn.sigmoid(zf)
    g = jnp.log(f)
    kk = 1.0 - f
    q = _silu(zq)
    vb = zi.astype(BF16)
    g_hi, g_lo = _split2(g)
    d2 = _dot(lmat, jnp.concatenate([g_hi, g_lo], axis=1))
    d = d2[:, :HGRN_KDIM] + d2[:, HGRN_KDIM:]
    b = d[0:C]
    b_end = b[0:1] if reverse else b[C - 1:C]
    dec = jnp.exp(b_end)
    qb = (q * jnp.exp(b)).astype(BF16)
    kb = (kk * jnp.exp(b_end - b)).astype(BF16)
    o = _dot_nt(qb, st.astype(BF16))
    o = o + jnp.sum(q * kk, axis=1, keepdims=True) * zi
    a = None
    for l in range(HGRN_LEVELS):
        if l < HGRN_COARSE:
            m = C >> (l + 1)
            r = m if reverse else m - 1
            ref = jnp.broadcast_to(b.reshape(C // (2 * m), 2 * m, HGRN_KDIM)[:, r:r + 1, :],
                                   (C // (2 * m), 2 * m, HGRN_KDIM)).reshape(C, HGRN_KDIM)
            dl = jnp.where(qrows[l], b - ref, ref - b)
        else:
            dl = d[(1 + l - HGRN_COARSE) * C:(2 + l - HGRN_COARSE) * C]
        mix = (jnp.where(qrows[l], q, kk) * jnp.exp(dl)).astype(BF16)
        p = jnp.where(masks[l], _dot_nt(mix, mix), 0.0)
        a = p if a is None else a + p
    o = o + _dot(a.astype(BF16), vb)
    st_new = st * dec + _dot_tn(vb, kb)
    return o, st_new


def _hgrn_kernel(zqf_ref, zff_ref, zif_ref, zqb_ref, zfb_ref, zib_ref, lb_ref, s0_ref, lf_ref, lr_ref,
                 of_ref, ob_ref, sfin_ref, st_ref, *, tb):
    i = pl.program_id(1)
    nb = pl.num_programs(1)

    @pl.when(i == 0)
    def _():
        st_ref[...] = s0_ref[:, 0]

    C = HGRN_CHUNK
    nch = tb // C
    for d in range(2):
        reverse = d == 1
        masks, qrows = _hgrn_masks(reverse)
        lmat = (lr_ref if reverse else lf_ref)[...]
        zq_ref, zf_ref, zi_ref, o_ref = ((zqb_ref, zfb_ref, zib_ref, ob_ref) if reverse
                                         else (zqf_ref, zff_ref, zif_ref, of_ref))
        for h in range(HGRN_HEADS):
            cols = slice(h * HGRN_KDIM, (h + 1) * HGRN_KDIM)
            lb = lb_ref[d:d + 1, cols]
            st = st_ref[d, h]
            order = range(nch - 1, -1, -1) if reverse else range(nch)
            for c in order:
                rows = slice(c * C, (c + 1) * C)
                o, st = _hgrn_chunk(zq_ref[0, rows, cols], zf_ref[0, rows, cols], zi_ref[0, rows, cols],
                                    lb, st, lmat, masks, qrows, reverse)
                o_ref[0, rows, cols] = o
            st_ref[d, h] = st

    @pl.when(i == nb - 1)
    def _():
        sfin_ref[:, 0] = st_ref[...]


def _hgrn(z, lb, s0, lmat_f, lmat_r):
    B, L, _ = z.shape
    tb = min(HGRN_TBLK, L)
    nb = L // tb
    nl = lmat_f.shape[0]

    def zspec(col, rev):
        if rev:
            return pl.BlockSpec((1, tb, HGRN_DIM), lambda b, i: (b, nb - 1 - i, col))
        return pl.BlockSpec((1, tb, HGRN_DIM), lambda b, i: (b, i, col))

    st_spec = pl.BlockSpec((2, 1, HGRN_HEADS, HGRN_KDIM, HGRN_KDIM), lambda b, i: (0, b, 0, 0, 0))
    return pl.pallas_call(
        functools.partial(_hgrn_kernel, tb=tb),
        grid=(B, nb),
        in_specs=[zspec(0, False), zspec(1, False), zspec(3, False),
                  zspec(0, True), zspec(2, True), zspec(3, True),
                  _const_spec((2, HGRN_DIM)), st_spec,
                  _const_spec((nl, HGRN_CHUNK)), _const_spec((nl, HGRN_CHUNK))],
        out_specs=[pl.BlockSpec((1, tb, HGRN_DIM), lambda b, i: (b, i, 0)),
                   pl.BlockSpec((1, tb, HGRN_DIM), lambda b, i: (b, nb - 1 - i, 0)),
                   st_spec],
        out_shape=[jax.ShapeDtypeStruct((B, L, HGRN_DIM), F32),
                   jax.ShapeDtypeStruct((B, L, HGRN_DIM), F32),
                   jax.ShapeDtypeStruct(s0.shape, F32)],
        scratch_shapes=[pltpu.VMEM((2, HGRN_HEADS, HGRN_KDIM, HGRN_KDIM), F32)],
        compiler_params=_cparams(("parallel", "arbitrary")),
        name="hgrn2_scan",
    )(z, z, z, z, z, z, lb, s0, lmat_f, lmat_r)


def _residual_router_tail(x, y, g1, n2g, sh2, sc2, wr, xo_ref, h2_ref, aff_ref):
    xn = x + g1 * y
    xo_ref[0] = xn
    h2 = _rms_modulate(xn, n2g, sh2, sc2)
    h2_ref[0] = h2.astype(BF16)
    w_hi, w_lo = _split2(wr)
    h_hi, h_lo = _split2(h2)
    logits = _dot_nt(w_hi, h_hi) + (_dot_nt(w_hi, h_lo) + _dot_nt(w_lo, h_hi))
    mx = jnp.max(logits, axis=0, keepdims=True)
    p = jnp.exp(logits - mx)
    aff_ref[0] = p / jnp.sum(p, axis=0, keepdims=True)


def _even_post_kernel(ya_ref, of_ref, ob_ref, zg_ref, hg_ref, w_ref, x_ref, g1_ref, n2g_ref, sh2_ref,
                      sc2_ref, wr_ref, xo_ref, h2_ref, aff_ref):
    o = of_ref[0] + ob_ref[0]
    zg = zg_ref[0]
    parts = []
    for h in range(HGRN_HEADS):
        cols = slice(h * HGRN_KDIM, (h + 1) * HGRN_KDIM)
        oh = o[:, cols]
        ms = jnp.mean(oh * oh, axis=-1, keepdims=True)
        parts.append(oh * lax.rsqrt(ms + EPS) * hg_ref[:, cols])
    yb = (jnp.concatenate(parts, axis=1) * _silu(zg)).astype(BF16)
    y = _dot(ya_ref[0], w_ref[:SGU_DIM, :]) + _dot(yb, w_ref[SGU_DIM:, :])
    _residual_router_tail(x_ref[0], y, g1_ref[0], n2g_ref[...], sh2_ref[0], sc2_ref[0], wr_ref[...],
                          xo_ref, h2_ref, aff_ref)


def _odd_post_kernel(o_ref, w_ref, x_ref, g1_ref, n2g_ref, sh2_ref, sc2_ref, wr_ref, xo_ref, h2_ref,
                     aff_ref):
    y = _dot(o_ref[0], w_ref[...])
    _residual_router_tail(x_ref[0], y, g1_ref[0], n2g_ref[...], sh2_ref[0], sc2_ref[0], wr_ref[...],
                          xo_ref, h2_ref, aff_ref)


def _post_out_specs(B, L, tm):
    D = D_MODEL
    specs = [_row_spec(tm, D), _row_spec(tm, D), pl.BlockSpec((1, N_EXPERTS, tm), lambda b, i: (b, 0, i))]
    shapes = [jax.ShapeDtypeStruct((B, L, D), F32), jax.ShapeDtypeStruct((B, L, D), BF16),
              jax.ShapeDtypeStruct((B, N_EXPERTS, L), F32)]
    return specs, shapes


def _even_post(ya, o_f, o_b, z, hg, w_out, x, g1, n2g, sh2, sc2, wr):
    B, L, D = x.shape
    tm = _row_tile(L)
    specs, shapes = _post_out_specs(B, L, tm)
    return pl.pallas_call(
        _even_post_kernel,
        grid=(B, L // tm),
        in_specs=[_row_spec(tm, SGU_DIM), _row_spec(tm, HGRN_DIM), _row_spec(tm, HGRN_DIM),
                  pl.BlockSpec((1, tm, HGRN_DIM), lambda b, i: (b, i, 4)),
                  _const_spec((1, HGRN_DIM)), _const_spec((D, D)), _row_spec(tm, D), _vec_spec(D),
                  _const_spec((1, D)), _vec_spec(D), _vec_spec(D), _const_spec((N_EXPERTS, D))],
        out_specs=specs,
        out_shape=shapes,
        compiler_params=_cparams(("parallel", "parallel")),
        name="even_post_router",
    )(ya, o_f, o_b, z, hg, w_out, x, g1, n2g.reshape(1, D), sh2, sc2, wr)


def _odd_post(o, w_out, x, g1, n2g, sh2, sc2, wr):
    B, L, D = x.shape
    tm = _row_tile(L)
    specs, shapes = _post_out_specs(B, L, tm)
    return pl.pallas_call(
        _odd_post_kernel,
        grid=(B, L // tm),
        in_specs=[_row_spec(tm, ATTN_Q_DIM), _const_spec((ATTN_Q_DIM, D)), _row_spec(tm, D), _vec_spec(D),
                  _const_spec((1, D)), _vec_spec(D), _vec_spec(D), _const_spec((N_EXPERTS, D))],
        out_specs=specs,
        out_shape=shapes,
        compiler_params=_cparams(("parallel", "parallel")),
        name="odd_post_router",
    )(o, w_out, x, g1, n2g.reshape(1, D), sh2, sc2, wr)


def _rope(x, cos, sin_signed):
    lane = lax.broadcasted_iota(jnp.int32, (x.shape[0], 128), 1)
    first = (lane % 32) < 16
    outs = []
    for j in range(x.shape[1] // 128):
        xb = x[:, j * 128:(j + 1) * 128]
        sw = jnp.where(first, pltpu.roll(xb, 112, 1), pltpu.roll(xb, 16, 1))
        outs.append(xb * cos + sw * sin_signed)
    return jnp.concatenate(outs, axis=1)


def _qkv_lat_kernel(x_ref, g_ref, sh_ref, sc_ref, w_ref, cos_ref, sin_ref, q_ref, k_ref, v_ref):
    h = _rms_modulate(x_ref[0], g_ref[...], sh_ref[0], sc_ref[0])
    z = _dot(h.astype(BF16), w_ref[...])
    cos = cos_ref[...]
    sin = sin_ref[...]
    scale = np.float32(HEAD_DIM ** -0.5 * LOG2_E)
    q_ref[0] = (_rope(z[:, :ATTN_Q_DIM], cos, sin) * scale).astype(BF16)
    k_ref[0] = _rope(z[:, ATTN_Q_DIM:ATTN_Q_DIM + ATTN_KV_DIM], cos, sin).astype(BF16)
    v_ref[0] = z[:, ATTN_Q_DIM + ATTN_KV_DIM:].astype(BF16)


def _qkv_lat(x, g, shift, scale, w, cos, sin):
    B, L, D = x.shape
    tm = _row_tile(L)
    n = w.shape[1]
    tab = pl.BlockSpec((tm, 128), lambda b, i: (i, 0))
    return pl.pallas_call(
        _qkv_lat_kernel,
        grid=(B, L // tm),
        in_specs=[_row_spec(tm, D), _const_spec((1, D)), _vec_spec(D), _vec_spec(D), _const_spec((D, n)), tab, tab],
        out_specs=[_row_spec(tm, ATTN_Q_DIM), _row_spec(tm, ATTN_KV_DIM), _row_spec(tm, ATTN_KV_DIM)],
        out_shape=[jax.ShapeDtypeStruct((B, L, ATTN_Q_DIM), BF16),
                   jax.ShapeDtypeStruct((B, L, ATTN_KV_DIM), BF16),
                   jax.ShapeDtypeStruct((B, L, ATTN_KV_DIM), BF16)],
        compiler_params=_cparams(("parallel", "parallel")),
        name="norm_qkv_rope",
    )(x, g.reshape(1, D), shift, scale, w, cos, sin)


def _attn_kernel(q_ref, kp_ref, kc_ref, kn_ref, vp_ref, vc_ref, vn_ref, kx_ref, vx_ref, sink_ref, o_ref,
                 *, n_tokens):
    bi = pl.program_id(1)
    blk = ATTN_BLOCK
    n_loc = 3 * blk
    nk = n_loc + kx_ref.shape[1]
    n_pairs = ATTN_GROUP // 2
    nq = n_pairs * blk
    pad = 16
    s_idx = lax.broadcasted_iota(jnp.int32, (n_loc, nq), 0)
    r_idx = lax.broadcasted_iota(jnp.int32, (n_loc, nq), 1) % blk
    j = bi * blk - blk + s_idx
    valid = (jnp.abs(s_idx - r_idx - blk) <= WINDOW) & (j >= 0) & (j < n_tokens)
    lane = lax.broadcasted_iota(jnp.int32, (nk, 128), 1)
    qcol = lax.broadcasted_iota(jnp.int32, (1, nq), 1)
    prow = lax.broadcasted_iota(jnp.int32, (pad, nq), 0)
    er = lax.broadcasted_iota(jnp.int32, (2 * nk + pad, 128), 0)
    el = lax.broadcasted_iota(jnp.int32, (2 * nk + pad, 128), 1)
    lo_rows = (er < nk) | (er == 2 * nk)
    hi_rows = ((er >= nk) & (er < 2 * nk)) | (er == 2 * nk + 1)
    ind = jnp.where((lo_rows & (el < HEAD_DIM)) | (hi_rows & (el >= HEAD_DIM)), 1.0, 0.0).astype(BF16)
    zpad = jnp.zeros((pad, 128), BF16)
    scores, values = [], []
    for kvh in range(ATTN_KV_HEADS):
        cols = slice((kvh // 2) * 128, (kvh // 2 + 1) * 128)
        own = (lane < HEAD_DIM) if kvh % 2 == 0 else (lane >= HEAD_DIM)
        k_all = jnp.concatenate([kp_ref[0, :, cols], kc_ref[0, :, cols], kn_ref[0, :, cols], kx_ref[0, :, cols]],
                                axis=0)
        v_all = jnp.concatenate([vp_ref[0, :, cols], vc_ref[0, :, cols], vn_ref[0, :, cols], vx_ref[0, :, cols]],
                                axis=0)
        k_own = jnp.where(own, k_all, jnp.zeros_like(k_all))
        v_own = jnp.where(own, v_all, jnp.zeros_like(v_all))
        k_oth = pltpu.roll(k_own, HEAD_DIM, 1)
        v_oth = pltpu.roll(v_own, HEAD_DIM, 1)
        if kvh % 2 == 0:
            k_cat = jnp.concatenate([k_own, k_oth], axis=0)
            v_cat = jnp.concatenate([v_own, v_oth, zpad], axis=0)
        else:
            k_cat = jnp.concatenate([k_oth, k_own], axis=0)
            v_cat = jnp.concatenate([v_oth, v_own, zpad], axis=0)
        q2 = jnp.concatenate([q_ref[0, :, (kvh * n_pairs + p) * 128:(kvh * n_pairs + p + 1) * 128]
                              for p in range(n_pairs)], axis=0)
        scores.append(_dot_nt(k_cat, q2))
        values.append(jnp.concatenate([v_cat, ind], axis=1))
    probs = []
    for kvh in range(ATTN_KV_HEADS):
        st = scores[kvh]
        ps, es = [], []
        for sub in range(2):
            sh = st[sub * nk:(sub + 1) * nk]
            sh = jnp.concatenate([jnp.where(valid, sh[:n_loc], NEG_BIG), sh[n_loc:]], axis=0)
            hd0 = kvh * ATTN_GROUP + sub
            sink = sink_ref[:, hd0:hd0 + 1]
            for p in range(1, n_pairs):
                sink = jnp.where(qcol < p * blk, sink, sink_ref[:, hd0 + 2 * p:hd0 + 2 * p + 1])
            m = jnp.maximum(jnp.max(sh, axis=0, keepdims=True), sink)
            ps.append(jnp.exp2(sh - m).astype(BF16))
            es.append(jnp.exp2(sink - m))
        sink_rows = jnp.where(prow == 0, es[0], jnp.where(prow == 1, es[1], 0.0)).astype(BF16)
        probs.append(jnp.concatenate(ps + [sink_rows], axis=0))
    for kvh in range(ATTN_KV_HEADS):
        od = _dot_tn(probs[kvh], values[kvh])
        o = (od[:, :128] / od[:, 128:]).astype(BF16)
        for p in range(n_pairs):
            o_ref[0, :, (kvh * n_pairs + p) * 128:(kvh * n_pairs + p + 1) * 128] = o[p * blk:(p + 1) * blk]


def _attention(q, k, v, k_ctx, v_ctx, sink):
    B, N, _ = q.shape
    Lc = k_ctx.shape[1]
    nb = N // ATTN_BLOCK
    blk = ATTN_BLOCK

    def kv_spec(off):
        return pl.BlockSpec((1, blk, ATTN_KV_DIM),
                            lambda b, i: (b, jnp.clip(i + off, 0, nb - 1), 0))

    ctx_spec = pl.BlockSpec((1, Lc, ATTN_KV_DIM), lambda b, i: (b, 0, 0))
    return pl.pallas_call(
        functools.partial(_attn_kernel, n_tokens=N),
        grid=(B, nb),
        in_specs=[_row_spec(blk, ATTN_Q_DIM), kv_spec(-1), kv_spec(0), kv_spec(1),
                  kv_spec(-1), kv_spec(0), kv_spec(1), ctx_spec, ctx_spec,
                  _const_spec((1, ATTN_HEADS))],
        out_specs=_row_spec(blk, ATTN_Q_DIM),
        out_shape=jax.ShapeDtypeStruct((B, N, ATTN_Q_DIM), BF16),
        compiler_params=_cparams(("parallel", "parallel")),
        name="window_attention",
    )(q, k, k, k, v, v, v, k_ctx, v_ctx, (sink * np.float32(LOG2_E)).reshape(1, ATTN_HEADS))


MOE_TILE = 256
MOE_WIN = 64
SLOT_ALIGN = 16
EXPERT_ROW_TILE_MAX = 1152


def _select_kernel(aff_ref, slot_ref, s0_ref, *, cap):
    x = aff_ref[0]
    E, L = x.shape
    T = min(MOE_TILE, L)

    def count(mask):
        return jnp.sum(jnp.where(mask, 1.0, 0.0), axis=1, keepdims=True)

    def as_f32(bits):
        return lax.bitcast_convert_type(bits, F32)

    def body(_, c):
        lo, hi = c
        mid = lo + lax.shift_right_logical(hi - lo, 1)
        ok = count(x >= as_f32(mid)) >= cap
        return jnp.where(ok, mid, lo), jnp.where(ok, hi, mid)

    lo0 = jnp.zeros((E, 1), jnp.int32)
    hi0 = jnp.full((E, 1), 0x7F800000, jnp.int32)
    lo, hi = lax.fori_loop(0, 31, body, (lo0, hi0))
    gt = x >= as_f32(hi)
    eq = (x >= as_f32(lo)) & jnp.logical_not(gt)
    need = cap - count(gt)
    ti = lax.broadcasted_iota(jnp.int32, (T, T), 0)
    tj = lax.broadcasted_iota(jnp.int32, (T, T), 1)
    upper = jnp.where(ti < tj, 1.0, 0.0).astype(BF16)

    def excl_cumsum(mask):
        carry = jnp.zeros((E, 1), F32)
        outs, starts = [], []
        for t in range(L // T):
            v = jnp.where(mask[:, t * T:(t + 1) * T], 1.0, 0.0)
            starts.append(carry)
            outs.append(_dot(v.astype(BF16), upper) + carry)
            carry = carry + jnp.sum(v, axis=1, keepdims=True)
        starts.append(carry)
        return jnp.concatenate(outs, axis=1), starts

    eq_rank, _ = excl_cumsum(eq)
    sel = gt | (eq & (eq_rank < need))
    slot, starts = excl_cumsum(sel)
    slot_ref[0] = jnp.where(sel, slot, -1.0).astype(jnp.int32)
    lane = lax.broadcasted_iota(jnp.int32, (E, 128), 1)
    s0 = jnp.zeros((E, 128), F32)
    for t, st in enumerate(starts):
        s0 = jnp.where(lane == t, st, s0)
    s0_ref[0] = s0.astype(jnp.int32)


def _select(aff_t, cap):
    B, E, L = aff_t.shape
    assert L // min(MOE_TILE, L) + 1 <= 128
    return pl.pallas_call(
        functools.partial(_select_kernel, cap=cap),
        grid=(B,),
        in_specs=[pl.BlockSpec((1, E, L), lambda b: (b, 0, 0))],
        out_specs=[pl.BlockSpec((1, E, L), lambda b: (b, 0, 0)), pl.BlockSpec((1, E, 128), lambda b: (b, 0, 0))],
        out_shape=[jax.ShapeDtypeStruct((B, E, L), jnp.int32), jax.ShapeDtypeStruct((B, E, 128), jnp.int32)],
        compiler_params=_cparams(("parallel",)),
        name="moe_select",
    )(aff_t)


def _window_start(s0_ref, b, e, i, nt, cap, win):
    base = (b * N_EXPERTS + e) * (nt + 1) + i
    a = (s0_ref[base] // SLOT_ALIGN) * SLOT_ALIGN
    return a, s0_ref[base + 1], jnp.minimum(a, cap - win)


def _n_windows(a, end, cap, win):
    return jnp.where(a + win >= cap, 1, jnp.maximum((end - a + win - 1) // win, 1))


def _slot_hits(slot_ref, e, a, w, cap, win, jrow):
    lo_s = a + w * win
    r0 = jnp.minimum(lo_s, cap - win)
    se = slot_ref[0, e:e + 1, :]
    return r0, ((se - r0) == jrow) & (se >= lo_s)


def _first_windows(s0_ref, slot_ref, b, i, nt, cap, win):
    T = slot_ref.shape[2]
    jrow = lax.broadcasted_iota(jnp.int32, (win, T), 0)
    first = []
    n_extra = 0
    for e in range(N_EXPERTS):
        a, end, _ = _window_start(s0_ref, b, e, i, nt, cap, win)
        r0, p = _slot_hits(slot_ref, e, a, 0, cap, win, jrow)
        first.append((a, end, r0, p))
        n_extra = n_extra + (_n_windows(a, end, cap, win) - 1)
    onehot_t = jnp.where(jnp.concatenate([f[3] for f in first], axis=0), 1.0, 0.0).astype(BF16)
    return first, onehot_t, n_extra, jrow


def _dispatch_kernel(s0_ref, h_ref, slot_ref, aff_ref, *rest, cap, win, nt):
    xe_ref, ge_ref = rest[-2:]
    b = pl.program_id(0)
    i = pl.program_id(1)

    @pl.when(i == 0)
    def _():
        xe_ref[...] = jnp.zeros_like(xe_ref)
        ge_ref[...] = jnp.zeros_like(ge_ref)

    h = h_ref[0]

    def place(e, r0, p, rows):
        r0 = pl.multiple_of(r0, SLOT_ALIGN)
        gate = jnp.sum(jnp.where(p, aff_ref[0, e:e + 1, :], 0.0), axis=1, keepdims=True)
        xe_ref[e, pl.ds(r0, win), :] += rows.astype(BF16)
        ge_ref[e, pl.ds(r0, win), :] += gate

    first, onehot_t, n_extra, jrow = _first_windows(s0_ref, slot_ref, b, i, nt, cap, win)
    rows = _dot(onehot_t, h)
    for e in range(N_EXPERTS):
        _, _, r0, p = first[e]
        place(e, r0, p, rows[e * win:(e + 1) * win])

    @pl.when(n_extra > 0)
    def _():
        for e in range(N_EXPERTS):
            a, end, _, _ = first[e]

            def extra(w, carry, e=e, a=a):
                r0w, pw = _slot_hits(slot_ref, e, a, w, cap, win, jrow)
                place(e, r0w, pw, _dot(jnp.where(pw, 1.0, 0.0).astype(BF16), h))
                return carry

            lax.fori_loop(1, _n_windows(a, end, cap, win), extra, 0)


def _dispatch(s0, h2, slot_t, aff_t, cap, win, rows_total, row_base, prev=None):
    B, L, D = h2.shape
    E = N_EXPERTS
    T = min(MOE_TILE, L)
    nt = L // T
    assert row_base % cap == 0
    blk0 = row_base // cap
    in_specs = [pl.BlockSpec((1, T, D), lambda b, i, s: (b, i, 0)),
                pl.BlockSpec((1, E, T), lambda b, i, s: (b, 0, i)),
                pl.BlockSpec((1, E, T), lambda b, i, s: (b, 0, i))]
    args = [s0, h2, slot_t, aff_t]
    aliases = {}
    if prev is not None:
        in_specs += [pl.BlockSpec(memory_space=pl.ANY), pl.BlockSpec(memory_space=pl.ANY)]
        aliases = {len(args): 0, len(args) + 1: 1}
        args += list(prev)
    grid_spec = pltpu.PrefetchScalarGridSpec(
        num_scalar_prefetch=1,
        grid=(B, nt),
        in_specs=in_specs,
        out_specs=[pl.BlockSpec((E, cap, D), lambda b, i, s: (0, blk0 + b, 0)),
                   pl.BlockSpec((E, cap, 1), lambda b, i, s: (0, blk0 + b, 0))],
    )
    return pl.pallas_call(
        functools.partial(_dispatch_kernel, cap=cap, win=win, nt=nt),
        grid_spec=grid_spec,
        out_shape=[jax.ShapeDtypeStruct((E, rows_total, D), BF16), jax.ShapeDtypeStruct((E, rows_total, 1), F32)],
        input_output_aliases=aliases,
        compiler_params=_cparams(("parallel", "arbitrary")),
        name="moe_dispatch",
    )(*args)


def _expert_kernel(x_ref, w1_ref, w3_ref, w2_ref, gate_ref, y_ref, wb1, wb3, wb2, acc_ref, *, tm, nf):
    f = pl.program_id(1)
    m = pl.program_id(2)

    @pl.when(m == 0)
    def _():
        wb1[...] = w1_ref[0, 0].astype(BF16)
        wb3[...] = w3_ref[0, 0].astype(BF16)
        wb2[...] = w2_ref[0, 0].astype(BF16)

    tile = pl.ds(pl.multiple_of(m * tm, tm), tm)

    @pl.when(f == 0)
    def _():
        acc_ref[tile, :] = jnp.zeros((tm, acc_ref.shape[1]), F32)

    x = x_ref[0]
    a = _dot(x, wb1[...])
    u = _dot(x, wb3[...])
    acc_ref[tile, :] += _dot((_silu(a) * u).astype(BF16), wb2[...])

    @pl.when(f == nf - 1)
    def _():
        y_ref[0] = (acc_ref[tile, :] * gate_ref[0]).astype(BF16)


def _experts(xe, w1, w3, w2, layer, gate, tm, tf=512):
    E, M, D = xe.shape
    F = w1.shape[3]
    nf = F // tf
    out = pl.BlockSpec((1, tm, D), lambda e, f, m: (e, jnp.where(f == nf - 1, m, 0), 0))
    return pl.pallas_call(
        functools.partial(_expert_kernel, tm=tm, nf=nf),
        grid=(E, nf, M // tm),
        in_specs=[pl.BlockSpec((1, tm, D), lambda e, f, m: (e, m, 0)),
                  pl.BlockSpec((1, 1, D, tf), lambda e, f, m: (layer, e, 0, f)),
                  pl.BlockSpec((1, 1, D, tf), lambda e, f, m: (layer, e, 0, f)),
                  pl.BlockSpec((1, 1, tf, D), lambda e, f, m: (layer, e, f, 0)),
                  pl.BlockSpec((1, tm, 1), lambda e, f, m: (e, m, 0))],
        out_specs=out,
        out_shape=jax.ShapeDtypeStruct((E, M, D), BF16),
        scratch_shapes=[pltpu.VMEM((D, tf), BF16), pltpu.VMEM((D, tf), BF16), pltpu.VMEM((tf, D), BF16),
                        pltpu.VMEM((M, D), F32)],
        compiler_params=_cparams(("parallel", "arbitrary", "arbitrary")),
        name="expert_ffn",
    )(xe, w1, w3, w2, gate)


def _combine_kernel(s0_ref, slot_ref, x_ref, g2_ref, fin_ref, y_hbm, o_ref, stage, extra_stage, acc_ref,
                    sem, *, cap, win, nt, row_base, final_norm):
    E = N_EXPERTS
    b = pl.program_id(0)
    i = pl.program_id(1)
    step = b * nt + i
    total = pl.num_programs(0) * nt
    buf = step % 2

    def window_copies(src_row, dst, e, s):
        src_row = pl.multiple_of(src_row, SLOT_ALIGN)
        return (pltpu.make_async_copy(y_hbm.at[e, pl.ds(src_row, win)], dst, s),)

    def tile_copies(bb, ii, bf):
        cps = []
        for e in range(E):
            _, _, r0 = _window_start(s0_ref, bb, e, ii, nt, cap, win)
            cps.extend(window_copies(row_base + bb * cap + r0, stage.at[bf, pl.ds(e * win, win)], e, sem.at[bf]))
        return cps

    @pl.when(step == 0)
    def _():
        for cp in tile_copies(b, i, 0):
            cp.start()

    nxt = step + 1

    @pl.when(nxt < total)
    def _():
        for cp in tile_copies(nxt // nt, nxt % nt, 1 - buf):
            cp.start()

    for cp in tile_copies(b, i, buf):
        cp.wait()

    first, onehot_t, n_extra, jrow = _first_windows(s0_ref, slot_ref, b, i, nt, cap, win)
    acc_ref[...] = _dot_tn(onehot_t, stage[buf])

    @pl.when(n_extra > 0)
    def _():
        for e in range(E):
            a, end, _, _ = first[e]

            def extra(w, carry, e=e, a=a):
                r0w, pw = _slot_hits(slot_ref, e, a, w, cap, win, jrow)
                cps = window_copies(row_base + b * cap + r0w, extra_stage, e, sem.at[2])
                for cp in cps:
                    cp.start()
                for cp in cps:
                    cp.wait()
                acc_ref[...] += _dot_tn(jnp.where(pw, 1.0, 0.0).astype(BF16), extra_stage[...])
                return carry

            lax.fori_loop(1, _n_windows(a, end, cap, win), extra, 0)

    out = x_ref[0] + g2_ref[0] * acc_ref[...]
    if final_norm:
        ms = jnp.mean(out * out, axis=-1, keepdims=True)
        out = out * lax.rsqrt(ms + EPS) * fin_ref[...]
    o_ref[0] = out


def _combine(s0, slot_t, x, g2, y, cap, win, row_base, final_g=None):
    B, L, D = x.shape
    E = N_EXPERTS
    T = min(MOE_TILE, L)
    nt = L // T
    grid_spec = pltpu.PrefetchScalarGridSpec(
        num_scalar_prefetch=1,
        grid=(B, nt),
        in_specs=[pl.BlockSpec((1, E, T), lambda b, i, s: (b, 0, i)),
                  pl.BlockSpec((1, T, D), lambda b, i, s: (b, i, 0)),
                  pl.BlockSpec((1, 1, D), lambda b, i, s: (b, 0, 0)),
                  pl.BlockSpec((1, D), lambda b, i, s: (0, 0)),
                  pl.BlockSpec(memory_space=pl.ANY)],
        out_specs=pl.BlockSpec((1, T, D), lambda b, i, s: (b, i, 0)),
        scratch_shapes=[pltpu.VMEM((2, E * win, D), BF16), pltpu.VMEM((win, D), BF16),
                        pltpu.VMEM((T, D), F32), pltpu.SemaphoreType.DMA((3,))],
    )
    return pl.pallas_call(
        functools.partial(_combine_kernel, cap=cap, win=win, nt=nt, row_base=row_base,
                          final_norm=final_g is not None),
        grid_spec=grid_spec,
        out_shape=jax.ShapeDtypeStruct((B, L, D), F32),
        compiler_params=_cparams(("arbitrary", "arbitrary")),
        name="moe_combine",
    )(s0, slot_t, x, g2, (jnp.ones((D,), F32) if final_g is None else final_g).reshape(1, D), y)


def _expert_row_tile(m):
    for t in range(min(m, EXPERT_ROW_TILE_MAX) // SLOT_ALIGN * SLOT_ALIGN, 0, -SLOT_ALIGN):
        if m % t == 0:
            return t
    raise ValueError(f"no expert row tile for {m} rows")


def _moe(streams, w1, w3, w2, layer, final_g=None):
    E = N_EXPERTS
    D = streams[0][0].shape[2]
    plans, row_base = [], 0
    for x, g2, h2, aff_t in streams:
        B, L, _ = h2.shape
        cap = CAPACITY_FACTOR * L // E
        plans.append((cap, min(MOE_WIN, cap), L // min(MOE_TILE, L), row_base))
        row_base += B * cap
    rows_total = row_base
    routed, prev = [], None
    if len(streams) > 1:
        prev = (jnp.zeros((E, rows_total, D), BF16), jnp.zeros((E, rows_total, 1), F32))
    for (x, g2, h2, aff_t), (cap, win, nt, base) in zip(streams, plans):
        slot_t, s0_pad = _select(aff_t, cap)
        s0 = s0_pad[:, :, :nt + 1].reshape(-1)
        prev = _dispatch(s0, h2, slot_t, aff_t, cap, win, rows_total, base, prev)
        routed.append((s0, slot_t))
    xe, ge = prev
    y = _experts(xe, w1, w3, w2, layer, ge, tm=_expert_row_tile(rows_total))
    outs = []
    for k, ((x, g2, h2, aff_t), (cap, win, nt, base), (s0, slot_t)) in enumerate(zip(streams, plans, routed)):
        outs.append(_combine(s0, slot_t, x, g2, y, cap, win, base, final_g if k == 0 else None))
    return outs


def _rope_tables(n):
    half = HEAD_DIM // 2
    quarter = half // 2
    inv = ROPE_BASE ** (-jnp.arange(quarter, dtype=F32) / quarter)
    t = jnp.arange(n)
    row = (t // GRID_W).astype(F32)
    col = (t % GRID_W).astype(F32)
    lane = np.arange(128)
    within = lane % HEAD_DIM
    use_col = jnp.asarray((within >= half)[None, :])
    freq = inv[jnp.asarray(within % quarter)][None, :]
    ang = jnp.where(use_col, col[:, None], row[:, None]) * freq
    sign = jnp.asarray(np.where((lane % half) < quarter, -1.0, 1.0).astype(np.float32))[None, :]
    return jnp.cos(ang), jnp.sin(ang) * sign


def kernel(x, c, ctx, c_ctx, mod_w, mod_b, norm1_g, norm2_g, ab_w_in, ab_w_out, sgu_w, sgu_b, hgrn_lb_logits,
           hgrn_norm_g, attn_w_qkv, attn_w_out, attn_sink, router_w, expert_w1, expert_w3, expert_w2, final_g):
    B, N, D = x.shape
    Lc = ctx.shape[1]
    depth = mod_w.shape[0]

    rows = ((B + 1 + 7) // 8) * 8
    cvec = jnp.zeros((rows, D), F32).at[:B].set(c).at[B].set(c_ctx)
    mod = _mod_vectors(cvec, mod_w, mod_b)

    def mods(l):
        lat = [mod[l, :B, k * D:(k + 1) * D].reshape(B, 1, D) for k in range(6)]
        cx = [jnp.broadcast_to(mod[l, B, k * D:(k + 1) * D].reshape(1, 1, D), (B, 1, D)) for k in range(6)]
        return lat, cx

    lb_all = jnp.cumsum(jax.nn.softmax(hgrn_lb_logits.astype(F32), axis=0), axis=0)
    lmat_f = jnp.asarray(_hgrn_level_matrix(False), BF16)
    lmat_r = jnp.asarray(_hgrn_level_matrix(True), BF16)

    x_lat, x_ctx = x, ctx
    for l in range(depth):
        last = l == depth - 1
        (sh1, sc1, g1, sh2, sc2, g2), (csh1, csc1, cg1, csh2, csc2, cg2) = mods(l)
        wr_t = router_w[l].T
        if l % 2 == 0:
            e = l // 2
            w_in = ab_w_in[e].astype(BF16)
            w_out = ab_w_out[e].astype(BF16)
            sw = sgu_w[e].astype(BF16)
            sb_t = sgu_b[e].T
            hg = hgrn_norm_g[e].reshape(1, HGRN_DIM)
            s0 = jnp.zeros((2, B, HGRN_HEADS, HGRN_KDIM, HGRN_KDIM), F32)

            def even(xs, shift, scale, gate1, shift2, scale2, s_in):
                ya, z = _proj_sgu(xs, norm1_g[l], shift, scale, w_in, sw, sb_t)
                o_f, o_b, s_fin = _hgrn(z, lb_all[e], s_in, lmat_f, lmat_r)
                xn, h2, aff = _even_post(ya, o_f, o_b, z, hg, w_out, xs, gate1, norm2_g[l], shift2, scale2, wr_t)
                return xn, h2, aff, s_fin

            x_ctx, h2c, affc, s_ctx = even(x_ctx, csh1, csc1, cg1, csh2, csc2, s0)
            x_lat, h2l, affl, _ = even(x_lat, sh1, sc1, g1, sh2, sc2, s_ctx)
        else:
            if not last:
                raise NotImplementedError("context update after an attention layer")
            o_idx = l // 2
            w_qkv = attn_w_qkv[o_idx].astype(BF16)
            w_out = attn_w_out[o_idx].astype(BF16)
            cos, sin = _rope_tables(N)
            zc = _proj(x_ctx, norm1_g[l], csh1, csc1, w_qkv[:, ATTN_Q_DIM:])
            k_ctx = zc[..., :ATTN_KV_DIM].astype(BF16)
            v_ctx = zc[..., ATTN_KV_DIM:].astype(BF16)
            q, k, v = _qkv_lat(x_lat, norm1_g[l], sh1, sc1, w_qkv, cos, sin)
            o = _attention(q, k, v, k_ctx, v_ctx, attn_sink[o_idx])
            x_lat, h2l, affl = _odd_post(o, w_out, x_lat, g1, norm2_g[l], sh2, sc2, wr_t)
        streams = [(x_lat, g2, h2l, affl)]
        if not last:
            streams.append((x_ctx, cg2, h2c, affc))
        outs = _moe(streams, expert_w1, expert_w3, expert_w2, l, final_g if last else None)
        x_lat = outs[0]
        if not last:
            x_ctx = outs[1]
    return x_lat
```

```python
import functools

import numpy as np
import jax
import jax.numpy as jnp
from jax import lax
from jax.experimental import pallas as pl
from jax.experimental.pallas import tpu as pltpu

F32 = jnp.float32
BF16 = jnp.bfloat16

D_MODEL = 1024
GRID_W = 64
EPS = 1e-6
SGU_GROUPS = 4
SGU_GROUP_DIM = 128
SGU_DIM = SGU_GROUPS * SGU_GROUP_DIM
SGU_CHUNK = 128
HGRN_HEADS = 4
HGRN_KDIM = 128
HGRN_DIM = HGRN_HEADS * HGRN_KDIM
HGRN_CHUNK = 64
HGRN_LEVELS = 6
HGRN_COARSE = 3
HGRN_MXU_LEVELS = 5
HGRN_TBLK = 512
AB_IN = 2 * SGU_DIM + 5 * HGRN_DIM
ATTN_HEADS = 16
ATTN_KV_HEADS = 4
ATTN_GROUP = ATTN_HEADS // ATTN_KV_HEADS
HEAD_DIM = 64
ATTN_Q_DIM = ATTN_HEADS * HEAD_DIM
ATTN_KV_DIM = ATTN_KV_HEADS * HEAD_DIM
WINDOW = 128
ATTN_BLOCK = 128
ROPE_BASE = 10000.0
N_EXPERTS = 16
CAPACITY_FACTOR = 2
EXPERT_FF = 2048
NEG_BIG = -1e30
LOG2_E = 1.4426950408889634

VMEM_LIMIT_BYTES = 56 * 1024 * 1024


ROW_TILE = 512


def _row_tile(n_rows):
    return min(ROW_TILE, n_rows)


def _cparams(sem):
    return pltpu.CompilerParams(dimension_semantics=sem, vmem_limit_bytes=VMEM_LIMIT_BYTES)


def _dot(a, b):
    return jnp.dot(a, b, preferred_element_type=F32)


def _dot_nt(a, b):
    return lax.dot_general(a, b, (((1,), (1,)), ((), ())), preferred_element_type=F32)


def _dot_tn(a, b):
    return lax.dot_general(a, b, (((0,), (0,)), ((), ())), preferred_element_type=F32)


def _split2(a):
    hi = a.astype(BF16)
    lo = (a - hi.astype(F32)).astype(BF16)
    return hi, lo


def _dot_f32x3(a, b):
    a_hi, a_lo = _split2(a)
    b_hi, b_lo = _split2(b)
    return _dot(a_hi, b_hi) + (_dot(a_hi, b_lo) + _dot(a_lo, b_hi))


def _silu(x):
    return x * jax.nn.sigmoid(x)


def _gelu(x):
    return 0.5 * x * (1.0 + lax.erf(x * np.float32(1.0 / np.sqrt(2.0))))


def _rms_modulate(x, g, shift, scale):
    ms = jnp.mean(x * x, axis=-1, keepdims=True)
    return (x * lax.rsqrt(ms + EPS) * g) * (1.0 + scale) + shift


def _mod_kernel(c_ref, w_ref, b_ref, o_ref):
    s = _silu(c_ref[...])
    o_ref[0] = _dot_f32x3(s, w_ref[0]) + b_ref[0]


def _mod_vectors(cvec, mod_w, mod_b):
    depth, d, n6 = mod_w.shape
    rows = cvec.shape[0]
    tn = 1536
    return pl.pallas_call(
        _mod_kernel,
        grid=(depth, n6 // tn),
        in_specs=[
            pl.BlockSpec((rows, d), lambda l, j: (0, 0)),
            pl.BlockSpec((1, d, tn), lambda l, j: (l, 0, j)),
            pl.BlockSpec((1, 1, tn), lambda l, j: (l, 0, j)),
        ],
        out_specs=pl.BlockSpec((1, rows, tn), lambda l, j: (l, 0, j)),
        out_shape=jax.ShapeDtypeStruct((depth, rows, n6), F32),
        compiler_params=_cparams(("arbitrary", "arbitrary")),
        name="mod_vectors",
    )(cvec, mod_w, mod_b.reshape(depth, 1, n6))


def _proj_kernel(x_ref, g_ref, sh_ref, sc_ref, w_ref, z_ref):
    h = _rms_modulate(x_ref[0], g_ref[...], sh_ref[0], sc_ref[0])
    z_ref[0] = _dot(h.astype(BF16), w_ref[...])


def _row_spec(tm, d):
    return pl.BlockSpec((1, tm, d), lambda b, i: (b, i, 0))


def _vec_spec(d):
    return pl.BlockSpec((1, 1, d), lambda b, i: (b, 0, 0))


def _const_spec(shape):
    nd = len(shape)
    return pl.BlockSpec(shape, lambda b, i: (0,) * nd)


def _proj(x, g, shift, scale, w):
    B, L, D = x.shape
    tm = _row_tile(L)
    n = w.shape[1]
    return pl.pallas_call(
        _proj_kernel,
        grid=(B, L // tm),
        in_specs=[_row_spec(tm, D), _const_spec((1, D)), _vec_spec(D), _vec_spec(D), _const_spec((D, n))],
        out_specs=_row_spec(tm, n),
        out_shape=jax.ShapeDtypeStruct((B, L, n), F32),
        compiler_params=_cparams(("parallel", "parallel")),
        name="norm_mod_proj",
    )(x, g.reshape(1, D), shift, scale, w)


def _sgu_rows(zu, zv, w_ref, b_ref, o_ref, rows):
    u = _gelu(zu)
    v = _gelu(zv)
    for g in range(SGU_GROUPS):
        cols = slice(g * SGU_GROUP_DIM, (g + 1) * SGU_GROUP_DIM)
        vg = v[:, cols]
        mu = jnp.mean(vg, axis=-1, keepdims=True)
        vc = vg - mu
        var = jnp.mean(vc * vc, axis=-1, keepdims=True)
        vn = (vc * lax.rsqrt(var + EPS)).astype(BF16)
        s = _dot(w_ref[g], vn) + b_ref[:, g:g + 1]
        o_ref[0, rows, cols] = (u[:, cols] * s).astype(BF16)


def _proj_sgu_kernel(x_ref, g_ref, sh_ref, sc_ref, w_ref, sw_ref, sb_ref, ya_ref, zr_ref, *, tm):
    h = _rms_modulate(x_ref[0], g_ref[...], sh_ref[0], sc_ref[0]).astype(BF16)
    z_uv = _dot(h, w_ref[:, :2 * SGU_DIM])
    zr_ref[0] = _dot(h, w_ref[:, 2 * SGU_DIM:])
    for ch in range(tm // SGU_CHUNK):
        rows = slice(ch * SGU_CHUNK, (ch + 1) * SGU_CHUNK)
        _sgu_rows(z_uv[rows, :SGU_DIM], z_uv[rows, SGU_DIM:], sw_ref, sb_ref, ya_ref, rows)


def _proj_sgu(x, g, shift, scale, w, sgu_w_bf16, sgu_b_t):
    B, L, D = x.shape
    tm = _row_tile(L)
    n_rest = w.shape[1] - 2 * SGU_DIM
    return pl.pallas_call(
        functools.partial(_proj_sgu_kernel, tm=tm),
        grid=(B, L // tm),
        in_specs=[_row_spec(tm, D), _const_spec((1, D)), _vec_spec(D), _vec_spec(D), _const_spec((D, w.shape[1])),
                  _const_spec((SGU_GROUPS, SGU_CHUNK, SGU_CHUNK)), _const_spec((SGU_CHUNK, SGU_GROUPS))],
        out_specs=[_row_spec(tm, SGU_DIM), _row_spec(tm, n_rest)],
        out_shape=[jax.ShapeDtypeStruct((B, L, SGU_DIM), BF16), jax.ShapeDtypeStruct((B, L, n_rest), F32)],
        compiler_params=_cparams(("parallel", "parallel")),
        name="norm_mod_proj_sgu",
    )(x, g.reshape(1, D), shift, scale, w, sgu_w_bf16, sgu_b_t)


def _hgrn_level_matrix(reverse):
    C = HGRN_CHUNK
    t = np.arange(C)[:, None]
    s = np.arange(C)[None, :]
    blocks = []
    blocks.append(((s <= t) if not reverse else (s >= t)).astype(np.float32))
    for l in range(HGRN_COARSE, HGRN_MXU_LEVELS):
        m = C >> (l + 1)
        base = (t // (2 * m)) * (2 * m)
        if not reverse:
            r = base + m - 1
            qrow = (t % (2 * m)) >= m
            mq = (s > r) & (s <= t)
            mk = (s > t) & (s <= r)
        else:
            r = base + m
            qrow = (t % (2 * m)) < m
            mq = (s >= t) & (s < r)
            mk = (s >= r) & (s < t)
        blocks.append(np.where(qrow, mq, mk).astype(np.float32))
    return np.concatenate(blocks, axis=0)


def _hgrn_masks(reverse):
    C = HGRN_CHUNK
    t = lax.broadcasted_iota(jnp.int32, (C, C), 0)
    s = lax.broadcasted_iota(jnp.int32, (C, C), 1)
    r = lax.broadcasted_iota(jnp.int32, (C, 1), 0)
    masks, qrows = [], []
    for l in range(HGRN_LEVELS):
        m = C >> (l + 1)
        same = (t // (2 * m)) == (s // (2 * m))
        if not reverse:
            mk = same & ((t % (2 * m)) >= m) & ((s % (2 * m)) < m)
            qr = (r % (2 * m)) >= m
        else:
            mk = same & ((t % (2 * m)) < m) & ((s % (2 * m)) >= m)
            qr = (r % (2 * m)) < m
        masks.append(mk)
        qrows.append(qr)
    return masks, qrows


def _hgrn_chunk(zq, zf, zi, lb, st, lmat, masks, qrows, reverse):
    C = HGRN_CHUNK
    f = lb + (1.0 - lb) * jax.nn.sigmoid(zf)
    g = jnp.log(f)
    kk = 1.0 - f
    q = _silu(zq)
    vb = zi.astype(BF16)
    g_hi, g_lo = _split2(g)
    d2 = _dot(lmat, jnp.concatenate([g_hi, g_lo], axis=1))
    d = d2[:, :HGRN_KDIM] + d2[:, HGRN_KDIM:]
    b = d[0:C]
    b_end = b[0:1] if reverse else b[C - 1:C]
    dec = jnp.exp(b_end)
    qb = (q * jnp.exp(b)).astype(BF16)
    kb = (kk * jnp.exp(b_end - b)).astype(BF16)
    o = _dot_nt(qb, st.astype(BF16))
    o = o + jnp.sum(q * kk, axis=1, keepdims=True) * zi
    shift = C - 1 if reverse else 1
    near = jnp.sum(q * f * pltpu.roll(kk, shift, 0), axis=1, keepdims=True)
    o = o + jnp.where(qrows[HGRN_LEVELS - 1], near, 0.0) * pltpu.roll(zi, shift, 0)
    a = None
    for l in range(HGRN_MXU_LEVELS):
        if l < HGRN_COARSE:
            m = C >> (l + 1)
            r = m if reverse else m - 1
            ref = jnp.broadcast_to(b.reshape(C // (2 * m), 2 * m, HGRN_KDIM)[:, r:r + 1, :],
                                   (C // (2 * m), 2 * m, HGRN_KDIM)).reshape(C, HGRN_KDIM)
            dl = jnp.where(qrows[l], b - ref, ref - b)
        else:
            dl = d[(1 + l - HGRN_COARSE) * C:(2 + l - HGRN_COARSE) * C]
        mix = (jnp.where(qrows[l], q, kk) * jnp.exp(dl)).astype(BF16)
        p = jnp.where(masks[l], _dot_nt(mix, mix), 0.0)
        a = p if a is None else a + p
    o = o + _dot(a.astype(BF16), vb)
    st_new = st * dec + _dot_tn(vb, kb)
    return o, st_new


def _hgrn_kernel(zqf_ref, zff_ref, zif_ref, zqb_ref, zfb_ref, zib_ref, lb_ref, s0_ref, lf_ref, lr_ref,
                 of_ref, ob_ref, sfin_ref, st_ref, *, tb):
    i = pl.program_id(1)
    nb = pl.num_programs(1)

    @pl.when(i == 0)
    def _():
        st_ref[...] = s0_ref[:, 0]

    C = HGRN_CHUNK
    nch = tb // C
    for d in range(2):
        reverse = d == 1
        masks, qrows = _hgrn_masks(reverse)
        lmat = (lr_ref if reverse else lf_ref)[...]
        zq_ref, zf_ref, zi_ref, o_ref = ((zqb_ref, zfb_ref, zib_ref, ob_ref) if reverse
                                         else (zqf_ref, zff_ref, zif_ref, of_ref))
        for h in range(HGRN_HEADS):
            cols = slice(h * HGRN_KDIM, (h + 1) * HGRN_KDIM)
            lb = lb_ref[d:d + 1, cols]
            st = st_ref[d, h]
            order = range(nch - 1, -1, -1) if reverse else range(nch)
            for c in order:
                rows = slice(c * C, (c + 1) * C)
                o, st = _hgrn_chunk(zq_ref[0, rows, cols], zf_ref[0, rows, cols], zi_ref[0, rows, cols],
                                    lb, st, lmat, masks, qrows, reverse)
                o_ref[0, rows, cols] = o.astype(BF16)
            st_ref[d, h] = st

    @pl.when(i == nb - 1)
    def _():
        sfin_ref[:, 0] = st_ref[...]


def _hgrn(z, lb, s0, lmat_f, lmat_r):
    B, L, _ = z.shape
    tb = min(HGRN_TBLK, L)
    nb = L // tb
    nl = lmat_f.shape[0]

    def zspec(col, rev):
        if rev:
            return pl.BlockSpec((1, tb, HGRN_DIM), lambda b, i: (b, nb - 1 - i, col))
        return pl.BlockSpec((1, tb, HGRN_DIM), lambda b, i: (b, i, col))

    st_spec = pl.BlockSpec((2, 1, HGRN_HEADS, HGRN_KDIM, HGRN_KDIM), lambda b, i: (0, b, 0, 0, 0))
    return pl.pallas_call(
        functools.partial(_hgrn_kernel, tb=tb),
        grid=(B, nb),
        in_specs=[zspec(0, False), zspec(1, False), zspec(3, False),
                  zspec(0, True), zspec(2, True), zspec(3, True),
                  _const_spec((2, HGRN_DIM)), st_spec,
                  _const_spec((nl, HGRN_CHUNK)), _const_spec((nl, HGRN_CHUNK))],
        out_specs=[pl.BlockSpec((1, tb, HGRN_DIM), lambda b, i: (b, i, 0)),
                   pl.BlockSpec((1, tb, HGRN_DIM), lambda b, i: (b, nb - 1 - i, 0)),
                   st_spec],
        out_shape=[jax.ShapeDtypeStruct((B, L, HGRN_DIM), BF16),
                   jax.ShapeDtypeStruct((B, L, HGRN_DIM), BF16),
                   jax.ShapeDtypeStruct(s0.shape, F32)],
        scratch_shapes=[pltpu.VMEM((2, HGRN_HEADS, HGRN_KDIM, HGRN_KDIM), F32)],
        compiler_params=_cparams(("parallel", "arbitrary")),
        name="hgrn2_scan",
    )(z, z, z, z, z, z, lb, s0, lmat_f, lmat_r)


def _residual_router_tail(x, y, g1, n2g, sh2, sc2, wr, xo_ref, h2_ref, aff_ref):
    xn = x + g1 * y
    xo_ref[0] = xn
    h2 = _rms_modulate(xn, n2g, sh2, sc2)
    h2_ref[0] = h2.astype(BF16)
    w_hi, w_lo = _split2(wr)
    h_hi, h_lo = _split2(h2)
    logits = _dot_nt(w_hi, h_hi) + (_dot_nt(w_hi, h_lo) + _dot_nt(w_lo, h_hi))
    mx = jnp.max(logits, axis=0, keepdims=True)
    p = jnp.exp(logits - mx)
    aff_ref[0] = p / jnp.sum(p, axis=0, keepdims=True)


def _even_post_kernel(ya_ref, of_ref, ob_ref, zg_ref, hg_ref, w_ref, x_ref, g1_ref, n2g_ref, sh2_ref,
                      sc2_ref, wr_ref, xo_ref, h2_ref, aff_ref):
    o = of_ref[0].astype(F32) + ob_ref[0].astype(F32)
    zg = zg_ref[0]
    parts = []
    for h in range(HGRN_HEADS):
        cols = slice(h * HGRN_KDIM, (h + 1) * HGRN_KDIM)
        oh = o[:, cols]
        ms = jnp.mean(oh * oh, axis=-1, keepdims=True)
        parts.append(oh * lax.rsqrt(ms + EPS) * hg_ref[:, cols])
    yb = (jnp.concatenate(parts, axis=1) * _silu(zg)).astype(BF16)
    y = _dot(ya_ref[0], w_ref[:SGU_DIM, :]) + _dot(yb, w_ref[SGU_DIM:, :])
    _residual_router_tail(x_ref[0], y, g1_ref[0], n2g_ref[...], sh2_ref[0], sc2_ref[0], wr_ref[...],
                          xo_ref, h2_ref, aff_ref)


def _odd_post_kernel(o_ref, w_ref, x_ref, g1_ref, n2g_ref, sh2_ref, sc2_ref, wr_ref, xo_ref, h2_ref,
                     aff_ref):
    y = _dot(o_ref[0], w_ref[...])
    _residual_router_tail(x_ref[0], y, g1_ref[0], n2g_ref[...], sh2_ref[0], sc2_ref[0], wr_ref[...],
                          xo_ref, h2_ref, aff_ref)


def _post_out_specs(B, L, tm):
    D = D_MODEL
    specs = [_row_spec(tm, D), _row_spec(tm, D), pl.BlockSpec((1, N_EXPERTS, tm), lambda b, i: (b, 0, i))]
    shapes = [jax.ShapeDtypeStruct((B, L, D), F32), jax.ShapeDtypeStruct((B, L, D), BF16),
              jax.ShapeDtypeStruct((B, N_EXPERTS, L), F32)]
    return specs, shapes


def _even_post(ya, o_f, o_b, z, hg, w_out, x, g1, n2g, sh2, sc2, wr):
    B, L, D = x.shape
    tm = _row_tile(L)
    specs, shapes = _post_out_specs(B, L, tm)
    return pl.pallas_call(
        _even_post_kernel,
        grid=(B, L // tm),
        in_specs=[_row_spec(tm, SGU_DIM), _row_spec(tm, HGRN_DIM), _row_spec(tm, HGRN_DIM),
                  pl.BlockSpec((1, tm, HGRN_DIM), lambda b, i: (b, i, 4)),
                  _const_spec((1, HGRN_DIM)), _const_spec((D, D)), _row_spec(tm, D), _vec_spec(D),
                  _const_spec((1, D)), _vec_spec(D), _vec_spec(D), _const_spec((N_EXPERTS, D))],
        out_specs=specs,
        out_shape=shapes,
        compiler_params=_cparams(("parallel", "parallel")),
        name="even_post_router",
    )(ya, o_f, o_b, z, hg, w_out, x, g1, n2g.reshape(1, D), sh2, sc2, wr)


def _odd_post(o, w_out, x, g1, n2g, sh2, sc2, wr):
    B, L, D = x.shape
    tm = _row_tile(L)
    specs, shapes = _post_out_specs(B, L, tm)
    return pl.pallas_call(
        _odd_post_kernel,
        grid=(B, L // tm),
        in_specs=[_row_spec(tm, ATTN_Q_DIM), _const_spec((ATTN_Q_DIM, D)), _row_spec(tm, D), _vec_spec(D),
                  _const_spec((1, D)), _vec_spec(D), _vec_spec(D), _const_spec((N_EXPERTS, D))],
        out_specs=specs,
        out_shape=shapes,
        compiler_params=_cparams(("parallel", "parallel")),
        name="odd_post_router",
    )(o, w_out, x, g1, n2g.reshape(1, D), sh2, sc2, wr)


def _rope(x, cos, sin_signed):
    lane = lax.broadcasted_iota(jnp.int32, (x.shape[0], 128), 1)
    first = (lane % 32) < 16
    outs = []
    for j in range(x.shape[1] // 128):
        xb = x[:, j * 128:(j + 1) * 128]
        sw = jnp.where(first, pltpu.roll(xb, 112, 1), pltpu.roll(xb, 16, 1))
        outs.append(xb * cos + sw * sin_signed)
    return jnp.concatenate(outs, axis=1)


def _qkv_lat_kernel(x_ref, g_ref, sh_ref, sc_ref, w_ref, cos_ref, sin_ref, q_ref, k_ref, v_ref):
    h = _rms_modulate(x_ref[0], g_ref[...], sh_ref[0], sc_ref[0])
    z = _dot(h.astype(BF16), w_ref[...])
    cos = cos_ref[...]
    sin = sin_ref[...]
    scale = np.float32(HEAD_DIM ** -0.5 * LOG2_E)
    q_ref[0] = (_rope(z[:, :ATTN_Q_DIM], cos, sin) * scale).astype(BF16)
    k_ref[0] = _rope(z[:, ATTN_Q_DIM:ATTN_Q_DIM + ATTN_KV_DIM], cos, sin).astype(BF16)
    v_ref[0] = z[:, ATTN_Q_DIM + ATTN_KV_DIM:].astype(BF16)


def _qkv_lat(x, g, shift, scale, w, cos, sin):
    B, L, D = x.shape
    tm = _row_tile(L)
    n = w.shape[1]
    tab = pl.BlockSpec((tm, 128), lambda b, i: (i, 0))
    return pl.pallas_call(
        _qkv_lat_kernel,
        grid=(B, L // tm),
        in_specs=[_row_spec(tm, D), _const_spec((1, D)), _vec_spec(D), _vec_spec(D), _const_spec((D, n)), tab, tab],
        out_specs=[_row_spec(tm, ATTN_Q_DIM), _row_spec(tm, ATTN_KV_DIM), _row_spec(tm, ATTN_KV_DIM)],
        out_shape=[jax.ShapeDtypeStruct((B, L, ATTN_Q_DIM), BF16),
                   jax.ShapeDtypeStruct((B, L, ATTN_KV_DIM), BF16),
                   jax.ShapeDtypeStruct((B, L, ATTN_KV_DIM), BF16)],
        compiler_params=_cparams(("parallel", "parallel")),
        name="norm_qkv_rope",
    )(x, g.reshape(1, D), shift, scale, w, cos, sin)


def _attn_kernel(q_ref, kp_ref, kc_ref, kn_ref, vp_ref, vc_ref, vn_ref, kx_ref, vx_ref, sink_ref, o_ref,
                 *, n_tokens):
    bi = pl.program_id(1)
    blk = ATTN_BLOCK
    n_loc = 3 * blk
    nk = n_loc + kx_ref.shape[1]
    n_pairs = ATTN_GROUP // 2
    nq = n_pairs * blk
    pad = 16
    s_idx = lax.broadcasted_iota(jnp.int32, (n_loc, nq), 0)
    r_idx = lax.broadcasted_iota(jnp.int32, (n_loc, nq), 1) % blk
    j = bi * blk - blk + s_idx
    valid = (jnp.abs(s_idx - r_idx - blk) <= WINDOW) & (j >= 0) & (j < n_tokens)
    lane = lax.broadcasted_iota(jnp.int32, (nk, 128), 1)
    qcol = lax.broadcasted_iota(jnp.int32, (1, nq), 1)
    prow = lax.broadcasted_iota(jnp.int32, (pad, nq), 0)
    er = lax.broadcasted_iota(jnp.int32, (2 * nk + pad, 128), 0)
    el = lax.broadcasted_iota(jnp.int32, (2 * nk + pad, 128), 1)
    lo_rows = (er < nk) | (er == 2 * nk)
    hi_rows = ((er >= nk) & (er < 2 * nk)) | (er == 2 * nk + 1)
    ind = jnp.where((lo_rows & (el < HEAD_DIM)) | (hi_rows & (el >= HEAD_DIM)), 1.0, 0.0).astype(BF16)
    zpad = jnp.zeros((pad, 128), BF16)
    scores, values = [], []
    for kvh in range(ATTN_KV_HEADS):
        cols = slice((kvh // 2) * 128, (kvh // 2 + 1) * 128)
        own = (lane < HEAD_DIM) if kvh % 2 == 0 else (lane >= HEAD_DIM)
        k_all = jnp.concatenate([kp_ref[0, :, cols], kc_ref[0, :, cols], kn_ref[0, :, cols], kx_ref[0, :, cols]],
                                axis=0)
        v_all = jnp.concatenate([vp_ref[0, :, cols], vc_ref[0, :, cols], vn_ref[0, :, cols], vx_ref[0, :, cols]],
                                axis=0)
        k_own = jnp.where(own, k_all, jnp.zeros_like(k_all))
        v_own = jnp.where(own, v_all, jnp.zeros_like(v_all))
        k_oth = pltpu.roll(k_own, HEAD_DIM, 1)
        v_oth = pltpu.roll(v_own, HEAD_DIM, 1)
        if kvh % 2 == 0:
            k_cat = jnp.concatenate([k_own, k_oth], axis=0)
            v_cat = jnp.concatenate([v_own, v_oth, zpad], axis=0)
        else:
            k_cat = jnp.concatenate([k_oth, k_own], axis=0)
            v_cat = jnp.concatenate([v_oth, v_own, zpad], axis=0)
        q2 = jnp.concatenate([q_ref[0, :, (kvh * n_pairs + p) * 128:(kvh * n_pairs + p + 1) * 128]
                              for p in range(n_pairs)], axis=0)
        scores.append(_dot_nt(k_cat, q2))
        values.append(jnp.concatenate([v_cat, ind], axis=1))
    probs = []
    for kvh in range(ATTN_KV_HEADS):
        st = scores[kvh]
        ps, es = [], []
        for sub in range(2):
            sh = st[sub * nk:(sub + 1) * nk]
            sh = jnp.concatenate([jnp.where(valid, sh[:n_loc], NEG_BIG), sh[n_loc:]], axis=0)
            hd0 = kvh * ATTN_GROUP + sub
            sink = sink_ref[:, hd0:hd0 + 1]
            for p in range(1, n_pairs):
                sink = jnp.where(qcol < p * blk, sink, sink_ref[:, hd0 + 2 * p:hd0 + 2 * p + 1])
            m = jnp.maximum(jnp.max(sh, axis=0, keepdims=True), sink)
            ps.append(jnp.exp2(sh - m).astype(BF16))
            es.append(jnp.exp2(sink - m))
        sink_rows = jnp.where(prow == 0, es[0], jnp.where(prow == 1, es[1], 0.0)).astype(BF16)
        probs.append(jnp.concatenate(ps + [sink_rows], axis=0))
    for kvh in range(ATTN_KV_HEADS):
        od = _dot_tn(probs[kvh], values[kvh])
        o = (od[:, :128] / od[:, 128:]).astype(BF16)
        for p in range(n_pairs):
            o_ref[0, :, (kvh * n_pairs + p) * 128:(kvh * n_pairs + p + 1) * 128] = o[p * blk:(p + 1) * blk]


def _attention(q, k, v, k_ctx, v_ctx, sink):
    B, N, _ = q.shape
    Lc = k_ctx.shape[1]
    nb = N // ATTN_BLOCK
    blk = ATTN_BLOCK

    def kv_spec(off):
        return pl.BlockSpec((1, blk, ATTN_KV_DIM),
                            lambda b, i: (b, jnp.clip(i + off, 0, nb - 1), 0))

    ctx_spec = pl.BlockSpec((1, Lc, ATTN_KV_DIM), lambda b, i: (b, 0, 0))
    return pl.pallas_call(
        functools.partial(_attn_kernel, n_tokens=N),
        grid=(B, nb),
        in_specs=[_row_spec(blk, ATTN_Q_DIM), kv_spec(-1), kv_spec(0), kv_spec(1),
                  kv_spec(-1), kv_spec(0), kv_spec(1), ctx_spec, ctx_spec,
                  _const_spec((1, ATTN_HEADS))],
        out_specs=_row_spec(blk, ATTN_Q_DIM),
        out_shape=jax.ShapeDtypeStruct((B, N, ATTN_Q_DIM), BF16),
        compiler_params=_cparams(("parallel", "parallel")),
        name="window_attention",
    )(q, k, k, k, v, v, v, k_ctx, v_ctx, (sink * np.float32(LOG2_E)).reshape(1, ATTN_HEADS))


MOE_TILE = 256
MOE_WIN = 64
SLOT_ALIGN = 16
EXPERT_ROW_TILE_MAX = 1152


def _select_kernel(aff_ref, slot_ref, s0_ref, *, cap):
    x = aff_ref[0]
    E, L = x.shape
    T = min(MOE_TILE, L)

    def count(mask):
        return jnp.sum(jnp.where(mask, 1.0, 0.0), axis=1, keepdims=True)

    def as_f32(bits):
        return lax.bitcast_convert_type(bits, F32)

    def body(_, c):
        lo, hi = c
        mid = lo + lax.shift_right_logical(hi - lo, 1)
        ok = count(x >= as_f32(mid)) >= cap
        return jnp.where(ok, mid, lo), jnp.where(ok, hi, mid)

    lo0 = jnp.zeros((E, 1), jnp.int32)
    hi0 = jnp.full((E, 1), 0x7F800000, jnp.int32)
    lo, hi = lax.fori_loop(0, 31, body, (lo0, hi0))
    gt = x >= as_f32(hi)
    eq = (x >= as_f32(lo)) & jnp.logical_not(gt)
    need = cap - count(gt)
    ti = lax.broadcasted_iota(jnp.int32, (T, T), 0)
    tj = lax.broadcasted_iota(jnp.int32, (T, T), 1)
    upper = jnp.where(ti < tj, 1.0, 0.0).astype(BF16)

    def excl_cumsum(mask):
        carry = jnp.zeros((E, 1), F32)
        outs, starts = [], []
        for t in range(L // T):
            v = jnp.where(mask[:, t * T:(t + 1) * T], 1.0, 0.0)
            starts.append(carry)
            outs.append(_dot(v.astype(BF16), upper) + carry)
            carry = carry + jnp.sum(v, axis=1, keepdims=True)
        starts.append(carry)
        return jnp.concatenate(outs, axis=1), starts

    eq_rank, _ = excl_cumsum(eq)
    sel = gt | (eq & (eq_rank < need))
    slot, starts = excl_cumsum(sel)
    slot_ref[0] = jnp.where(sel, slot, -1.0).astype(jnp.int32)
    lane = lax.broadcasted_iota(jnp.int32, (E, 128), 1)
    s0 = jnp.zeros((E, 128), F32)
    for t, st in enumerate(starts):
        s0 = jnp.where(lane == t, st, s0)
    s0_ref[0] = s0.astype(jnp.int32)


def _select(aff_t, cap):
    B, E, L = aff_t.shape
    assert L // min(MOE_TILE, L) + 1 <= 128
    return pl.pallas_call(
        functools.partial(_select_kernel, cap=cap),
        grid=(B,),
        in_specs=[pl.BlockSpec((1, E, L), lambda b: (b, 0, 0))],
        out_specs=[pl.BlockSpec((1, E, L), lambda b: (b, 0, 0)), pl.BlockSpec((1, E, 128), lambda b: (b, 0, 0))],
        out_shape=[jax.ShapeDtypeStruct((B, E, L), jnp.int32), jax.ShapeDtypeStruct((B, E, 128), jnp.int32)],
        compiler_params=_cparams(("parallel",)),
        name="moe_select",
    )(aff_t)


def _window_start(s0_ref, b, e, i, nt, cap, win):
    base = (b * N_EXPERTS + e) * (nt + 1) + i
    a = (s0_ref[base] // SLOT_ALIGN) * SLOT_ALIGN
    return a, s0_ref[base + 1], jnp.minimum(a, cap - win)


def _n_windows(a, end, cap, win):
    return jnp.where(a + win >= cap, 1, jnp.maximum((end - a + win - 1) // win, 1))


def _slot_hits(slot_ref, e, a, w, cap, win, jrow):
    lo_s = a + w * win
    r0 = jnp.minimum(lo_s, cap - win)
    se = slot_ref[0, e:e + 1, :]
    return r0, ((se - r0) == jrow) & (se >= lo_s)


def _first_windows(s0_ref, slot_ref, b, i, nt, cap, win):
    T = slot_ref.shape[2]
    jrow = lax.broadcasted_iota(jnp.int32, (win, T), 0)
    first = []
    n_extra = 0
    for e in range(N_EXPERTS):
        a, end, _ = _window_start(s0_ref, b, e, i, nt, cap, win)
        r0, p = _slot_hits(slot_ref, e, a, 0, cap, win, jrow)
        first.append((a, end, r0, p))
        n_extra = n_extra + (_n_windows(a, end, cap, win) - 1)
    onehot_t = jnp.where(jnp.concatenate([f[3] for f in first], axis=0), 1.0, 0.0).astype(BF16)
    return first, onehot_t, n_extra, jrow


def _dispatch_kernel(s0_ref, h_ref, slot_ref, aff_ref, *rest, cap, win, nt, row_off):
    xe_ref, ge_ref = rest[-2:]
    b = pl.program_id(0)
    i = pl.program_id(1)

    @pl.when(i == 0)
    def _():
        xe_ref[...] = jnp.zeros_like(xe_ref)
        ge_ref[...] = jnp.zeros_like(ge_ref)

    h = h_ref[0]

    def place(e, r0, p, rows):
        r0 = pl.multiple_of(row_off + r0, SLOT_ALIGN)
        gate = jnp.sum(jnp.where(p, aff_ref[0, e:e + 1, :], 0.0), axis=1, keepdims=True)
        xe_ref[e, pl.ds(r0, win), :] += rows.astype(BF16)
        ge_ref[e, pl.ds(r0, win), :] += gate

    first, onehot_t, n_extra, jrow = _first_windows(s0_ref, slot_ref, b, i, nt, cap, win)
    rows = _dot(onehot_t, h)
    for e in range(N_EXPERTS):
        _, _, r0, p = first[e]
        place(e, r0, p, rows[e * win:(e + 1) * win])

    @pl.when(n_extra > 0)
    def _():
        for e in range(N_EXPERTS):
            a, end, _, _ = first[e]

            def extra(w, carry, e=e, a=a):
                r0w, pw = _slot_hits(slot_ref, e, a, w, cap, win, jrow)
                place(e, r0w, pw, _dot(jnp.where(pw, 1.0, 0.0).astype(BF16), h))
                return carry

            lax.fori_loop(1, _n_windows(a, end, cap, win), extra, 0)


def _dispatch(s0, h2, slot_t, aff_t, cap, win, stride, off, prev=None):
    B, L, D = h2.shape
    E = N_EXPERTS
    T = min(MOE_TILE, L)
    nt = L // T
    if prev is None:
        blk_rows, row_off, per_sample, blk0 = stride, off, 1, 0
    else:
        assert stride % cap == 0 and off % cap == 0
        blk_rows, row_off, per_sample, blk0 = cap, 0, stride // cap, off // cap
    in_specs = [pl.BlockSpec((1, T, D), lambda b, i, s: (b, i, 0)),
                pl.BlockSpec((1, E, T), lambda b, i, s: (b, 0, i)),
                pl.BlockSpec((1, E, T), lambda b, i, s: (b, 0, i))]
    args = [s0, h2, slot_t, aff_t]
    aliases = {}
    if prev is not None:
        in_specs += [pl.BlockSpec(memory_space=pl.ANY), pl.BlockSpec(memory_space=pl.ANY)]
        aliases = {len(args): 0, len(args) + 1: 1}
        args += list(prev)
    grid_spec = pltpu.PrefetchScalarGridSpec(
        num_scalar_prefetch=1,
        grid=(B, nt),
        in_specs=in_specs,
        out_specs=[pl.BlockSpec((E, blk_rows, D), lambda b, i, s: (0, blk0 + b * per_sample, 0)),
                   pl.BlockSpec((E, blk_rows, 1), lambda b, i, s: (0, blk0 + b * per_sample, 0))],
    )
    return pl.pallas_call(
        functools.partial(_dispatch_kernel, cap=cap, win=win, nt=nt, row_off=row_off),
        grid_spec=grid_spec,
        out_shape=[jax.ShapeDtypeStruct((E, B * stride, D), BF16), jax.ShapeDtypeStruct((E, B * stride, 1), F32)],
        input_output_aliases=aliases,
        compiler_params=_cparams(("parallel", "arbitrary")),
        name="moe_dispatch",
    )(*args)


def _expert_kernel(x_ref, w1_ref, w3_ref, w2_ref, gate_ref, y_ref, wb1, wb3, wb2, acc_ref, *, tm, nf):
    f = pl.program_id(1)
    m = pl.program_id(2)

    @pl.when(m == 0)
    def _():
        wb1[...] = w1_ref[0, 0].astype(BF16)
        wb3[...] = w3_ref[0, 0].astype(BF16)
        wb2[...] = w2_ref[0, 0].astype(BF16)

    tile = pl.ds(pl.multiple_of(m * tm, tm), tm)

    @pl.when(f == 0)
    def _():
        acc_ref[tile, :] = jnp.zeros((tm, acc_ref.shape[1]), F32)

    x = x_ref[0]
    a = _dot(x, wb1[...])
    u = _dot(x, wb3[...])
    acc_ref[tile, :] += _dot((_silu(a) * u).astype(BF16), wb2[...])

    @pl.when(f == nf - 1)
    def _():
        y_ref[0] = (acc_ref[tile, :] * gate_ref[0]).astype(BF16)


def _experts(xe, w1, w3, w2, layer, gate, tm, tf=512):
    E, M, D = xe.shape
    F = w1.shape[3]
    nf = F // tf
    out = pl.BlockSpec((1, tm, D), lambda e, f, m: (e, jnp.where(f == nf - 1, m, 0), 0))
    return pl.pallas_call(
        functools.partial(_expert_kernel, tm=tm, nf=nf),
        grid=(E, nf, M // tm),
        in_specs=[pl.BlockSpec((1, tm, D), lambda e, f, m: (e, m, 0)),
                  pl.BlockSpec((1, 1, D, tf), lambda e, f, m: (layer, e, 0, f)),
                  pl.BlockSpec((1, 1, D, tf), lambda e, f, m: (layer, e, 0, f)),
                  pl.BlockSpec((1, 1, tf, D), lambda e, f, m: (layer, e, f, 0)),
                  pl.BlockSpec((1, tm, 1), lambda e, f, m: (e, m, 0))],
        out_specs=out,
        out_shape=jax.ShapeDtypeStruct((E, M, D), BF16),
        scratch_shapes=[pltpu.VMEM((D, tf), BF16), pltpu.VMEM((D, tf), BF16), pltpu.VMEM((tf, D), BF16),
                        pltpu.VMEM((M, D), F32)],
        compiler_params=_cparams(("parallel", "arbitrary", "arbitrary")),
        name="expert_ffn",
    )(xe, w1, w3, w2, gate)


def _combine_kernel(s0_ref, slot_ref, x_ref, g2_ref, fin_ref, y_hbm, o_ref, stage, extra_stage, acc_ref,
                    sem, *, cap, win, nt, stride, off, final_norm):
    E = N_EXPERTS
    b = pl.program_id(0)
    i = pl.program_id(1)
    step = b * nt + i
    total = pl.num_programs(0) * nt
    buf = step % 2

    def window_copies(src_row, dst, e, s):
        src_row = pl.multiple_of(src_row, SLOT_ALIGN)
        return (pltpu.make_async_copy(y_hbm.at[e, pl.ds(src_row, win)], dst, s),)

    def tile_copies(bb, ii, bf):
        cps = []
        for e in range(E):
            _, _, r0 = _window_start(s0_ref, bb, e, ii, nt, cap, win)
            cps.extend(window_copies(bb * stride + off + r0, stage.at[bf, pl.ds(e * win, win)], e, sem.at[bf]))
        return cps

    @pl.when(step == 0)
    def _():
        for cp in tile_copies(b, i, 0):
            cp.start()

    nxt = step + 1

    @pl.when(nxt < total)
    def _():
        for cp in tile_copies(nxt // nt, nxt % nt, 1 - buf):
            cp.start()

    for cp in tile_copies(b, i, buf):
        cp.wait()

    first, onehot_t, n_extra, jrow = _first_windows(s0_ref, slot_ref, b, i, nt, cap, win)
    acc_ref[...] = _dot_tn(onehot_t, stage[buf])

    @pl.when(n_extra > 0)
    def _():
        for e in range(E):
            a, end, _, _ = first[e]

            def extra(w, carry, e=e, a=a):
                r0w, pw = _slot_hits(slot_ref, e, a, w, cap, win, jrow)
                cps = window_copies(b * stride + off + r0w, extra_stage, e, sem.at[2])
                for cp in cps:
                    cp.start()
                for cp in cps:
                    cp.wait()
                acc_ref[...] += _dot_tn(jnp.where(pw, 1.0, 0.0).astype(BF16), extra_stage[...])
                return carry

            lax.fori_loop(1, _n_windows(a, end, cap, win), extra, 0)

    out = x_ref[0] + g2_ref[0] * acc_ref[...]
    if final_norm:
        ms = jnp.mean(out * out, axis=-1, keepdims=True)
        out = out * lax.rsqrt(ms + EPS) * fin_ref[...]
    o_ref[0] = out


def _combine(s0, slot_t, x, g2, y, cap, win, stride, off, final_g=None):
    B, L, D = x.shape
    E = N_EXPERTS
    T = min(MOE_TILE, L)
    nt = L // T
    grid_spec = pltpu.PrefetchScalarGridSpec(
        num_scalar_prefetch=1,
        grid=(B, nt),
        in_specs=[pl.BlockSpec((1, E, T), lambda b, i, s: (b, 0, i)),
                  pl.BlockSpec((1, T, D), lambda b, i, s: (b, i, 0)),
                  pl.BlockSpec((1, 1, D), lambda b, i, s: (b, 0, 0)),
                  pl.BlockSpec((1, D), lambda b, i, s: (0, 0)),
                  pl.BlockSpec(memory_space=pl.ANY)],
        out_specs=pl.BlockSpec((1, T, D), lambda b, i, s: (b, i, 0)),
        scratch_shapes=[pltpu.VMEM((2, E * win, D), BF16), pltpu.VMEM((win, D), BF16),
                        pltpu.VMEM((T, D), F32), pltpu.SemaphoreType.DMA((3,))],
    )
    return pl.pallas_call(
        functools.partial(_combine_kernel, cap=cap, win=win, nt=nt, stride=stride, off=off,
                          final_norm=final_g is not None),
        grid_spec=grid_spec,
        out_shape=jax.ShapeDtypeStruct((B, L, D), F32),
        compiler_params=_cparams(("arbitrary", "arbitrary")),
        name="moe_combine",
    )(s0, slot_t, x, g2, (jnp.ones((D,), F32) if final_g is None else final_g).reshape(1, D), y)


def _expert_row_tile(m):
    for t in range(min(m, EXPERT_ROW_TILE_MAX) // SLOT_ALIGN * SLOT_ALIGN, 0, -SLOT_ALIGN):
        if m % t == 0:
            return t
    raise ValueError(f"no expert row tile for {m} rows")


def _moe(streams, w1, w3, w2, layer, final_g=None):
    E = N_EXPERTS
    B = streams[0][0].shape[0]
    plans, stride = [], 0
    for x, g2, h2, aff_t in streams:
        L = h2.shape[1]
        cap = CAPACITY_FACTOR * L // E
        plans.append((cap, min(MOE_WIN, cap), L // min(MOE_TILE, L), stride))
        stride += cap
    routed, prev = [], None
    for (x, g2, h2, aff_t), (cap, win, nt, off) in zip(streams, plans):
        slot_t, s0_pad = _select(aff_t, cap)
        s0 = s0_pad[:, :, :nt + 1].reshape(-1)
        prev = _dispatch(s0, h2, slot_t, aff_t, cap, win, stride, off, prev)
        routed.append((s0, slot_t))
    xe, ge = prev
    y = _experts(xe, w1, w3, w2, layer, ge, tm=_expert_row_tile(B * stride))
    outs = []
    for k, ((x, g2, h2, aff_t), (cap, win, nt, off), (s0, slot_t)) in enumerate(zip(streams, plans, routed)):
        outs.append(_combine(s0, slot_t, x, g2, y, cap, win, stride, off, final_g if k == 0 else None))
    return outs


def _rope_tables(n):
    half = HEAD_DIM // 2
    quarter = half // 2
    inv = ROPE_BASE ** (-jnp.arange(quarter, dtype=F32) / quarter)
    t = jnp.arange(n)
    row = (t // GRID_W).astype(F32)
    col = (t % GRID_W).astype(F32)
    lane = np.arange(128)
    within = lane % HEAD_DIM
    use_col = jnp.asarray((within >= half)[None, :])
    freq = inv[jnp.asarray(within % quarter)][None, :]
    ang = jnp.where(use_col, col[:, None], row[:, None]) * freq
    sign = jnp.asarray(np.where((lane % half) < quarter, -1.0, 1.0).astype(np.float32))[None, :]
    return jnp.cos(ang), jnp.sin(ang) * sign


def kernel(x, c, ctx, c_ctx, mod_w, mod_b, norm1_g, norm2_g, ab_w_in, ab_w_out, sgu_w, sgu_b, hgrn_lb_logits,
           hgrn_norm_g, attn_w_qkv, attn_w_out, attn_sink, router_w, expert_w1, expert_w3, expert_w2, final_g):
    B, N, D = x.shape
    Lc = ctx.shape[1]
    depth = mod_w.shape[0]

    rows = ((B + 1 + 7) // 8) * 8
    cvec = jnp.zeros((rows, D), F32).at[:B].set(c).at[B].set(c_ctx)
    mod = _mod_vectors(cvec, mod_w, mod_b)

    def mods(l):
        lat = [mod[l, :B, k * D:(k + 1) * D].reshape(B, 1, D) for k in range(6)]
        cx = [jnp.broadcast_to(mod[l, B, k * D:(k + 1) * D].reshape(1, 1, D), (B, 1, D)) for k in range(6)]
        return lat, cx

    lb_all = jnp.cumsum(jax.nn.softmax(hgrn_lb_logits.astype(F32), axis=0), axis=0)
    lmat_f = jnp.asarray(_hgrn_level_matrix(False), BF16)
    lmat_r = jnp.asarray(_hgrn_level_matrix(True), BF16)

    x_lat, x_ctx = x, ctx
    for l in range(depth):
        last = l == depth - 1
        (sh1, sc1, g1, sh2, sc2, g2), (csh1, csc1, cg1, csh2, csc2, cg2) = mods(l)
        wr_t = router_w[l].T
        if l % 2 == 0:
            e = l // 2
            w_in = ab_w_in[e].astype(BF16)
            w_out = ab_w_out[e].astype(BF16)
            sw = sgu_w[e].astype(BF16)
            sb_t = sgu_b[e].T
            hg = hgrn_norm_g[e].reshape(1, HGRN_DIM)
            s0 = jnp.zeros((2, B, HGRN_HEADS, HGRN_KDIM, HGRN_KDIM), F32)

            def even(xs, shift, scale, gate1, shift2, scale2, s_in):
                ya, z = _proj_sgu(xs, norm1_g[l], shift, scale, w_in, sw, sb_t)
                o_f, o_b, s_fin = _hgrn(z, lb_all[e], s_in, lmat_f, lmat_r)
                xn, h2, aff = _even_post(ya, o_f, o_b, z, hg, w_out, xs, gate1, norm2_g[l], shift2, scale2, wr_t)
                return xn, h2, aff, s_fin

            x_ctx, h2c, affc, s_ctx = even(x_ctx, csh1, csc1, cg1, csh2, csc2, s0)
            x_lat, h2l, affl, _ = even(x_lat, sh1, sc1, g1, sh2, sc2, s_ctx)
        else:
            if not last:
                raise NotImplementedError("context update after an attention layer")
            o_idx = l // 2
            w_qkv = attn_w_qkv[o_idx].astype(BF16)
            w_out = attn_w_out[o_idx].astype(BF16)
            cos, sin = _rope_tables(N)
            zc = _proj(x_ctx, norm1_g[l], csh1, csc1, w_qkv[:, ATTN_Q_DIM:])
            k_ctx = zc[..., :ATTN_KV_DIM].astype(BF16)
            v_ctx = zc[..., ATTN_KV_DIM:].astype(BF16)
            q, k, v = _qkv_lat(x_lat, norm1_g[l], sh1, sc1, w_qkv, cos, sin)
            o = _attention(q, k, v, k_ctx, v_ctx, attn_sink[o_idx])
            x_lat, h2l, affl = _odd_post(o, w_out, x_lat, g1, norm2_g[l], sh2, sc2, wr_t)
        streams = [(x_lat, g2, h2l, affl)]
        if not last:
            streams.append((x_ctx, cg2, h2c, affc))
        outs = _moe(streams, expert_w1, expert_w3, expert_w2, l, final_g if last else None)
        x_lat = outs[0]
        if not last:
            x_ctx = outs[1]
    return x_lat
```

```python
import functools

import numpy as np
import jax
import jax.numpy as jnp
from jax import lax
from jax.experimental import pallas as pl
from jax.experimental.pallas import tpu as pltpu

F32 = jnp.float32
BF16 = jnp.bfloat16

D_MODEL = 1024
GRID_W = 64
EPS = 1e-6
SGU_GROUPS = 4
SGU_GROUP_DIM = 128
SGU_DIM = SGU_GROUPS * SGU_GROUP_DIM
SGU_CHUNK = 128
HGRN_HEADS = 4
HGRN_KDIM = 128
HGRN_DIM = HGRN_HEADS * HGRN_KDIM
HGRN_CHUNK = 64
HGRN_LEVELS = 6
HGRN_COARSE = 3
HGRN_MXU_LEVELS = 5
HGRN_TBLK = 512
AB_IN = 2 * SGU_DIM + 5 * HGRN_DIM
ATTN_HEADS = 16
ATTN_KV_HEADS = 4
ATTN_GROUP = ATTN_HEADS // ATTN_KV_HEADS
HEAD_DIM = 64
ATTN_Q_DIM = ATTN_HEADS * HEAD_DIM
ATTN_KV_DIM = ATTN_KV_HEADS * HEAD_DIM
WINDOW = 128
ATTN_BLOCK = 128
ROPE_BASE = 10000.0
N_EXPERTS = 16
CAPACITY_FACTOR = 2
EXPERT_FF = 2048
NEG_BIG = -1e30
LOG2_E = 1.4426950408889634

VMEM_LIMIT_BYTES = 56 * 1024 * 1024


ROW_TILE = 512
ROW_TILE_WIDE = 1024


def _row_tile(n_rows, limit=ROW_TILE):
    return min(limit, n_rows)


def _cparams(sem):
    return pltpu.CompilerParams(dimension_semantics=sem, vmem_limit_bytes=VMEM_LIMIT_BYTES)


def _dot(a, b):
    return jnp.dot(a, b, preferred_element_type=F32)


def _dot_nt(a, b):
    return lax.dot_general(a, b, (((1,), (1,)), ((), ())), preferred_element_type=F32)


def _dot_tn(a, b):
    return lax.dot_general(a, b, (((0,), (0,)), ((), ())), preferred_element_type=F32)


def _split2(a):
    hi = a.astype(BF16)
    lo = (a - hi.astype(F32)).astype(BF16)
    return hi, lo


def _dot_f32x3(a, b):
    a_hi, a_lo = _split2(a)
    b_hi, b_lo = _split2(b)
    return _dot(a_hi, b_hi) + (_dot(a_hi, b_lo) + _dot(a_lo, b_hi))


def _silu(x):
    return x * jax.nn.sigmoid(x)


def _gelu(x):
    return 0.5 * x * (1.0 + lax.erf(x * np.float32(1.0 / np.sqrt(2.0))))


def _rms_modulate(x, g, shift, scale):
    ms = jnp.mean(x * x, axis=-1, keepdims=True)
    return (x * lax.rsqrt(ms + EPS) * g) * (1.0 + scale) + shift


def _mod_kernel(c_ref, w_ref, b_ref, o_ref):
    s = _silu(c_ref[...])
    o_ref[0] = _dot_f32x3(s, w_ref[0]) + b_ref[0]


def _mod_vectors(cvec, mod_w, mod_b):
    depth, d, n6 = mod_w.shape
    rows = cvec.shape[0]
    tn = 1536
    return pl.pallas_call(
        _mod_kernel,
        grid=(depth, n6 // tn),
        in_specs=[
            pl.BlockSpec((rows, d), lambda l, j: (0, 0)),
            pl.BlockSpec((1, d, tn), lambda l, j: (l, 0, j)),
            pl.BlockSpec((1, 1, tn), lambda l, j: (l, 0, j)),
        ],
        out_specs=pl.BlockSpec((1, rows, tn), lambda l, j: (l, 0, j)),
        out_shape=jax.ShapeDtypeStruct((depth, rows, n6), F32),
        compiler_params=_cparams(("arbitrary", "arbitrary")),
        name="mod_vectors",
    )(cvec, mod_w, mod_b.reshape(depth, 1, n6))


def _proj_kernel(x_ref, g_ref, sh_ref, sc_ref, w_ref, z_ref):
    h = _rms_modulate(x_ref[0], g_ref[...], sh_ref[0], sc_ref[0])
    z_ref[0] = _dot(h.astype(BF16), w_ref[...])


def _row_spec(tm, d):
    return pl.BlockSpec((1, tm, d), lambda b, i: (b, i, 0))


def _vec_spec(d):
    return pl.BlockSpec((1, 1, d), lambda b, i: (b, 0, 0))


def _const_spec(shape):
    nd = len(shape)
    return pl.BlockSpec(shape, lambda b, i: (0,) * nd)


def _proj(x, g, shift, scale, w):
    B, L, D = x.shape
    tm = _row_tile(L)
    n = w.shape[1]
    return pl.pallas_call(
        _proj_kernel,
        grid=(B, L // tm),
        in_specs=[_row_spec(tm, D), _const_spec((1, D)), _vec_spec(D), _vec_spec(D), _const_spec((D, n))],
        out_specs=_row_spec(tm, n),
        out_shape=jax.ShapeDtypeStruct((B, L, n), F32),
        compiler_params=_cparams(("parallel", "parallel")),
        name="norm_mod_proj",
    )(x, g.reshape(1, D), shift, scale, w)


def _sgu_rows(zu, zv, w_ref, b_ref, o_ref, rows):
    u = _gelu(zu)
    v = _gelu(zv)
    for g in range(SGU_GROUPS):
        cols = slice(g * SGU_GROUP_DIM, (g + 1) * SGU_GROUP_DIM)
        vg = v[:, cols]
        mu = jnp.mean(vg, axis=-1, keepdims=True)
        vc = vg - mu
        var = jnp.mean(vc * vc, axis=-1, keepdims=True)
        vn = (vc * lax.rsqrt(var + EPS)).astype(BF16)
        s = _dot(w_ref[g], vn) + b_ref[:, g:g + 1]
        o_ref[0, rows, cols] = (u[:, cols] * s).astype(BF16)


def _proj_sgu_kernel(x_ref, g_ref, sh_ref, sc_ref, w_ref, sw_ref, sb_ref, ya_ref, zr_ref, *, tm):
    h = _rms_modulate(x_ref[0], g_ref[...], sh_ref[0], sc_ref[0]).astype(BF16)
    z_uv = _dot(h, w_ref[:, :2 * SGU_DIM])
    zr_ref[0] = _dot(h, w_ref[:, 2 * SGU_DIM:])
    for ch in range(tm // SGU_CHUNK):
        rows = slice(ch * SGU_CHUNK, (ch + 1) * SGU_CHUNK)
        _sgu_rows(z_uv[rows, :SGU_DIM], z_uv[rows, SGU_DIM:], sw_ref, sb_ref, ya_ref, rows)


def _proj_sgu(x, g, shift, scale, w, sgu_w_bf16, sgu_b_t):
    B, L, D = x.shape
    tm = _row_tile(L)
    n_rest = w.shape[1] - 2 * SGU_DIM
    return pl.pallas_call(
        functools.partial(_proj_sgu_kernel, tm=tm),
        grid=(B, L // tm),
        in_specs=[_row_spec(tm, D), _const_spec((1, D)), _vec_spec(D), _vec_spec(D), _const_spec((D, w.shape[1])),
                  _const_spec((SGU_GROUPS, SGU_CHUNK, SGU_CHUNK)), _const_spec((SGU_CHUNK, SGU_GROUPS))],
        out_specs=[_row_spec(tm, SGU_DIM), _row_spec(tm, n_rest)],
        out_shape=[jax.ShapeDtypeStruct((B, L, SGU_DIM), BF16), jax.ShapeDtypeStruct((B, L, n_rest), F32)],
        compiler_params=_cparams(("parallel", "parallel")),
        name="norm_mod_proj_sgu",
    )(x, g.reshape(1, D), shift, scale, w, sgu_w_bf16, sgu_b_t)


def _hgrn_level_matrix(reverse):
    C = HGRN_CHUNK
    t = np.arange(C)[:, None]
    s = np.arange(C)[None, :]
    blocks = []
    blocks.append(((s <= t) if not reverse else (s >= t)).astype(np.float32))
    for l in range(HGRN_COARSE, HGRN_MXU_LEVELS):
        m = C >> (l + 1)
        base = (t // (2 * m)) * (2 * m)
        if not reverse:
            r = base + m - 1
            qrow = (t % (2 * m)) >= m
            mq = (s > r) & (s <= t)
            mk = (s > t) & (s <= r)
        else:
            r = base + m
            qrow = (t % (2 * m)) < m
            mq = (s >= t) & (s < r)
            mk = (s >= r) & (s < t)
        blocks.append(np.where(qrow, mq, mk).astype(np.float32))
    return np.concatenate(blocks, axis=0)


def _hgrn_masks(reverse):
    C = HGRN_CHUNK
    t = lax.broadcasted_iota(jnp.int32, (C, C), 0)
    s = lax.broadcasted_iota(jnp.int32, (C, C), 1)
    r = lax.broadcasted_iota(jnp.int32, (C, 1), 0)
    masks, qrows = [], []
    for l in range(HGRN_LEVELS):
        m = C >> (l + 1)
        same = (t // (2 * m)) == (s // (2 * m))
        if not reverse:
            mk = same & ((t % (2 * m)) >= m) & ((s % (2 * m)) < m)
            qr = (r % (2 * m)) >= m
        else:
            mk = same & ((t % (2 * m)) < m) & ((s % (2 * m)) >= m)
            qr = (r % (2 * m)) < m
        masks.append(mk)
        qrows.append(qr)
    return masks, qrows


def _hgrn_chunk(zq, zf, zi, lb, st, lmat, masks, qrows, reverse):
    C = HGRN_CHUNK
    f = lb + (1.0 - lb) * jax.nn.sigmoid(zf)
    g = jnp.log(f)
    kk = 1.0 - f
    q = _silu(zq)
    vb = zi.astype(BF16)
    g_hi, g_lo = _split2(g)
    d2 = _dot(lmat, jnp.concatenate([g_hi, g_lo], axis=1))
    d = d2[:, :HGRN_KDIM] + d2[:, HGRN_KDIM:]
    b = d[0:C]
    b_end = b[0:1] if reverse else b[C - 1:C]
    dec = jnp.exp(b_end)
    qb = (q * jnp.exp(b)).astype(BF16)
    kb = (kk * jnp.exp(b_end - b)).astype(BF16)
    o = _dot_nt(qb, st.astype(BF16))
    o = o + jnp.sum(q * kk, axis=1, keepdims=True) * zi
    shift = C - 1 if reverse else 1
    near = jnp.sum(q * f * pltpu.roll(kk, shift, 0), axis=1, keepdims=True)
    o = o + jnp.where(qrows[HGRN_LEVELS - 1], near, 0.0) * pltpu.roll(zi, shift, 0)
    a = None
    for l in range(HGRN_MXU_LEVELS):
        if l < HGRN_COARSE:
            m = C >> (l + 1)
            r = m if reverse else m - 1
            ref = jnp.broadcast_to(b.reshape(C // (2 * m), 2 * m, HGRN_KDIM)[:, r:r + 1, :],
                                   (C // (2 * m), 2 * m, HGRN_KDIM)).reshape(C, HGRN_KDIM)
            dl = jnp.where(qrows[l], b - ref, ref - b)
        else:
            dl = d[(1 + l - HGRN_COARSE) * C:(2 + l - HGRN_COARSE) * C]
        mix = (jnp.where(qrows[l], q, kk) * jnp.exp(dl)).astype(BF16)
        p = jnp.where(masks[l], _dot_nt(mix, mix), 0.0)
        a = p if a is None else a + p
    o = o + _dot(a.astype(BF16), vb)
    st_new = st * dec + _dot_tn(vb, kb)
    return o, st_new


def _hgrn_kernel(zqf_ref, zff_ref, zif_ref, zqb_ref, zfb_ref, zib_ref, lb_ref, s0_ref, lf_ref, lr_ref,
                 of_ref, ob_ref, sfin_ref, st_ref, *, tb):
    i = pl.program_id(1)
    nb = pl.num_programs(1)

    @pl.when(i == 0)
    def _():
        st_ref[...] = s0_ref[:, 0]

    C = HGRN_CHUNK
    nch = tb // C
    for d in range(2):
        reverse = d == 1
        masks, qrows = _hgrn_masks(reverse)
        lmat = (lr_ref if reverse else lf_ref)[...]
        zq_ref, zf_ref, zi_ref, o_ref = ((zqb_ref, zfb_ref, zib_ref, ob_ref) if reverse
                                         else (zqf_ref, zff_ref, zif_ref, of_ref))
        for h in range(HGRN_HEADS):
            cols = slice(h * HGRN_KDIM, (h + 1) * HGRN_KDIM)
            lb = lb_ref[d:d + 1, cols]
            st = st_ref[d, h]
            order = range(nch - 1, -1, -1) if reverse else range(nch)
            for c in order:
                rows = slice(c * C, (c + 1) * C)
                o, st = _hgrn_chunk(zq_ref[0, rows, cols], zf_ref[0, rows, cols], zi_ref[0, rows, cols],
                                    lb, st, lmat, masks, qrows, reverse)
                o_ref[0, rows, cols] = o.astype(BF16)
            st_ref[d, h] = st

    @pl.when(i == nb - 1)
    def _():
        sfin_ref[:, 0] = st_ref[...]


def _hgrn(z, lb, s0, lmat_f, lmat_r):
    B, L, _ = z.shape
    tb = min(HGRN_TBLK, L)
    nb = L // tb
    nl = lmat_f.shape[0]

    def zspec(col, rev):
        if rev:
            return pl.BlockSpec((1, tb, HGRN_DIM), lambda b, i: (b, nb - 1 - i, col))
        return pl.BlockSpec((1, tb, HGRN_DIM), lambda b, i: (b, i, col))

    st_spec = pl.BlockSpec((2, 1, HGRN_HEADS, HGRN_KDIM, HGRN_KDIM), lambda b, i: (0, b, 0, 0, 0))
    return pl.pallas_call(
        functools.partial(_hgrn_kernel, tb=tb),
        grid=(B, nb),
        in_specs=[zspec(0, False), zspec(1, False), zspec(3, False),
                  zspec(0, True), zspec(2, True), zspec(3, True),
                  _const_spec((2, HGRN_DIM)), st_spec,
                  _const_spec((nl, HGRN_CHUNK)), _const_spec((nl, HGRN_CHUNK))],
        out_specs=[pl.BlockSpec((1, tb, HGRN_DIM), lambda b, i: (b, i, 0)),
                   pl.BlockSpec((1, tb, HGRN_DIM), lambda b, i: (b, nb - 1 - i, 0)),
                   st_spec],
        out_shape=[jax.ShapeDtypeStruct((B, L, HGRN_DIM), BF16),
                   jax.ShapeDtypeStruct((B, L, HGRN_DIM), BF16),
                   jax.ShapeDtypeStruct(s0.shape, F32)],
        scratch_shapes=[pltpu.VMEM((2, HGRN_HEADS, HGRN_KDIM, HGRN_KDIM), F32)],
        compiler_params=_cparams(("parallel", "arbitrary")),
        name="hgrn2_scan",
    )(z, z, z, z, z, z, lb, s0, lmat_f, lmat_r)


def _residual_router_tail(x, y, g1, n2g, sh2, sc2, wr, xo_ref, h2_ref, aff_ref):
    xn = x + g1 * y
    xo_ref[0] = xn
    h2 = _rms_modulate(xn, n2g, sh2, sc2)
    h2_ref[0] = h2.astype(BF16)
    w_hi, w_lo = _split2(wr)
    h_hi, h_lo = _split2(h2)
    logits = _dot_nt(w_hi, h_hi) + (_dot_nt(w_hi, h_lo) + _dot_nt(w_lo, h_hi))
    mx = jnp.max(logits, axis=0, keepdims=True)
    p = jnp.exp(logits - mx)
    aff_ref[0] = p / jnp.sum(p, axis=0, keepdims=True)


def _even_post_kernel(ya_ref, of_ref, ob_ref, zg_ref, hg_ref, w_ref, x_ref, g1_ref, n2g_ref, sh2_ref,
                      sc2_ref, wr_ref, xo_ref, h2_ref, aff_ref):
    o = of_ref[0].astype(F32) + ob_ref[0].astype(F32)
    zg = zg_ref[0]
    parts = []
    for h in range(HGRN_HEADS):
        cols = slice(h * HGRN_KDIM, (h + 1) * HGRN_KDIM)
        oh = o[:, cols]
        ms = jnp.mean(oh * oh, axis=-1, keepdims=True)
        parts.append(oh * lax.rsqrt(ms + EPS) * hg_ref[:, cols])
    yb = (jnp.concatenate(parts, axis=1) * _silu(zg)).astype(BF16)
    y = _dot(ya_ref[0], w_ref[:SGU_DIM, :]) + _dot(yb, w_ref[SGU_DIM:, :])
    _residual_router_tail(x_ref[0], y, g1_ref[0], n2g_ref[...], sh2_ref[0], sc2_ref[0], wr_ref[...],
                          xo_ref, h2_ref, aff_ref)


def _odd_post_kernel(o_ref, w_ref, x_ref, g1_ref, n2g_ref, sh2_ref, sc2_ref, wr_ref, xo_ref, h2_ref,
                     aff_ref):
    y = _dot(o_ref[0], w_ref[...])
    _residual_router_tail(x_ref[0], y, g1_ref[0], n2g_ref[...], sh2_ref[0], sc2_ref[0], wr_ref[...],
                          xo_ref, h2_ref, aff_ref)


def _post_out_specs(B, L, tm):
    D = D_MODEL
    specs = [_row_spec(tm, D), _row_spec(tm, D), pl.BlockSpec((1, N_EXPERTS, tm), lambda b, i: (b, 0, i))]
    shapes = [jax.ShapeDtypeStruct((B, L, D), F32), jax.ShapeDtypeStruct((B, L, D), BF16),
              jax.ShapeDtypeStruct((B, N_EXPERTS, L), F32)]
    return specs, shapes


def _even_post(ya, o_f, o_b, z, hg, w_out, x, g1, n2g, sh2, sc2, wr):
    B, L, D = x.shape
    tm = _row_tile(L, ROW_TILE_WIDE)
    specs, shapes = _post_out_specs(B, L, tm)
    return pl.pallas_call(
        _even_post_kernel,
        grid=(B, L // tm),
        in_specs=[_row_spec(tm, SGU_DIM), _row_spec(tm, HGRN_DIM), _row_spec(tm, HGRN_DIM),
                  pl.BlockSpec((1, tm, HGRN_DIM), lambda b, i: (b, i, 4)),
                  _const_spec((1, HGRN_DIM)), _const_spec((D, D)), _row_spec(tm, D), _vec_spec(D),
                  _const_spec((1, D)), _vec_spec(D), _vec_spec(D), _const_spec((N_EXPERTS, D))],
        out_specs=specs,
        out_shape=shapes,
        compiler_params=_cparams(("parallel", "parallel")),
        name="even_post_router",
    )(ya, o_f, o_b, z, hg, w_out, x, g1, n2g.reshape(1, D), sh2, sc2, wr)


def _odd_post(o, w_out, x, g1, n2g, sh2, sc2, wr):
    B, L, D = x.shape
    tm = _row_tile(L, ROW_TILE_WIDE)
    specs, shapes = _post_out_specs(B, L, tm)
    return pl.pallas_call(
        _odd_post_kernel,
        grid=(B, L // tm),
        in_specs=[_row_spec(tm, ATTN_Q_DIM), _const_spec((ATTN_Q_DIM, D)), _row_spec(tm, D), _vec_spec(D),
                  _const_spec((1, D)), _vec_spec(D), _vec_spec(D), _const_spec((N_EXPERTS, D))],
        out_specs=specs,
        out_shape=shapes,
        compiler_params=_cparams(("parallel", "parallel")),
        name="odd_post_router",
    )(o, w_out, x, g1, n2g.reshape(1, D), sh2, sc2, wr)


def _rope(x, cos, sin_signed):
    lane = lax.broadcasted_iota(jnp.int32, (x.shape[0], 128), 1)
    first = (lane % 32) < 16
    outs = []
    for j in range(x.shape[1] // 128):
        xb = x[:, j * 128:(j + 1) * 128]
        sw = jnp.where(first, pltpu.roll(xb, 112, 1), pltpu.roll(xb, 16, 1))
        outs.append(xb * cos + sw * sin_signed)
    return jnp.concatenate(outs, axis=1)


def _qkv_lat_kernel(x_ref, g_ref, sh_ref, sc_ref, w_ref, cos_ref, sin_ref, q_ref, k_ref, v_ref):
    h = _rms_modulate(x_ref[0], g_ref[...], sh_ref[0], sc_ref[0])
    z = _dot(h.astype(BF16), w_ref[...])
    cos = cos_ref[...]
    sin = sin_ref[...]
    scale = np.float32(HEAD_DIM ** -0.5 * LOG2_E)
    q_ref[0] = (_rope(z[:, :ATTN_Q_DIM], cos, sin) * scale).astype(BF16)
    k_ref[0] = _rope(z[:, ATTN_Q_DIM:ATTN_Q_DIM + ATTN_KV_DIM], cos, sin).astype(BF16)
    v_ref[0] = z[:, ATTN_Q_DIM + ATTN_KV_DIM:].astype(BF16)


def _qkv_lat(x, g, shift, scale, w, cos, sin):
    B, L, D = x.shape
    tm = _row_tile(L, ROW_TILE_WIDE)
    n = w.shape[1]
    tab = pl.BlockSpec((tm, 128), lambda b, i: (i, 0))
    return pl.pallas_call(
        _qkv_lat_kernel,
        grid=(B, L // tm),
        in_specs=[_row_spec(tm, D), _const_spec((1, D)), _vec_spec(D), _vec_spec(D), _const_spec((D, n)), tab, tab],
        out_specs=[_row_spec(tm, ATTN_Q_DIM), _row_spec(tm, ATTN_KV_DIM), _row_spec(tm, ATTN_KV_DIM)],
        out_shape=[jax.ShapeDtypeStruct((B, L, ATTN_Q_DIM), BF16),
                   jax.ShapeDtypeStruct((B, L, ATTN_KV_DIM), BF16),
                   jax.ShapeDtypeStruct((B, L, ATTN_KV_DIM), BF16)],
        compiler_params=_cparams(("parallel", "parallel")),
        name="norm_qkv_rope",
    )(x, g.reshape(1, D), shift, scale, w, cos, sin)


def _attn_block(q_ref, k_parts, v_parts, kx_ref, vx_ref, sink_ref, o_ref, rows, bi, n_tokens):
    blk = ATTN_BLOCK
    n_loc = 3 * blk
    nk = n_loc + kx_ref.shape[1]
    n_pairs = ATTN_GROUP // 2
    nq = n_pairs * blk
    pad = 16
    s_idx = lax.broadcasted_iota(jnp.int32, (n_loc, nq), 0)
    r_idx = lax.broadcasted_iota(jnp.int32, (n_loc, nq), 1) % blk
    j = bi * blk - blk + s_idx
    valid = (jnp.abs(s_idx - r_idx - blk) <= WINDOW) & (j >= 0) & (j < n_tokens)
    lane = lax.broadcasted_iota(jnp.int32, (nk, 128), 1)
    qcol = lax.broadcasted_iota(jnp.int32, (1, nq), 1)
    prow = lax.broadcasted_iota(jnp.int32, (pad, nq), 0)
    er = lax.broadcasted_iota(jnp.int32, (2 * nk + pad, 128), 0)
    el = lax.broadcasted_iota(jnp.int32, (2 * nk + pad, 128), 1)
    lo_rows = (er < nk) | (er == 2 * nk)
    hi_rows = ((er >= nk) & (er < 2 * nk)) | (er == 2 * nk + 1)
    ind = jnp.where((lo_rows & (el < HEAD_DIM)) | (hi_rows & (el >= HEAD_DIM)), 1.0, 0.0).astype(BF16)
    zpad = jnp.zeros((pad, 128), BF16)
    scores, values = [], []
    for kvh in range(ATTN_KV_HEADS):
        cols = slice((kvh // 2) * 128, (kvh // 2 + 1) * 128)
        own = (lane < HEAD_DIM) if kvh % 2 == 0 else (lane >= HEAD_DIM)
        k_all = jnp.concatenate([r[0, rs, cols] for r, rs in k_parts] + [kx_ref[0, :, cols]], axis=0)
        v_all = jnp.concatenate([r[0, rs, cols] for r, rs in v_parts] + [vx_ref[0, :, cols]], axis=0)
        k_own = jnp.where(own, k_all, jnp.zeros_like(k_all))
        v_own = jnp.where(own, v_all, jnp.zeros_like(v_all))
        k_oth = pltpu.roll(k_own, HEAD_DIM, 1)
        v_oth = pltpu.roll(v_own, HEAD_DIM, 1)
        if kvh % 2 == 0:
            k_cat = jnp.concatenate([k_own, k_oth], axis=0)
            v_cat = jnp.concatenate([v_own, v_oth, zpad], axis=0)
        else:
            k_cat = jnp.concatenate([k_oth, k_own], axis=0)
            v_cat = jnp.concatenate([v_oth, v_own, zpad], axis=0)
        q2 = jnp.concatenate([q_ref[0, rows, (kvh * n_pairs + p) * 128:(kvh * n_pairs + p + 1) * 128]
                              for p in range(n_pairs)], axis=0)
        scores.append(_dot_nt(k_cat, q2))
        values.append(jnp.concatenate([v_cat, ind], axis=1))
    probs = []
    for kvh in range(ATTN_KV_HEADS):
        st = scores[kvh]
        ps, es = [], []
        for sub in range(2):
            sh = st[sub * nk:(sub + 1) * nk]
            sh = jnp.concatenate([jnp.where(valid, sh[:n_loc], NEG_BIG), sh[n_loc:]], axis=0)
            hd0 = kvh * ATTN_GROUP + sub
            sink = sink_ref[:, hd0:hd0 + 1]
            for p in range(1, n_pairs):
                sink = jnp.where(qcol < p * blk, sink, sink_ref[:, hd0 + 2 * p:hd0 + 2 * p + 1])
            m = jnp.maximum(jnp.max(sh, axis=0, keepdims=True), sink)
            ps.append(jnp.exp2(sh - m).astype(BF16))
            es.append(jnp.exp2(sink - m))
        sink_rows = jnp.where(prow == 0, es[0], jnp.where(prow == 1, es[1], 0.0)).astype(BF16)
        probs.append(jnp.concatenate(ps + [sink_rows], axis=0))
    for kvh in range(ATTN_KV_HEADS):
        od = _dot_tn(probs[kvh], values[kvh])
        o = (od[:, :128] / od[:, 128:]).astype(BF16)
        for p in range(n_pairs):
            o_ref[0, rows, (kvh * n_pairs + p) * 128:(kvh * n_pairs + p + 1) * 128] = o[p * blk:(p + 1) * blk]


def _attn_kernel(q_ref, kp_ref, kc_ref, kn_ref, vp_ref, vc_ref, vn_ref, kx_ref, vx_ref, sink_ref, o_ref,
                 *, n_tokens):
    blk = ATTN_BLOCK
    lo, hi, one = slice(0, blk), slice(blk, 2 * blk), slice(0, blk)
    first = 2 * pl.program_id(1)
    _attn_block(q_ref, [(kp_ref, one), (kc_ref, lo), (kc_ref, hi)], [(vp_ref, one), (vc_ref, lo), (vc_ref, hi)],
                kx_ref, vx_ref, sink_ref, o_ref, lo, first, n_tokens)
    _attn_block(q_ref, [(kc_ref, lo), (kc_ref, hi), (kn_ref, one)], [(vc_ref, lo), (vc_ref, hi), (vn_ref, one)],
                kx_ref, vx_ref, sink_ref, o_ref, hi, first + 1, n_tokens)


def _attention(q, k, v, k_ctx, v_ctx, sink):
    B, N, _ = q.shape
    Lc = k_ctx.shape[1]
    blk = ATTN_BLOCK
    nb = N // blk
    assert nb % 2 == 0

    def edge_spec(off):
        return pl.BlockSpec((1, blk, ATTN_KV_DIM), lambda b, i: (b, jnp.clip(2 * i + off, 0, nb - 1), 0))

    pair_spec = pl.BlockSpec((1, 2 * blk, ATTN_KV_DIM), lambda b, i: (b, i, 0))
    ctx_spec = pl.BlockSpec((1, Lc, ATTN_KV_DIM), lambda b, i: (b, 0, 0))
    return pl.pallas_call(
        functools.partial(_attn_kernel, n_tokens=N),
        grid=(B, nb // 2),
        in_specs=[_row_spec(2 * blk, ATTN_Q_DIM), edge_spec(-1), pair_spec, edge_spec(2),
                  edge_spec(-1), pair_spec, edge_spec(2), ctx_spec, ctx_spec,
                  _const_spec((1, ATTN_HEADS))],
        out_specs=_row_spec(2 * blk, ATTN_Q_DIM),
        out_shape=jax.ShapeDtypeStruct((B, N, ATTN_Q_DIM), BF16),
        compiler_params=_cparams(("parallel", "parallel")),
        name="window_attention",
    )(q, k, k, k, v, v, v, k_ctx, v_ctx, (sink * np.float32(LOG2_E)).reshape(1, ATTN_HEADS))


MOE_TILE = 256
MOE_WIN = 64
MOE_TILES_PER_STEP = 2
SLOT_ALIGN = 16
EXPERT_ROW_TILE_MAX = 1152


def _select_kernel(aff_ref, slot_ref, s0_ref, *, cap):
    x = aff_ref[...]
    E, L = x.shape
    T = min(MOE_TILE, L)

    def count(mask):
        return jnp.sum(jnp.where(mask, 1.0, 0.0), axis=1, keepdims=True)

    def as_f32(bits):
        return lax.bitcast_convert_type(bits, F32)

    def body(_, c):
        lo, hi = c
        mid = lo + lax.shift_right_logical(hi - lo, 1)
        ok = count(x >= as_f32(mid)) >= cap
        return jnp.where(ok, mid, lo), jnp.where(ok, hi, mid)

    lo0 = jnp.zeros((E, 1), jnp.int32)
    hi0 = jnp.full((E, 1), 0x7F800000, jnp.int32)
    lo, hi = lax.fori_loop(0, 31, body, (lo0, hi0))
    gt = x >= as_f32(hi)
    eq = (x >= as_f32(lo)) & jnp.logical_not(gt)
    need = cap - count(gt)
    ti = lax.broadcasted_iota(jnp.int32, (T, T), 0)
    tj = lax.broadcasted_iota(jnp.int32, (T, T), 1)
    upper = jnp.where(ti < tj, 1.0, 0.0).astype(BF16)

    def excl_cumsum(mask):
        carry = jnp.zeros((E, 1), F32)
        outs, starts = [], []
        for t in range(L // T):
            v = jnp.where(mask[:, t * T:(t + 1) * T], 1.0, 0.0)
            starts.append(carry)
            outs.append(_dot(v.astype(BF16), upper) + carry)
            carry = carry + jnp.sum(v, axis=1, keepdims=True)
        starts.append(carry)
        return jnp.concatenate(outs, axis=1), starts

    eq_rank, _ = excl_cumsum(eq)
    sel = gt | (eq & (eq_rank < need))
    slot, starts = excl_cumsum(sel)
    slot_ref[...] = jnp.where(sel, slot, -1.0).astype(jnp.int32)
    lane = lax.broadcasted_iota(jnp.int32, (E, 128), 1)
    s0 = jnp.zeros((E, 128), F32)
    for t, st in enumerate(starts):
        s0 = jnp.where(lane == t, st, s0)
    s0_ref[...] = s0.astype(jnp.int32)


def _select(aff_t, cap):
    B, E, L = aff_t.shape
    assert L // min(MOE_TILE, L) + 1 <= 128
    rows = B * E
    slot, s0 = pl.pallas_call(
        functools.partial(_select_kernel, cap=cap),
        grid=(1,),
        in_specs=[pl.BlockSpec((rows, L), lambda i: (0, 0))],
        out_specs=[pl.BlockSpec((rows, L), lambda i: (0, 0)), pl.BlockSpec((rows, 128), lambda i: (0, 0))],
        out_shape=[jax.ShapeDtypeStruct((rows, L), jnp.int32), jax.ShapeDtypeStruct((rows, 128), jnp.int32)],
        compiler_params=_cparams(("arbitrary",)),
        name="moe_select",
    )(aff_t.reshape(rows, L))
    return slot.reshape(B, E, L), s0.reshape(B, E, 128)


def _tile_group(nt):
    return MOE_TILES_PER_STEP if nt % MOE_TILES_PER_STEP == 0 else 1


def _window_start(s0_ref, b, e, i, nt, cap, win):
    base = (b * N_EXPERTS + e) * (nt + 1) + i
    a = (s0_ref[base] // SLOT_ALIGN) * SLOT_ALIGN
    return a, s0_ref[base + 1], jnp.minimum(a, cap - win)


def _n_windows(a, end, cap, win):
    return jnp.where(a + win >= cap, 1, jnp.maximum((end - a + win - 1) // win, 1))


def _slot_hits(slot_ref, cols, e, a, w, cap, win, jrow):
    lo_s = a + w * win
    r0 = jnp.minimum(lo_s, cap - win)
    se = slot_ref[0, e:e + 1, cols]
    return r0, ((se - r0) == jrow) & (se >= lo_s)


def _first_windows(s0_ref, slot_ref, cols, b, i, nt, cap, win):
    T = cols.stop - cols.start
    jrow = lax.broadcasted_iota(jnp.int32, (win, T), 0)
    first = []
    n_extra = 0
    for e in range(N_EXPERTS):
        a, end, _ = _window_start(s0_ref, b, e, i, nt, cap, win)
        r0, p = _slot_hits(slot_ref, cols, e, a, 0, cap, win, jrow)
        first.append((a, end, r0, p))
        n_extra = n_extra + (_n_windows(a, end, cap, win) - 1)
    onehot_t = jnp.where(jnp.concatenate([f[3] for f in first], axis=0), 1.0, 0.0).astype(BF16)
    return first, onehot_t, n_extra, jrow


def _dispatch_kernel(s0_ref, h_ref, slot_ref, aff_ref, *rest, cap, win, nt, row_off, group):
    xe_ref, ge_ref = rest[-2:]
    b = pl.program_id(0)
    i = pl.program_id(1)

    @pl.when(i == 0)
    def _():
        xe_ref[...] = jnp.zeros_like(xe_ref)
        ge_ref[...] = jnp.zeros_like(ge_ref)

    T = h_ref.shape[1] // group
    for j in range(group):
        cols = slice(j * T, (j + 1) * T)
        h = h_ref[0, cols, :]

        def place(e, r0, p, rows, cols=cols):
            r0 = pl.multiple_of(row_off + r0, SLOT_ALIGN)
            gate = jnp.sum(jnp.where(p, aff_ref[0, e:e + 1, cols], 0.0), axis=1, keepdims=True)
            xe_ref[e, pl.ds(r0, win), :] += rows.astype(BF16)
            ge_ref[e, pl.ds(r0, win), :] += gate

        first, onehot_t, n_extra, jrow = _first_windows(s0_ref, slot_ref, cols, b, i * group + j, nt, cap, win)
        rows = _dot(onehot_t, h)
        for e in range(N_EXPERTS):
            _, _, r0, p = first[e]
            place(e, r0, p, rows[e * win:(e + 1) * win])

        @pl.when(n_extra > 0)
        def _(first=first, cols=cols, h=h, jrow=jrow, place=place):
            for e in range(N_EXPERTS):
                a, end, _, _ = first[e]

                def extra(w, carry, e=e, a=a):
                    r0w, pw = _slot_hits(slot_ref, cols, e, a, w, cap, win, jrow)
                    place(e, r0w, pw, _dot(jnp.where(pw, 1.0, 0.0).astype(BF16), h))
                    return carry

                lax.fori_loop(1, _n_windows(a, end, cap, win), extra, 0)


def _dispatch(s0, h2, slot_t, aff_t, cap, win, stride, off, prev=None):
    B, L, D = h2.shape
    E = N_EXPERTS
    T = min(MOE_TILE, L)
    nt = L // T
    group = _tile_group(nt)
    if prev is None:
        blk_rows, row_off, per_sample, blk0 = stride, off, 1, 0
    else:
        assert stride % cap == 0 and off % cap == 0
        blk_rows, row_off, per_sample, blk0 = cap, 0, stride // cap, off // cap
    in_specs = [pl.BlockSpec((1, group * T, D), lambda b, i, s: (b, i, 0)),
                pl.BlockSpec((1, E, group * T), lambda b, i, s: (b, 0, i)),
                pl.BlockSpec((1, E, group * T), lambda b, i, s: (b, 0, i))]
    args = [s0, h2, slot_t, aff_t]
    aliases = {}
    if prev is not None:
        in_specs += [pl.BlockSpec(memory_space=pl.ANY), pl.BlockSpec(memory_space=pl.ANY)]
        aliases = {len(args): 0, len(args) + 1: 1}
        args += list(prev)
    grid_spec = pltpu.PrefetchScalarGridSpec(
        num_scalar_prefetch=1,
        grid=(B, nt // group),
        in_specs=in_specs,
        out_specs=[pl.BlockSpec((E, blk_rows, D), lambda b, i, s: (0, blk0 + b * per_sample, 0)),
                   pl.BlockSpec((E, blk_rows, 1), lambda b, i, s: (0, blk0 + b * per_sample, 0))],
    )
    return pl.pallas_call(
        functools.partial(_dispatch_kernel, cap=cap, win=win, nt=nt, row_off=row_off, group=group),
        grid_spec=grid_spec,
        out_shape=[jax.ShapeDtypeStruct((E, B * stride, D), BF16), jax.ShapeDtypeStruct((E, B * stride, 1), F32)],
        input_output_aliases=aliases,
        compiler_params=_cparams(("parallel", "arbitrary")),
        name="moe_dispatch",
    )(*args)


def _expert_kernel(x_ref, w1_ref, w3_ref, w2_ref, gate_ref, y_ref, wb1, wb3, wb2, acc_ref, *, tm, nf):
    f = pl.program_id(1)
    m = pl.program_id(2)

    @pl.when(m == 0)
    def _():
        wb1[...] = w1_ref[0, 0].astype(BF16)
        wb3[...] = w3_ref[0, 0].astype(BF16)
        wb2[...] = w2_ref[0, 0].astype(BF16)

    tile = pl.ds(pl.multiple_of(m * tm, tm), tm)

    @pl.when(f == 0)
    def _():
        acc_ref[tile, :] = jnp.zeros((tm, acc_ref.shape[1]), F32)

    x = x_ref[0]
    a = _dot(x, wb1[...])
    u = _dot(x, wb3[...])
    acc_ref[tile, :] += _dot((_silu(a) * u).astype(BF16), wb2[...])

    @pl.when(f == nf - 1)
    def _():
        y_ref[0] = (acc_ref[tile, :] * gate_ref[0]).astype(BF16)


def _experts(xe, w1, w3, w2, layer, gate, tm, tf=512):
    E, M, D = xe.shape
    F = w1.shape[3]
    nf = F // tf
    out = pl.BlockSpec((1, tm, D), lambda e, f, m: (e, jnp.where(f == nf - 1, m, 0), 0))
    return pl.pallas_call(
        functools.partial(_expert_kernel, tm=tm, nf=nf),
        grid=(E, nf, M // tm),
        in_specs=[pl.BlockSpec((1, tm, D), lambda e, f, m: (e, m, 0)),
                  pl.BlockSpec((1, 1, D, tf), lambda e, f, m: (layer, e, 0, f)),
                  pl.BlockSpec((1, 1, D, tf), lambda e, f, m: (layer, e, 0, f)),
                  pl.BlockSpec((1, 1, tf, D), lambda e, f, m: (layer, e, f, 0)),
                  pl.BlockSpec((1, tm, 1), lambda e, f, m: (e, m, 0))],
        out_specs=out,
        out_shape=jax.ShapeDtypeStruct((E, M, D), BF16),
        scratch_shapes=[pltpu.VMEM((D, tf), BF16), pltpu.VMEM((D, tf), BF16), pltpu.VMEM((tf, D), BF16),
                        pltpu.VMEM((M, D), F32)],
        compiler_params=_cparams(("parallel", "arbitrary", "arbitrary")),
        name="expert_ffn",
    )(xe, w1, w3, w2, gate)


def _combine_kernel(s0_ref, slot_ref, x_ref, g2_ref, fin_ref, y_hbm, o_ref, stage, extra_stage, acc_ref,
                    sem, *, cap, win, nt, stride, off, group, final_norm):
    E = N_EXPERTS
    b = pl.program_id(0)
    i = pl.program_id(1)
    steps = nt // group
    step = b * steps + i
    total = pl.num_programs(0) * steps
    buf = step % 2
    T = x_ref.shape[1] // group

    def window_copies(src_row, dst, e, s):
        src_row = pl.multiple_of(src_row, SLOT_ALIGN)
        return (pltpu.make_async_copy(y_hbm.at[e, pl.ds(src_row, win)], dst, s),)

    def step_copies(bb, ii, bf):
        cps = []
        for j in range(group):
            for e in range(E):
                _, _, r0 = _window_start(s0_ref, bb, e, ii * group + j, nt, cap, win)
                cps.extend(window_copies(bb * stride + off + r0, stage.at[bf, j, pl.ds(e * win, win)], e, sem.at[bf]))
        return cps

    @pl.when(step == 0)
    def _():
        for cp in step_copies(b, i, 0):
            cp.start()

    nxt = step + 1

    @pl.when(nxt < total)
    def _():
        for cp in step_copies(nxt // steps, nxt % steps, 1 - buf):
            cp.start()

    for cp in step_copies(b, i, buf):
        cp.wait()

    for j in range(group):
        cols = slice(j * T, (j + 1) * T)
        first, onehot_t, n_extra, jrow = _first_windows(s0_ref, slot_ref, cols, b, i * group + j, nt, cap, win)
        acc_ref[...] = _dot_tn(onehot_t, stage[buf, j])

        @pl.when(n_extra > 0)
        def _(first=first, cols=cols, jrow=jrow):
            for e in range(E):
                a, end, _, _ = first[e]

                def extra(w, carry, e=e, a=a):
                    r0w, pw = _slot_hits(slot_ref, cols, e, a, w, cap, win, jrow)
                    cps = window_copies(b * stride + off + r0w, extra_stage, e, sem.at[2])
                    for cp in cps:
                        cp.start()
                    for cp in cps:
                        cp.wait()
                    acc_ref[...] += _dot_tn(jnp.where(pw, 1.0, 0.0).astype(BF16), extra_stage[...])
                    return carry

                lax.fori_loop(1, _n_windows(a, end, cap, win), extra, 0)

        out = x_ref[0, cols, :] + g2_ref[0] * acc_ref[...]
        if final_norm:
            ms = jnp.mean(out * out, axis=-1, keepdims=True)
            out = out * lax.rsqrt(ms + EPS) * fin_ref[...]
        o_ref[0, cols, :] = out


def _combine(s0, slot_t, x, g2, y, cap, win, stride, off, final_g=None):
    B, L, D = x.shape
    E = N_EXPERTS
    T = min(MOE_TILE, L)
    nt = L // T
    group = _tile_group(nt)
    grid_spec = pltpu.PrefetchScalarGridSpec(
        num_scalar_prefetch=1,
        grid=(B, nt // group),
        in_specs=[pl.BlockSpec((1, E, group * T), lambda b, i, s: (b, 0, i)),
                  pl.BlockSpec((1, group * T, D), lambda b, i, s: (b, i, 0)),
                  pl.BlockSpec((1, 1, D), lambda b, i, s: (b, 0, 0)),
                  pl.BlockSpec((1, D), lambda b, i, s: (0, 0)),
                  pl.BlockSpec(memory_space=pl.ANY)],
        out_specs=pl.BlockSpec((1, group * T, D), lambda b, i, s: (b, i, 0)),
        scratch_shapes=[pltpu.VMEM((2, group, E * win, D), BF16), pltpu.VMEM((win, D), BF16),
                        pltpu.VMEM((T, D), F32), pltpu.SemaphoreType.DMA((3,))],
    )
    return pl.pallas_call(
        functools.partial(_combine_kernel, cap=cap, win=win, nt=nt, stride=stride, off=off, group=group,
                          final_norm=final_g is not None),
        grid_spec=grid_spec,
        out_shape=jax.ShapeDtypeStruct((B, L, D), F32),
        compiler_params=_cparams(("arbitrary", "arbitrary")),
        name="moe_combine",
    )(s0, slot_t, x, g2, (jnp.ones((D,), F32) if final_g is None else final_g).reshape(1, D), y)


def _expert_row_tile(m):
    for t in range(min(m, EXPERT_ROW_TILE_MAX) // SLOT_ALIGN * SLOT_ALIGN, 0, -SLOT_ALIGN):
        if m % t == 0:
            return t
    raise ValueError(f"no expert row tile for {m} rows")


def _moe(streams, w1, w3, w2, layer, final_g=None):
    E = N_EXPERTS
    B = streams[0][0].shape[0]
    plans, stride = [], 0
    for x, g2, h2, aff_t in streams:
        L = h2.shape[1]
        cap = CAPACITY_FACTOR * L // E
        plans.append((cap, min(MOE_WIN, cap), L // min(MOE_TILE, L), stride))
        stride += cap
    routed, prev = [], None
    for (x, g2, h2, aff_t), (cap, win, nt, off) in zip(streams, plans):
        slot_t, s0_pad = _select(aff_t, cap)
        s0 = s0_pad[:, :, :nt + 1].reshape(-1)
        prev = _dispatch(s0, h2, slot_t, aff_t, cap, win, stride, off, prev)
        routed.append((s0, slot_t))
    xe, ge = prev
    y = _experts(xe, w1, w3, w2, layer, ge, tm=_expert_row_tile(B * stride))
    outs = []
    for k, ((x, g2, h2, aff_t), (cap, win, nt, off), (s0, slot_t)) in enumerate(zip(streams, plans, routed)):
        outs.append(_combine(s0, slot_t, x, g2, y, cap, win, stride, off, final_g if k == 0 else None))
    return outs


def _rope_tables(n):
    half = HEAD_DIM // 2
    quarter = half // 2
    inv = ROPE_BASE ** (-jnp.arange(quarter, dtype=F32) / quarter)
    t = jnp.arange(n)
    row = (t // GRID_W).astype(F32)
    col = (t % GRID_W).astype(F32)
    lane = np.arange(128)
    within = lane % HEAD_DIM
    use_col = jnp.asarray((within >= half)[None, :])
    freq = inv[jnp.asarray(within % quarter)][None, :]
    ang = jnp.where(use_col, col[:, None], row[:, None]) * freq
    sign = jnp.asarray(np.where((lane % half) < quarter, -1.0, 1.0).astype(np.float32))[None, :]
    return jnp.cos(ang), jnp.sin(ang) * sign


def kernel(x, c, ctx, c_ctx, mod_w, mod_b, norm1_g, norm2_g, ab_w_in, ab_w_out, sgu_w, sgu_b, hgrn_lb_logits,
           hgrn_norm_g, attn_w_qkv, attn_w_out, attn_sink, router_w, expert_w1, expert_w3, expert_w2, final_g):
    B, N, D = x.shape
    Lc = ctx.shape[1]
    depth = mod_w.shape[0]

    rows = ((B + 1 + 7) // 8) * 8
    cvec = jnp.zeros((rows, D), F32).at[:B].set(c).at[B].set(c_ctx)
    mod = _mod_vectors(cvec, mod_w, mod_b)

    def mods(l):
        lat = [mod[l, :B, k * D:(k + 1) * D].reshape(B, 1, D) for k in range(6)]
        cx = [jnp.broadcast_to(mod[l, B, k * D:(k + 1) * D].reshape(1, 1, D), (B, 1, D)) for k in range(6)]
        return lat, cx

    lb_all = jnp.cumsum(jax.nn.softmax(hgrn_lb_logits.astype(F32), axis=0), axis=0)
    lmat_f = jnp.asarray(_hgrn_level_matrix(False), BF16)
    lmat_r = jnp.asarray(_hgrn_level_matrix(True), BF16)

    x_lat, x_ctx = x, ctx
    for l in range(depth):
        last = l == depth - 1
        (sh1, sc1, g1, sh2, sc2, g2), (csh1, csc1, cg1, csh2, csc2, cg2) = mods(l)
        wr_t = router_w[l].T
        if l % 2 == 0:
            e = l // 2
            w_in = ab_w_in[e].astype(BF16)
            w_out = ab_w_out[e].astype(BF16)
            sw = sgu_w[e].astype(BF16)
            sb_t = sgu_b[e].T
            hg = hgrn_norm_g[e].reshape(1, HGRN_DIM)
            s0 = jnp.zeros((2, B, HGRN_HEADS, HGRN_KDIM, HGRN_KDIM), F32)

            def even(xs, shift, scale, gate1, shift2, scale2, s_in):
                ya, z = _proj_sgu(xs, norm1_g[l], shift, scale, w_in, sw, sb_t)
                o_f, o_b, s_fin = _hgrn(z, lb_all[e], s_in, lmat_f, lmat_r)
                xn, h2, aff = _even_post(ya, o_f, o_b, z, hg, w_out, xs, gate1, norm2_g[l], shift2, scale2, wr_t)
                return xn, h2, aff, s_fin

            x_ctx, h2c, affc, s_ctx = even(x_ctx, csh1, csc1, cg1, csh2, csc2, s0)
            x_lat, h2l, affl, _ = even(x_lat, sh1, sc1, g1, sh2, sc2, s_ctx)
        else:
            if not last:
                raise NotImplementedError("context update after an attention layer")
            o_idx = l // 2
            w_qkv = attn_w_qkv[o_idx].astype(BF16)
            w_out = attn_w_out[o_idx].astype(BF16)
            cos, sin = _rope_tables(N)
            zc = _proj(x_ctx, norm1_g[l], csh1, csc1, w_qkv[:, ATTN_Q_DIM:])
            k_ctx = zc[..., :ATTN_KV_DIM].astype(BF16)
            v_ctx = zc[..., ATTN_KV_DIM:].astype(BF16)
            q, k, v = _qkv_lat(x_lat, norm1_g[l], sh1, sc1, w_qkv, cos, sin)
            o = _attention(q, k, v, k_ctx, v_ctx, attn_sink[o_idx])
            x_lat, h2l, affl = _odd_post(o, w_out, x_lat, g1, norm2_g[l], sh2, sc2, wr_t)
        streams = [(x_lat, g2, h2l, affl)]
        if not last:
            streams.append((x_ctx, cg2, h2c, affc))
        outs = _moe(streams, expert_w1, expert_w3, expert_w2, l, final_g if last else None)
        x_lat = outs[0]
        if not last:
            x_ctx = outs[1]
    return x_lat
```

```python
import functools

import numpy as np
import jax
import jax.numpy as jnp
from jax import lax
from jax.experimental import pallas as pl
from jax.experimental.pallas import tpu as pltpu

F32 = jnp.float32
BF16 = jnp.bfloat16

D_MODEL = 1024
GRID_W = 64
EPS = 1e-6
SGU_GROUPS = 4
SGU_GROUP_DIM = 128
SGU_DIM = SGU_GROUPS * SGU_GROUP_DIM
SGU_CHUNK = 128
HGRN_HEADS = 4
HGRN_KDIM = 128
HGRN_DIM = HGRN_HEADS * HGRN_KDIM
HGRN_CHUNK = 64
HGRN_LEVELS = 6
HGRN_COARSE = 3
HGRN_MXU_LEVELS = 5
HGRN_TBLK = 512
AB_IN = 2 * SGU_DIM + 5 * HGRN_DIM
ATTN_HEADS = 16
ATTN_KV_HEADS = 4
ATTN_GROUP = ATTN_HEADS // ATTN_KV_HEADS
HEAD_DIM = 64
ATTN_Q_DIM = ATTN_HEADS * HEAD_DIM
ATTN_KV_DIM = ATTN_KV_HEADS * HEAD_DIM
WINDOW = 128
ATTN_BLOCK = 128
ATTN_STEP_BLOCKS = 4
ROPE_BASE = 10000.0
N_EXPERTS = 16
CAPACITY_FACTOR = 2
EXPERT_FF = 2048
NEG_BIG = -1e30
LOG2_E = 1.4426950408889634

VMEM_LIMIT_BYTES = 56 * 1024 * 1024


ROW_TILE = 512
ROW_TILE_WIDE = 1024


def _row_tile(n_rows, limit=ROW_TILE):
    return min(limit, n_rows)


def _cparams(sem):
    return pltpu.CompilerParams(dimension_semantics=sem, vmem_limit_bytes=VMEM_LIMIT_BYTES)


def _dot(a, b):
    return jnp.dot(a, b, preferred_element_type=F32)


def _dot_nt(a, b):
    return lax.dot_general(a, b, (((1,), (1,)), ((), ())), preferred_element_type=F32)


def _dot_tn(a, b):
    return lax.dot_general(a, b, (((0,), (0,)), ((), ())), preferred_element_type=F32)


def _split2(a):
    hi = a.astype(BF16)
    lo = (a - hi.astype(F32)).astype(BF16)
    return hi, lo


def _dot_f32x3(a, b):
    a_hi, a_lo = _split2(a)
    b_hi, b_lo = _split2(b)
    return _dot(a_hi, b_hi) + (_dot(a_hi, b_lo) + _dot(a_lo, b_hi))


def _silu(x):
    return x * jax.nn.sigmoid(x)


def _gelu(x):
    return 0.5 * x * (1.0 + lax.erf(x * np.float32(1.0 / np.sqrt(2.0))))


def _rms_modulate(x, g, shift, scale):
    ms = jnp.mean(x * x, axis=-1, keepdims=True)
    return (x * lax.rsqrt(ms + EPS) * g) * (1.0 + scale) + shift


def _mod_kernel(c_ref, w_ref, b_ref, o_ref):
    s = _silu(c_ref[...])
    o_ref[0] = _dot_f32x3(s, w_ref[0]) + b_ref[0]


def _mod_vectors(cvec, mod_w, mod_b):
    depth, d, n6 = mod_w.shape
    rows = cvec.shape[0]
    tn = 1536
    return pl.pallas_call(
        _mod_kernel,
        grid=(depth, n6 // tn),
        in_specs=[
            pl.BlockSpec((rows, d), lambda l, j: (0, 0)),
            pl.BlockSpec((1, d, tn), lambda l, j: (l, 0, j)),
            pl.BlockSpec((1, 1, tn), lambda l, j: (l, 0, j)),
        ],
        out_specs=pl.BlockSpec((1, rows, tn), lambda l, j: (l, 0, j)),
        out_shape=jax.ShapeDtypeStruct((depth, rows, n6), F32),
        compiler_params=_cparams(("arbitrary", "arbitrary")),
        name="mod_vectors",
    )(cvec, mod_w, mod_b.reshape(depth, 1, n6))


def _proj_kernel(x_ref, g_ref, sh_ref, sc_ref, w_ref, z_ref):
    h = _rms_modulate(x_ref[0], g_ref[...], sh_ref[0], sc_ref[0])
    z_ref[0] = _dot(h.astype(BF16), w_ref[...])


def _row_spec(tm, d):
    return pl.BlockSpec((1, tm, d), lambda b, i: (b, i, 0))


def _vec_spec(d):
    return pl.BlockSpec((1, 1, d), lambda b, i: (b, 0, 0))


def _const_spec(shape):
    nd = len(shape)
    return pl.BlockSpec(shape, lambda b, i: (0,) * nd)


def _proj(x, g, shift, scale, w):
    B, L, D = x.shape
    tm = _row_tile(L)
    n = w.shape[1]
    return pl.pallas_call(
        _proj_kernel,
        grid=(B, L // tm),
        in_specs=[_row_spec(tm, D), _const_spec((1, D)), _vec_spec(D), _vec_spec(D), _const_spec((D, n))],
        out_specs=_row_spec(tm, n),
        out_shape=jax.ShapeDtypeStruct((B, L, n), F32),
        compiler_params=_cparams(("parallel", "parallel")),
        name="norm_mod_proj",
    )(x, g.reshape(1, D), shift, scale, w)


def _sgu_rows(zu, zv, w_ref, b_ref, o_ref, rows):
    u = _gelu(zu)
    v = _gelu(zv)
    for g in range(SGU_GROUPS):
        cols = slice(g * SGU_GROUP_DIM, (g + 1) * SGU_GROUP_DIM)
        vg = v[:, cols]
        mu = jnp.mean(vg, axis=-1, keepdims=True)
        vc = vg - mu
        var = jnp.mean(vc * vc, axis=-1, keepdims=True)
        vn = (vc * lax.rsqrt(var + EPS)).astype(BF16)
        s = _dot(w_ref[g], vn) + b_ref[:, g:g + 1]
        o_ref[0, rows, cols] = (u[:, cols] * s).astype(BF16)


def _proj_sgu_kernel(x_ref, g_ref, sh_ref, sc_ref, w_ref, sw_ref, sb_ref, ya_ref, zr_ref, *, tm):
    h = _rms_modulate(x_ref[0], g_ref[...], sh_ref[0], sc_ref[0]).astype(BF16)
    z_uv = _dot(h, w_ref[:, :2 * SGU_DIM])
    zr_ref[0] = _dot(h, w_ref[:, 2 * SGU_DIM:])
    for ch in range(tm // SGU_CHUNK):
        rows = slice(ch * SGU_CHUNK, (ch + 1) * SGU_CHUNK)
        _sgu_rows(z_uv[rows, :SGU_DIM], z_uv[rows, SGU_DIM:], sw_ref, sb_ref, ya_ref, rows)


def _proj_sgu(x, g, shift, scale, w, sgu_w_bf16, sgu_b_t):
    B, L, D = x.shape
    tm = _row_tile(L)
    n_rest = w.shape[1] - 2 * SGU_DIM
    return pl.pallas_call(
        functools.partial(_proj_sgu_kernel, tm=tm),
        grid=(B, L // tm),
        in_specs=[_row_spec(tm, D), _const_spec((1, D)), _vec_spec(D), _vec_spec(D), _const_spec((D, w.shape[1])),
                  _const_spec((SGU_GROUPS, SGU_CHUNK, SGU_CHUNK)), _const_spec((SGU_CHUNK, SGU_GROUPS))],
        out_specs=[_row_spec(tm, SGU_DIM), _row_spec(tm, n_rest)],
        out_shape=[jax.ShapeDtypeStruct((B, L, SGU_DIM), BF16), jax.ShapeDtypeStruct((B, L, n_rest), F32)],
        compiler_params=_cparams(("parallel", "parallel")),
        name="norm_mod_proj_sgu",
    )(x, g.reshape(1, D), shift, scale, w, sgu_w_bf16, sgu_b_t)


def _hgrn_level_matrix(reverse):
    C = HGRN_CHUNK
    t = np.arange(C)[:, None]
    s = np.arange(C)[None, :]
    blocks = []
    blocks.append(((s <= t) if not reverse else (s >= t)).astype(np.float32))
    for l in range(HGRN_COARSE, HGRN_MXU_LEVELS):
        m = C >> (l + 1)
        base = (t // (2 * m)) * (2 * m)
        if not reverse:
            r = base + m - 1
            qrow = (t % (2 * m)) >= m
            mq = (s > r) & (s <= t)
            mk = (s > t) & (s <= r)
        else:
            r = base + m
            qrow = (t % (2 * m)) < m
            mq = (s >= t) & (s < r)
            mk = (s >= r) & (s < t)
        blocks.append(np.where(qrow, mq, mk).astype(np.float32))
    return np.concatenate(blocks, axis=0)


def _hgrn_masks(reverse):
    C = HGRN_CHUNK
    t = lax.broadcasted_iota(jnp.int32, (C, C), 0)
    s = lax.broadcasted_iota(jnp.int32, (C, C), 1)
    r = lax.broadcasted_iota(jnp.int32, (C, 1), 0)
    masks, qrows = [], []
    for l in range(HGRN_LEVELS):
        m = C >> (l + 1)
        same = (t // (2 * m)) == (s // (2 * m))
        if not reverse:
            mk = same & ((t % (2 * m)) >= m) & ((s % (2 * m)) < m)
            qr = (r % (2 * m)) >= m
        else:
            mk = same & ((t % (2 * m)) < m) & ((s % (2 * m)) >= m)
            qr = (r % (2 * m)) < m
        masks.append(mk)
        qrows.append(qr)
    return masks, qrows


def _hgrn_chunk(zq, zf, zi, lb, st, lmat, masks, qrows, reverse):
    C = HGRN_CHUNK
    f = lb + (1.0 - lb) * jax.nn.sigmoid(zf)
    g = jnp.log(f)
    kk = 1.0 - f
    q = _silu(zq)
    vb = zi.astype(BF16)
    g_hi, g_lo = _split2(g)
    d2 = _dot(lmat, jnp.concatenate([g_hi, g_lo], axis=1))
    d = d2[:, :HGRN_KDIM] + d2[:, HGRN_KDIM:]
    b = d[0:C]
    b_end = b[0:1] if reverse else b[C - 1:C]
    dec = jnp.exp(b_end)
    qb = (q * jnp.exp(b)).astype(BF16)
    kb = (kk * jnp.exp(b_end - b)).astype(BF16)
    o = _dot_nt(qb, st.astype(BF16))
    o = o + jnp.sum(q * kk, axis=1, keepdims=True) * zi
    shift = C - 1 if reverse else 1
    near = jnp.sum(q * f * pltpu.roll(kk, shift, 0), axis=1, keepdims=True)
    o = o + jnp.where(qrows[HGRN_LEVELS - 1], near, 0.0) * pltpu.roll(zi, shift, 0)
    a = None
    for l in range(HGRN_MXU_LEVELS):
        if l < HGRN_COARSE:
            m = C >> (l + 1)
            r = m if reverse else m - 1
            ref = jnp.broadcast_to(b.reshape(C // (2 * m), 2 * m, HGRN_KDIM)[:, r:r + 1, :],
                                   (C // (2 * m), 2 * m, HGRN_KDIM)).reshape(C, HGRN_KDIM)
            dl = jnp.where(qrows[l], b - ref, ref - b)
        else:
            dl = d[(1 + l - HGRN_COARSE) * C:(2 + l - HGRN_COARSE) * C]
        mix = (jnp.where(qrows[l], q, kk) * jnp.exp(dl)).astype(BF16)
        p = jnp.where(masks[l], _dot_nt(mix, mix), 0.0)
        a = p if a is None else a + p
    o = o + _dot(a.astype(BF16), vb)
    st_new = st * dec + _dot_tn(vb, kb)
    return o, st_new


def _hgrn_kernel(zqf_ref, zff_ref, zif_ref, zqb_ref, zfb_ref, zib_ref, lb_ref, s0_ref, lf_ref, lr_ref,
                 of_ref, ob_ref, sfin_ref, st_ref, *, tb):
    i = pl.program_id(1)
    nb = pl.num_programs(1)

    @pl.when(i == 0)
    def _():
        st_ref[...] = s0_ref[:, 0]

    C = HGRN_CHUNK
    nch = tb // C
    for d in range(2):
        reverse = d == 1
        masks, qrows = _hgrn_masks(reverse)
        lmat = (lr_ref if reverse else lf_ref)[...]
        zq_ref, zf_ref, zi_ref, o_ref = ((zqb_ref, zfb_ref, zib_ref, ob_ref) if reverse
                                         else (zqf_ref, zff_ref, zif_ref, of_ref))
        for h in range(HGRN_HEADS):
            cols = slice(h * HGRN_KDIM, (h + 1) * HGRN_KDIM)
            lb = lb_ref[d:d + 1, cols]
            st = st_ref[d, h]
            order = range(nch - 1, -1, -1) if reverse else range(nch)
            for c in order:
                rows = slice(c * C, (c + 1) * C)
                o, st = _hgrn_chunk(zq_ref[0, rows, cols], zf_ref[0, rows, cols], zi_ref[0, rows, cols],
                                    lb, st, lmat, masks, qrows, reverse)
                o_ref[0, rows, cols] = o.astype(BF16)
            st_ref[d, h] = st

    @pl.when(i == nb - 1)
    def _():
        sfin_ref[:, 0] = st_ref[...]


def _hgrn(z, lb, s0, lmat_f, lmat_r):
    B, L, _ = z.shape
    tb = min(HGRN_TBLK, L)
    nb = L // tb
    nl = lmat_f.shape[0]

    def zspec(col, rev):
        if rev:
            return pl.BlockSpec((1, tb, HGRN_DIM), lambda b, i: (b, nb - 1 - i, col))
        return pl.BlockSpec((1, tb, HGRN_DIM), lambda b, i: (b, i, col))

    st_spec = pl.BlockSpec((2, 1, HGRN_HEADS, HGRN_KDIM, HGRN_KDIM), lambda b, i: (0, b, 0, 0, 0))
    return pl.pallas_call(
        functools.partial(_hgrn_kernel, tb=tb),
        grid=(B, nb),
        in_specs=[zspec(0, False), zspec(1, False), zspec(3, False),
                  zspec(0, True), zspec(2, True), zspec(3, True),
                  _const_spec((2, HGRN_DIM)), st_spec,
                  _const_spec((nl, HGRN_CHUNK)), _const_spec((nl, HGRN_CHUNK))],
        out_specs=[pl.BlockSpec((1, tb, HGRN_DIM), lambda b, i: (b, i, 0)),
                   pl.BlockSpec((1, tb, HGRN_DIM), lambda b, i: (b, nb - 1 - i, 0)),
                   st_spec],
        out_shape=[jax.ShapeDtypeStruct((B, L, HGRN_DIM), BF16),
                   jax.ShapeDtypeStruct((B, L, HGRN_DIM), BF16),
                   jax.ShapeDtypeStruct(s0.shape, F32)],
        scratch_shapes=[pltpu.VMEM((2, HGRN_HEADS, HGRN_KDIM, HGRN_KDIM), F32)],
        compiler_params=_cparams(("parallel", "arbitrary")),
        name="hgrn2_scan",
    )(z, z, z, z, z, z, lb, s0, lmat_f, lmat_r)


def _residual_router_tail(x, y, g1, n2g, sh2, sc2, wr, xo_ref, h2_ref, aff_ref):
    xn = x + g1 * y
    xo_ref[0] = xn
    h2 = _rms_modulate(xn, n2g, sh2, sc2)
    h2_ref[0] = h2.astype(BF16)
    w_hi, w_lo = _split2(wr)
    h_hi, h_lo = _split2(h2)
    logits = _dot_nt(w_hi, h_hi) + (_dot_nt(w_hi, h_lo) + _dot_nt(w_lo, h_hi))
    mx = jnp.max(logits, axis=0, keepdims=True)
    p = jnp.exp(logits - mx)
    aff_ref[0] = p / jnp.sum(p, axis=0, keepdims=True)


def _even_post_kernel(ya_ref, of_ref, ob_ref, zg_ref, hg_ref, w_ref, x_ref, g1_ref, n2g_ref, sh2_ref,
                      sc2_ref, wr_ref, xo_ref, h2_ref, aff_ref):
    o = of_ref[0].astype(F32) + ob_ref[0].astype(F32)
    zg = zg_ref[0]
    parts = []
    for h in range(HGRN_HEADS):
        cols = slice(h * HGRN_KDIM, (h + 1) * HGRN_KDIM)
        oh = o[:, cols]
        ms = jnp.mean(oh * oh, axis=-1, keepdims=True)
        parts.append(oh * lax.rsqrt(ms + EPS) * hg_ref[:, cols])
    yb = (jnp.concatenate(parts, axis=1) * _silu(zg)).astype(BF16)
    y = _dot(ya_ref[0], w_ref[:SGU_DIM, :]) + _dot(yb, w_ref[SGU_DIM:, :])
    _residual_router_tail(x_ref[0], y, g1_ref[0], n2g_ref[...], sh2_ref[0], sc2_ref[0], wr_ref[...],
                          xo_ref, h2_ref, aff_ref)


def _odd_post_kernel(o_ref, w_ref, x_ref, g1_ref, n2g_ref, sh2_ref, sc2_ref, wr_ref, xo_ref, h2_ref,
                     aff_ref):
    y = _dot(o_ref[0], w_ref[...])
    _residual_router_tail(x_ref[0], y, g1_ref[0], n2g_ref[...], sh2_ref[0], sc2_ref[0], wr_ref[...],
                          xo_ref, h2_ref, aff_ref)


def _post_out_specs(B, L, tm):
    D = D_MODEL
    specs = [_row_spec(tm, D), _row_spec(tm, D), pl.BlockSpec((1, N_EXPERTS, tm), lambda b, i: (b, 0, i))]
    shapes = [jax.ShapeDtypeStruct((B, L, D), F32), jax.ShapeDtypeStruct((B, L, D), BF16),
              jax.ShapeDtypeStruct((B, N_EXPERTS, L), F32)]
    return specs, shapes


def _even_post(ya, o_f, o_b, z, hg, w_out, x, g1, n2g, sh2, sc2, wr):
    B, L, D = x.shape
    tm = _row_tile(L, ROW_TILE_WIDE)
    specs, shapes = _post_out_specs(B, L, tm)
    return pl.pallas_call(
        _even_post_kernel,
        grid=(B, L // tm),
        in_specs=[_row_spec(tm, SGU_DIM), _row_spec(tm, HGRN_DIM), _row_spec(tm, HGRN_DIM),
                  pl.BlockSpec((1, tm, HGRN_DIM), lambda b, i: (b, i, 4)),
                  _const_spec((1, HGRN_DIM)), _const_spec((D, D)), _row_spec(tm, D), _vec_spec(D),
                  _const_spec((1, D)), _vec_spec(D), _vec_spec(D), _const_spec((N_EXPERTS, D))],
        out_specs=specs,
        out_shape=shapes,
        compiler_params=_cparams(("parallel", "parallel")),
        name="even_post_router",
    )(ya, o_f, o_b, z, hg, w_out, x, g1, n2g.reshape(1, D), sh2, sc2, wr)


def _odd_post(o, w_out, x, g1, n2g, sh2, sc2, wr):
    B, L, D = x.shape
    tm = _row_tile(L, ROW_TILE_WIDE)
    specs, shapes = _post_out_specs(B, L, tm)
    return pl.pallas_call(
        _odd_post_kernel,
        grid=(B, L // tm),
        in_specs=[_row_spec(tm, ATTN_Q_DIM), _const_spec((ATTN_Q_DIM, D)), _row_spec(tm, D), _vec_spec(D),
                  _const_spec((1, D)), _vec_spec(D), _vec_spec(D), _const_spec((N_EXPERTS, D))],
        out_specs=specs,
        out_shape=shapes,
        compiler_params=_cparams(("parallel", "parallel")),
        name="odd_post_router",
    )(o, w_out, x, g1, n2g.reshape(1, D), sh2, sc2, wr)


def _rope(x, cos, sin_signed):
    lane = lax.broadcasted_iota(jnp.int32, (x.shape[0], 128), 1)
    first = (lane % 32) < 16
    outs = []
    for j in range(x.shape[1] // 128):
        xb = x[:, j * 128:(j + 1) * 128]
        sw = jnp.where(first, pltpu.roll(xb, 112, 1), pltpu.roll(xb, 16, 1))
        outs.append(xb * cos + sw * sin_signed)
    return jnp.concatenate(outs, axis=1)


def _qkv_lat_kernel(x_ref, g_ref, sh_ref, sc_ref, w_ref, cos_ref, sin_ref, q_ref, k_ref, v_ref):
    h = _rms_modulate(x_ref[0], g_ref[...], sh_ref[0], sc_ref[0])
    z = _dot(h.astype(BF16), w_ref[...])
    cos = cos_ref[...]
    sin = sin_ref[...]
    scale = np.float32(HEAD_DIM ** -0.5 * LOG2_E)
    q_ref[0] = (_rope(z[:, :ATTN_Q_DIM], cos, sin) * scale).astype(BF16)
    k_ref[0] = _rope(z[:, ATTN_Q_DIM:ATTN_Q_DIM + ATTN_KV_DIM], cos, sin).astype(BF16)
    v_ref[0] = z[:, ATTN_Q_DIM + ATTN_KV_DIM:].astype(BF16)


def _qkv_lat(x, g, shift, scale, w, cos, sin):
    B, L, D = x.shape
    tm = _row_tile(L, ROW_TILE_WIDE)
    n = w.shape[1]
    tab = pl.BlockSpec((tm, 128), lambda b, i: (i, 0))
    return pl.pallas_call(
        _qkv_lat_kernel,
        grid=(B, L // tm),
        in_specs=[_row_spec(tm, D), _const_spec((1, D)), _vec_spec(D), _vec_spec(D), _const_spec((D, n)), tab, tab],
        out_specs=[_row_spec(tm, ATTN_Q_DIM), _row_spec(tm, ATTN_KV_DIM), _row_spec(tm, ATTN_KV_DIM)],
        out_shape=[jax.ShapeDtypeStruct((B, L, ATTN_Q_DIM), BF16),
                   jax.ShapeDtypeStruct((B, L, ATTN_KV_DIM), BF16),
                   jax.ShapeDtypeStruct((B, L, ATTN_KV_DIM), BF16)],
        compiler_params=_cparams(("parallel", "parallel")),
        name="norm_qkv_rope",
    )(x, g.reshape(1, D), shift, scale, w, cos, sin)


def _attn_block(q_ref, k_parts, v_parts, kx_ref, vx_ref, sink_ref, o_ref, rows, bi, n_tokens):
    blk = ATTN_BLOCK
    n_loc = 3 * blk
    nk = n_loc + kx_ref.shape[1]
    n_pairs = ATTN_GROUP // 2
    nq = n_pairs * blk
    pad = 16
    s_idx = lax.broadcasted_iota(jnp.int32, (n_loc, nq), 0)
    r_idx = lax.broadcasted_iota(jnp.int32, (n_loc, nq), 1) % blk
    j = bi * blk - blk + s_idx
    valid = (jnp.abs(s_idx - r_idx - blk) <= WINDOW) & (j >= 0) & (j < n_tokens)
    lane = lax.broadcasted_iota(jnp.int32, (nk, 128), 1)
    qcol = lax.broadcasted_iota(jnp.int32, (1, nq), 1)
    prow = lax.broadcasted_iota(jnp.int32, (pad, nq), 0)
    er = lax.broadcasted_iota(jnp.int32, (2 * nk + pad, 128), 0)
    el = lax.broadcasted_iota(jnp.int32, (2 * nk + pad, 128), 1)
    lo_rows = (er < nk) | (er == 2 * nk)
    hi_rows = ((er >= nk) & (er < 2 * nk)) | (er == 2 * nk + 1)
    ind = jnp.where((lo_rows & (el < HEAD_DIM)) | (hi_rows & (el >= HEAD_DIM)), 1.0, 0.0).astype(BF16)
    zpad = jnp.zeros((pad, 128), BF16)
    scores, values = [], []
    for kvh in range(ATTN_KV_HEADS):
        cols = slice((kvh // 2) * 128, (kvh // 2 + 1) * 128)
        own = (lane < HEAD_DIM) if kvh % 2 == 0 else (lane >= HEAD_DIM)
        k_all = jnp.concatenate([r[0, rs, cols] for r, rs in k_parts] + [kx_ref[0, :, cols]], axis=0)
        v_all = jnp.concatenate([r[0, rs, cols] for r, rs in v_parts] + [vx_ref[0, :, cols]], axis=0)
        k_own = jnp.where(own, k_all, jnp.zeros_like(k_all))
        v_own = jnp.where(own, v_all, jnp.zeros_like(v_all))
        k_oth = pltpu.roll(k_own, HEAD_DIM, 1)
        v_oth = pltpu.roll(v_own, HEAD_DIM, 1)
        if kvh % 2 == 0:
            k_cat = jnp.concatenate([k_own, k_oth], axis=0)
            v_cat = jnp.concatenate([v_own, v_oth, zpad], axis=0)
        else:
            k_cat = jnp.concatenate([k_oth, k_own], axis=0)
            v_cat = jnp.concatenate([v_oth, v_own, zpad], axis=0)
        q2 = jnp.concatenate([q_ref[0, rows, (kvh * n_pairs + p) * 128:(kvh * n_pairs + p + 1) * 128]
                              for p in range(n_pairs)], axis=0)
        scores.append(_dot_nt(k_cat, q2))
        values.append(jnp.concatenate([v_cat, ind], axis=1))
    probs = []
    for kvh in range(ATTN_KV_HEADS):
        st = scores[kvh]
        ps, es = [], []
        for sub in range(2):
            sh = st[sub * nk:(sub + 1) * nk]
            sh = jnp.concatenate([jnp.where(valid, sh[:n_loc], NEG_BIG), sh[n_loc:]], axis=0)
            hd0 = kvh * ATTN_GROUP + sub
            sink = sink_ref[:, hd0:hd0 + 1]
            for p in range(1, n_pairs):
                sink = jnp.where(qcol < p * blk, sink, sink_ref[:, hd0 + 2 * p:hd0 + 2 * p + 1])
            m = jnp.maximum(jnp.max(sh, axis=0, keepdims=True), sink)
            ps.append(jnp.exp2(sh - m).astype(BF16))
            es.append(jnp.exp2(sink - m))
        sink_rows = jnp.where(prow == 0, es[0], jnp.where(prow == 1, es[1], 0.0)).astype(BF16)
        probs.append(jnp.concatenate(ps + [sink_rows], axis=0))
    for kvh in range(ATTN_KV_HEADS):
        od = _dot_tn(probs[kvh], values[kvh])
        o = (od[:, :128] / od[:, 128:]).astype(BF16)
        for p in range(n_pairs):
            o_ref[0, rows, (kvh * n_pairs + p) * 128:(kvh * n_pairs + p + 1) * 128] = o[p * blk:(p + 1) * blk]


def _attn_kernel(q_ref, kp_ref, kc_ref, kn_ref, vp_ref, vc_ref, vn_ref, kx_ref, vx_ref, sink_ref, o_ref,
                 *, n_tokens, nblk):
    blk = ATTN_BLOCK
    first = nblk * pl.program_id(1)

    def part(prev_ref, cur_ref, next_ref, idx):
        if idx < 0:
            return prev_ref, slice(0, blk)
        if idx >= nblk:
            return next_ref, slice(0, blk)
        return cur_ref, slice(idx * blk, (idx + 1) * blk)

    for j in range(nblk):
        _attn_block(q_ref, [part(kp_ref, kc_ref, kn_ref, j + d) for d in (-1, 0, 1)],
                    [part(vp_ref, vc_ref, vn_ref, j + d) for d in (-1, 0, 1)],
                    kx_ref, vx_ref, sink_ref, o_ref, slice(j * blk, (j + 1) * blk), first + j, n_tokens)


def _attention(q, k, v, k_ctx, v_ctx, sink):
    B, N, _ = q.shape
    Lc = k_ctx.shape[1]
    blk = ATTN_BLOCK
    nb = N // blk
    nblk = ATTN_STEP_BLOCKS if nb % ATTN_STEP_BLOCKS == 0 else 1

    def edge_spec(off):
        return pl.BlockSpec((1, blk, ATTN_KV_DIM), lambda b, i: (b, jnp.clip(nblk * i + off, 0, nb - 1), 0))

    cur_spec = pl.BlockSpec((1, nblk * blk, ATTN_KV_DIM), lambda b, i: (b, i, 0))
    ctx_spec = pl.BlockSpec((1, Lc, ATTN_KV_DIM), lambda b, i: (b, 0, 0))
    return pl.pallas_call(
        functools.partial(_attn_kernel, n_tokens=N, nblk=nblk),
        grid=(B, nb // nblk),
        in_specs=[_row_spec(nblk * blk, ATTN_Q_DIM), edge_spec(-1), cur_spec, edge_spec(nblk),
                  edge_spec(-1), cur_spec, edge_spec(nblk), ctx_spec, ctx_spec,
                  _const_spec((1, ATTN_HEADS))],
        out_specs=_row_spec(nblk * blk, ATTN_Q_DIM),
        out_shape=jax.ShapeDtypeStruct((B, N, ATTN_Q_DIM), BF16),
        compiler_params=_cparams(("parallel", "parallel")),
        name="window_attention",
    )(q, k, k, k, v, v, v, k_ctx, v_ctx, (sink * np.float32(LOG2_E)).reshape(1, ATTN_HEADS))


MOE_TILE = 256
MOE_WIN = 64
MOE_TILES_PER_STEP = 2
SLOT_ALIGN = 16
EXPERT_FF_CHUNK = 256


def _select_kernel(aff_ref, slot_ref, s0_ref, *, cap):
    x = aff_ref[...]
    E, L = x.shape
    T = min(MOE_TILE, L)

    def count(mask):
        return jnp.sum(jnp.where(mask, 1.0, 0.0), axis=1, keepdims=True)

    def as_f32(bits):
        return lax.bitcast_convert_type(bits, F32)

    def body(_, c):
        lo, hi = c
        mid = lo + lax.shift_right_logical(hi - lo, 1)
        ok = count(x >= as_f32(mid)) >= cap
        return jnp.where(ok, mid, lo), jnp.where(ok, hi, mid)

    lo0 = jnp.zeros((E, 1), jnp.int32)
    hi0 = jnp.full((E, 1), 0x7F800000, jnp.int32)
    lo, hi = lax.fori_loop(0, 31, body, (lo0, hi0))
    gt = x >= as_f32(hi)
    eq = (x >= as_f32(lo)) & jnp.logical_not(gt)
    need = cap - count(gt)
    ti = lax.broadcasted_iota(jnp.int32, (T, T), 0)
    tj = lax.broadcasted_iota(jnp.int32, (T, T), 1)
    upper = jnp.where(ti < tj, 1.0, 0.0).astype(BF16)

    def excl_cumsum(mask):
        carry = jnp.zeros((E, 1), F32)
        outs, starts = [], []
        for t in range(L // T):
            v = jnp.where(mask[:, t * T:(t + 1) * T], 1.0, 0.0)
            starts.append(carry)
            outs.append(_dot(v.astype(BF16), upper) + carry)
            carry = carry + jnp.sum(v, axis=1, keepdims=True)
        starts.append(carry)
        return jnp.concatenate(outs, axis=1), starts

    eq_rank, _ = excl_cumsum(eq)
    sel = gt | (eq & (eq_rank < need))
    slot, starts = excl_cumsum(sel)
    slot_ref[...] = jnp.where(sel, slot, -1.0).astype(jnp.int32)
    lane = lax.broadcasted_iota(jnp.int32, (E, 128), 1)
    s0 = jnp.zeros((E, 128), F32)
    for t, st in enumerate(starts):
        s0 = jnp.where(lane == t, st, s0)
    s0_ref[...] = s0.astype(jnp.int32)


def _select(aff_t, cap):
    B, E, L = aff_t.shape
    assert L // min(MOE_TILE, L) + 1 <= 128
    rows = B * E
    slot, s0 = pl.pallas_call(
        functools.partial(_select_kernel, cap=cap),
        grid=(1,),
        in_specs=[pl.BlockSpec((rows, L), lambda i: (0, 0))],
        out_specs=[pl.BlockSpec((rows, L), lambda i: (0, 0)), pl.BlockSpec((rows, 128), lambda i: (0, 0))],
        out_shape=[jax.ShapeDtypeStruct((rows, L), jnp.int32), jax.ShapeDtypeStruct((rows, 128), jnp.int32)],
        compiler_params=_cparams(("arbitrary",)),
        name="moe_select",
    )(aff_t.reshape(rows, L))
    return slot.reshape(B, E, L), s0.reshape(B, E, 128)


def _tile_group(nt):
    return MOE_TILES_PER_STEP if nt % MOE_TILES_PER_STEP == 0 else 1


def _window_start(s0_ref, b, e, i, nt, cap, win):
    base = (b * N_EXPERTS + e) * (nt + 1) + i
    a = (s0_ref[base] // SLOT_ALIGN) * SLOT_ALIGN
    return a, s0_ref[base + 1], jnp.minimum(a, cap - win)


def _n_windows(a, end, cap, win):
    return jnp.where(a + win >= cap, 1, jnp.maximum((end - a + win - 1) // win, 1))


def _slot_hits(slot_ref, cols, e, a, w, cap, win, jrow):
    lo_s = a + w * win
    r0 = jnp.minimum(lo_s, cap - win)
    se = slot_ref[0, e:e + 1, cols]
    return r0, ((se - r0) == jrow) & (se >= lo_s)


def _first_windows(s0_ref, slot_ref, cols, b, i, nt, cap, win):
    T = cols.stop - cols.start
    jrow = lax.broadcasted_iota(jnp.int32, (win, T), 0)
    first = []
    n_extra = 0
    for e in range(N_EXPERTS):
        a, end, _ = _window_start(s0_ref, b, e, i, nt, cap, win)
        r0, p = _slot_hits(slot_ref, cols, e, a, 0, cap, win, jrow)
        first.append((a, end, r0, p))
        n_extra = n_extra + (_n_windows(a, end, cap, win) - 1)
    onehot_t = jnp.where(jnp.concatenate([f[3] for f in first], axis=0), 1.0, 0.0).astype(BF16)
    return first, onehot_t, n_extra, jrow


def _dispatch_kernel(s0_ref, h_ref, slot_ref, aff_ref, *rest, cap, win, nt, row_off, group):
    xe_ref, ge_ref = rest[-2:]
    b = pl.program_id(0)
    i = pl.program_id(1)

    @pl.when(i == 0)
    def _():
        xe_ref[...] = jnp.zeros_like(xe_ref)
        ge_ref[...] = jnp.zeros_like(ge_ref)

    T = h_ref.shape[1] // group
    for j in range(group):
        cols = slice(j * T, (j + 1) * T)
        h = h_ref[0, cols, :]

        def place(e, r0, p, rows, cols=cols):
            r0 = pl.multiple_of(row_off + r0, SLOT_ALIGN)
            gate = jnp.sum(jnp.where(p, aff_ref[0, e:e + 1, cols], 0.0), axis=1, keepdims=True)
            xe_ref[e, pl.ds(r0, win), :] += rows.astype(BF16)
            ge_ref[e, pl.ds(r0, win), :] += gate

        first, onehot_t, n_extra, jrow = _first_windows(s0_ref, slot_ref, cols, b, i * group + j, nt, cap, win)
        rows = _dot(onehot_t, h)
        for e in range(N_EXPERTS):
            _, _, r0, p = first[e]
            place(e, r0, p, rows[e * win:(e + 1) * win])

        @pl.when(n_extra > 0)
        def _(first=first, cols=cols, h=h, jrow=jrow, place=place):
            for e in range(N_EXPERTS):
                a, end, _, _ = first[e]

                def extra(w, carry, e=e, a=a):
                    r0w, pw = _slot_hits(slot_ref, cols, e, a, w, cap, win, jrow)
                    place(e, r0w, pw, _dot(jnp.where(pw, 1.0, 0.0).astype(BF16), h))
                    return carry

                lax.fori_loop(1, _n_windows(a, end, cap, win), extra, 0)


def _dispatch(s0, h2, slot_t, aff_t, cap, win, stride, off, prev=None):
    B, L, D = h2.shape
    E = N_EXPERTS
    T = min(MOE_TILE, L)
    nt = L // T
    group = _tile_group(nt)
    if prev is None:
        blk_rows, row_off, per_sample, blk0 = stride, off, 1, 0
    else:
        assert stride % cap == 0 and off % cap == 0
        blk_rows, row_off, per_sample, blk0 = cap, 0, stride // cap, off // cap
    in_specs = [pl.BlockSpec((1, group * T, D), lambda b, i, s: (b, i, 0)),
                pl.BlockSpec((1, E, group * T), lambda b, i, s: (b, 0, i)),
                pl.BlockSpec((1, E, group * T), lambda b, i, s: (b, 0, i))]
    args = [s0, h2, slot_t, aff_t]
    aliases = {}
    if prev is not None:
        in_specs += [pl.BlockSpec(memory_space=pl.ANY), pl.BlockSpec(memory_space=pl.ANY)]
        aliases = {len(args): 0, len(args) + 1: 1}
        args += list(prev)
    grid_spec = pltpu.PrefetchScalarGridSpec(
        num_scalar_prefetch=1,
        grid=(B, nt // group),
        in_specs=in_specs,
        out_specs=[pl.BlockSpec((E, blk_rows, D), lambda b, i, s: (0, blk0 + b * per_sample, 0)),
                   pl.BlockSpec((E, blk_rows, 1), lambda b, i, s: (0, blk0 + b * per_sample, 0))],
    )
    return pl.pallas_call(
        functools.partial(_dispatch_kernel, cap=cap, win=win, nt=nt, row_off=row_off, group=group),
        grid_spec=grid_spec,
        out_shape=[jax.ShapeDtypeStruct((E, B * stride, D), BF16), jax.ShapeDtypeStruct((E, B * stride, 1), F32)],
        input_output_aliases=aliases,
        compiler_params=_cparams(("parallel", "arbitrary")),
        name="moe_dispatch",
    )(*args)


def _expert_kernel(x_ref, w1_ref, w3_ref, w2_ref, gate_ref, y_ref, wb1, wb3, wb2, *, n_experts, n_chunks):
    e = pl.program_id(0)
    m = pl.program_id(1)

    @pl.when(e < n_experts)
    def _():
        slot = e % 2
        wb1[slot, m] = w1_ref[0, 0].astype(BF16)
        wb3[slot, m] = w3_ref[0, 0].astype(BF16)
        wb2[slot, m] = w2_ref[0, 0].astype(BF16)

    @pl.when(e > 0)
    def _():
        slot = (e + 1) % 2
        x = x_ref[0]
        acts = []
        for c in range(n_chunks):
            a = _dot(x, wb1[slot, c])
            u = _dot(x, wb3[slot, c])
            acts.append((_silu(a) * u).astype(BF16))
        w2 = wb2[slot].reshape(n_chunks * wb2.shape[2], wb2.shape[3])
        y = _dot(jnp.concatenate(acts, axis=1), w2)
        y_ref[0] = (y * gate_ref[0]).astype(BF16)


def _experts(xe, w1, w3, w2, layer, gate):
    E, M, D = xe.shape
    F = w1.shape[3]
    tf = EXPERT_FF_CHUNK
    n_chunks = F // tf
    assert M % (n_chunks * SLOT_ALIGN) == 0
    tm = M // n_chunks
    cur = lambda e: jnp.minimum(e, E - 1)
    prv = lambda e: jnp.maximum(e - 1, 0)
    row = lambda e, m: jnp.where(e == 0, 0, m)
    return pl.pallas_call(
        functools.partial(_expert_kernel, n_experts=E, n_chunks=n_chunks),
        grid=(E + 1, n_chunks),
        in_specs=[pl.BlockSpec((1, tm, D), lambda e, m: (prv(e), row(e, m), 0)),
                  pl.BlockSpec((1, 1, D, tf), lambda e, m: (layer, cur(e), 0, m)),
                  pl.BlockSpec((1, 1, D, tf), lambda e, m: (layer, cur(e), 0, m)),
                  pl.BlockSpec((1, 1, tf, D), lambda e, m: (layer, cur(e), m, 0)),
                  pl.BlockSpec((1, tm, 1), lambda e, m: (prv(e), row(e, m), 0))],
        out_specs=pl.BlockSpec((1, tm, D), lambda e, m: (prv(e), row(e, m), 0)),
        out_shape=jax.ShapeDtypeStruct((E, M, D), BF16),
        scratch_shapes=[pltpu.VMEM((2, n_chunks, D, tf), BF16), pltpu.VMEM((2, n_chunks, D, tf), BF16),
                        pltpu.VMEM((2, n_chunks, tf, D), BF16)],
        compiler_params=_cparams(("arbitrary", "arbitrary")),
        name="expert_ffn",
    )(xe, w1, w3, w2, gate)


def _combine_kernel(s0_ref, slot_ref, x_ref, g2_ref, fin_ref, y_hbm, o_ref, stage, extra_stage, acc_ref,
                    sem, *, cap, win, nt, stride, off, group, final_norm):
    E = N_EXPERTS
    b = pl.program_id(0)
    i = pl.program_id(1)
    steps = nt // group
    step = b * steps + i
    total = pl.num_programs(0) * steps
    buf = step % 2
    T = x_ref.shape[1] // group

    def window_copies(src_row, dst, e, s):
        src_row = pl.multiple_of(src_row, SLOT_ALIGN)
        return (pltpu.make_async_copy(y_hbm.at[e, pl.ds(src_row, win)], dst, s),)

    def step_copies(bb, ii, bf):
        cps = []
        for j in range(group):
            for e in range(E):
                _, _, r0 = _window_start(s0_ref, bb, e, ii * group + j, nt, cap, win)
                cps.extend(window_copies(bb * stride + off + r0, stage.at[bf, j, pl.ds(e * win, win)], e, sem.at[bf]))
        return cps

    @pl.when(step == 0)
    def _():
        for cp in step_copies(b, i, 0):
            cp.start()

    nxt = step + 1

    @pl.when(nxt < total)
    def _():
        for cp in step_copies(nxt // steps, nxt % steps, 1 - buf):
            cp.start()

    for cp in step_copies(b, i, buf):
        cp.wait()

    for j in range(group):
        cols = slice(j * T, (j + 1) * T)
        first, onehot_t, n_extra, jrow = _first_windows(s0_ref, slot_ref, cols, b, i * group + j, nt, cap, win)
        acc_ref[...] = _dot_tn(onehot_t, stage[buf, j])

        @pl.when(n_extra > 0)
        def _(first=first, cols=cols, jrow=jrow):
            for e in range(E):
                a, end, _, _ = first[e]

                def extra(w, carry, e=e, a=a):
                    r0w, pw = _slot_hits(slot_ref, cols, e, a, w, cap, win, jrow)
                    cps = window_copies(b * stride + off + r0w, extra_stage, e, sem.at[2])
                    for cp in cps:
                        cp.start()
                    for cp in cps:
                        cp.wait()
                    acc_ref[...] += _dot_tn(jnp.where(pw, 1.0, 0.0).astype(BF16), extra_stage[...])
                    return carry

                lax.fori_loop(1, _n_windows(a, end, cap, win), extra, 0)

        out = x_ref[0, cols, :] + g2_ref[0] * acc_ref[...]
        if final_norm:
            ms = jnp.mean(out * out, axis=-1, keepdims=True)
            out = out * lax.rsqrt(ms + EPS) * fin_ref[...]
        o_ref[0, cols, :] = out


def _combine(s0, slot_t, x, g2, y, cap, win, stride, off, final_g=None):
    B, L, D = x.shape
    E = N_EXPERTS
    T = min(MOE_TILE, L)
    nt = L // T
    group = _tile_group(nt)
    grid_spec = pltpu.PrefetchScalarGridSpec(
        num_scalar_prefetch=1,
        grid=(B, nt // group),
        in_specs=[pl.BlockSpec((1, E, group * T), lambda b, i, s: (b, 0, i)),
                  pl.BlockSpec((1, group * T, D), lambda b, i, s: (b, i, 0)),
                  pl.BlockSpec((1, 1, D), lambda b, i, s: (b, 0, 0)),
                  pl.BlockSpec((1, D), lambda b, i, s: (0, 0)),
                  pl.BlockSpec(memory_space=pl.ANY)],
        out_specs=pl.BlockSpec((1, group * T, D), lambda b, i, s: (b, i, 0)),
        scratch_shapes=[pltpu.VMEM((2, group, E * win, D), BF16), pltpu.VMEM((win, D), BF16),
                        pltpu.VMEM((T, D), F32), pltpu.SemaphoreType.DMA((3,))],
    )
    return pl.pallas_call(
        functools.partial(_combine_kernel, cap=cap, win=win, nt=nt, stride=stride, off=off, group=group,
                          final_norm=final_g is not None),
        grid_spec=grid_spec,
        out_shape=jax.ShapeDtypeStruct((B, L, D), F32),
        compiler_params=_cparams(("arbitrary", "arbitrary")),
        name="moe_combine",
    )(s0, slot_t, x, g2, (jnp.ones((D,), F32) if final_g is None else final_g).reshape(1, D), y)


def _moe(streams, w1, w3, w2, layer, final_g=None):
    E = N_EXPERTS
    B = streams[0][0].shape[0]
    plans, stride = [], 0
    for x, g2, h2, aff_t in streams:
        L = h2.shape[1]
        cap = CAPACITY_FACTOR * L // E
        plans.append((cap, min(MOE_WIN, cap), L // min(MOE_TILE, L), stride))
        stride += cap
    routed, prev = [], None
    for (x, g2, h2, aff_t), (cap, win, nt, off) in zip(streams, plans):
        slot_t, s0_pad = _select(aff_t, cap)
        s0 = s0_pad[:, :, :nt + 1].reshape(-1)
        prev = _dispatch(s0, h2, slot_t, aff_t, cap, win, stride, off, prev)
        routed.append((s0, slot_t))
    xe, ge = prev
    y = _experts(xe, w1, w3, w2, layer, ge)
    outs = []
    for k, ((x, g2, h2, aff_t), (cap, win, nt, off), (s0, slot_t)) in enumerate(zip(streams, plans, routed)):
        outs.append(_combine(s0, slot_t, x, g2, y, cap, win, stride, off, final_g if k == 0 else None))
    return outs


def _rope_tables(n):
    half = HEAD_DIM // 2
    quarter = half // 2
    inv = ROPE_BASE ** (-jnp.arange(quarter, dtype=F32) / quarter)
    t = jnp.arange(n)
    row = (t // GRID_W).astype(F32)
    col = (t % GRID_W).astype(F32)
    lane = np.arange(128)
    within = lane % HEAD_DIM
    use_col = jnp.asarray((within >= half)[None, :])
    freq = inv[jnp.asarray(within % quarter)][None, :]
    ang = jnp.where(use_col, col[:, None], row[:, None]) * freq
    sign = jnp.asarray(np.where((lane % half) < quarter, -1.0, 1.0).astype(np.float32))[None, :]
    return jnp.cos(ang), jnp.sin(ang) * sign


def kernel(x, c, ctx, c_ctx, mod_w, mod_b, norm1_g, norm2_g, ab_w_in, ab_w_out, sgu_w, sgu_b, hgrn_lb_logits,
           hgrn_norm_g, attn_w_qkv, attn_w_out, attn_sink, router_w, expert_w1, expert_w3, expert_w2, final_g):
    B, N, D = x.shape
    Lc = ctx.shape[1]
    depth = mod_w.shape[0]

    rows = ((B + 1 + 7) // 8) * 8
    cvec = jnp.zeros((rows, D), F32).at[:B].set(c).at[B].set(c_ctx)
    mod = _mod_vectors(cvec, mod_w, mod_b)

    def mods(l):
        lat = [mod[l, :B, k * D:(k + 1) * D].reshape(B, 1, D) for k in range(6)]
        cx = [jnp.broadcast_to(mod[l, B, k * D:(k + 1) * D].reshape(1, 1, D), (B, 1, D)) for k in range(6)]
        return lat, cx

    lb_all = jnp.cumsum(jax.nn.softmax(hgrn_lb_logits.astype(F32), axis=0), axis=0)
    lmat_f = jnp.asarray(_hgrn_level_matrix(False), BF16)
    lmat_r = jnp.asarray(_hgrn_level_matrix(True), BF16)

    x_lat, x_ctx = x, ctx
    for l in range(depth):
        last = l == depth - 1
        (sh1, sc1, g1, sh2, sc2, g2), (csh1, csc1, cg1, csh2, csc2, cg2) = mods(l)
        wr_t = router_w[l].T
        if l % 2 == 0:
            e = l // 2
            w_in = ab_w_in[e].astype(BF16)
            w_out = ab_w_out[e].astype(BF16)
            sw = sgu_w[e].astype(BF16)
            sb_t = sgu_b[e].T
            hg = hgrn_norm_g[e].reshape(1, HGRN_DIM)
            s0 = jnp.zeros((2, B, HGRN_HEADS, HGRN_KDIM, HGRN_KDIM), F32)

            def even(xs, shift, scale, gate1, shift2, scale2, s_in):
                ya, z = _proj_sgu(xs, norm1_g[l], shift, scale, w_in, sw, sb_t)
                o_f, o_b, s_fin = _hgrn(z, lb_all[e], s_in, lmat_f, lmat_r)
                xn, h2, aff = _even_post(ya, o_f, o_b, z, hg, w_out, xs, gate1, norm2_g[l], shift2, scale2, wr_t)
                return xn, h2, aff, s_fin

            x_ctx, h2c, affc, s_ctx = even(x_ctx, csh1, csc1, cg1, csh2, csc2, s0)
            x_lat, h2l, affl, _ = even(x_lat, sh1, sc1, g1, sh2, sc2, s_ctx)
        else:
            if not last:
                raise NotImplementedError("context update after an attention layer")
            o_idx = l // 2
            w_qkv = attn_w_qkv[o_idx].astype(BF16)
            w_out = attn_w_out[o_idx].astype(BF16)
            cos, sin = _rope_tables(N)
            zc = _proj(x_ctx, norm1_g[l], csh1, csc1, w_qkv[:, ATTN_Q_DIM:])
            k_ctx = zc[..., :ATTN_KV_DIM].astype(BF16)
            v_ctx = zc[..., ATTN_KV_DIM:].astype(BF16)
            q, k, v = _qkv_lat(x_lat, norm1_g[l], sh1, sc1, w_qkv, cos, sin)
            o = _attention(q, k, v, k_ctx, v_ctx, attn_sink[o_idx])
            x_lat, h2l, affl = _odd_post(o, w_out, x_lat, g1, norm2_g[l], sh2, sc2, wr_t)
        streams = [(x_lat, g2, h2l, affl)]
        if not last:
            streams.append((x_ctx, cg2, h2c, affc))
        outs = _moe(streams, expert_w1, expert_w3, expert_w2, l, final_g if last else None)
        x_lat = outs[0]
        if not last:
            x_ctx = outs[1]
    return x_lat
```

```python
import functools

import numpy as np
import jax
import jax.numpy as jnp
from jax import lax
from jax.experimental import pallas as pl
from jax.experimental.pallas import tpu as pltpu

F32 = jnp.float32
BF16 = jnp.bfloat16

D_MODEL = 1024
GRID_W = 64
EPS = 1e-6
SGU_GROUPS = 4
SGU_GROUP_DIM = 128
SGU_DIM = SGU_GROUPS * SGU_GROUP_DIM
SGU_CHUNK = 128
HGRN_HEADS = 4
HGRN_KDIM = 128
HGRN_DIM = HGRN_HEADS * HGRN_KDIM
HGRN_CHUNK = 64
HGRN_LEVELS = 6
HGRN_COARSE = 3
HGRN_MXU_LEVELS = 5
HGRN_TBLK = 512
AB_IN = 2 * SGU_DIM + 5 * HGRN_DIM
ATTN_HEADS = 16
ATTN_KV_HEADS = 4
ATTN_GROUP = ATTN_HEADS // ATTN_KV_HEADS
HEAD_DIM = 64
ATTN_Q_DIM = ATTN_HEADS * HEAD_DIM
ATTN_KV_DIM = ATTN_KV_HEADS * HEAD_DIM
WINDOW = 128
ATTN_BLOCK = 128
ATTN_STEP_BLOCKS = 4
ROPE_BASE = 10000.0
N_EXPERTS = 16
CAPACITY_FACTOR = 2
EXPERT_FF = 2048
NEG_BIG = -1e30
LOG2_E = 1.4426950408889634

VMEM_LIMIT_BYTES = 56 * 1024 * 1024


ROW_TILE = 512
ROW_TILE_WIDE = 1024


def _row_tile(n_rows, limit=ROW_TILE):
    return min(limit, n_rows)


def _cparams(sem):
    return pltpu.CompilerParams(dimension_semantics=sem, vmem_limit_bytes=VMEM_LIMIT_BYTES)


def _dot(a, b):
    return jnp.dot(a, b, preferred_element_type=F32)


def _dot_nt(a, b):
    return lax.dot_general(a, b, (((1,), (1,)), ((), ())), preferred_element_type=F32)


def _dot_tn(a, b):
    return lax.dot_general(a, b, (((0,), (0,)), ((), ())), preferred_element_type=F32)


def _split2(a):
    hi = a.astype(BF16)
    lo = (a - hi.astype(F32)).astype(BF16)
    return hi, lo


def _dot_f32x3(a, b):
    a_hi, a_lo = _split2(a)
    b_hi, b_lo = _split2(b)
    return _dot(a_hi, b_hi) + (_dot(a_hi, b_lo) + _dot(a_lo, b_hi))


def _silu(x):
    return x * jax.nn.sigmoid(x)


def _gelu(x):
    return 0.5 * x * (1.0 + lax.erf(x * np.float32(1.0 / np.sqrt(2.0))))


def _rms_modulate(x, g, shift, scale):
    ms = jnp.mean(x * x, axis=-1, keepdims=True)
    return (x * lax.rsqrt(ms + EPS) * g) * (1.0 + scale) + shift


def _mod_kernel(c_ref, w_ref, b_ref, o_ref):
    s = _silu(c_ref[...])
    o_ref[0] = _dot_f32x3(s, w_ref[0]) + b_ref[0]


def _mod_vectors(cvec, mod_w, mod_b):
    depth, d, n6 = mod_w.shape
    rows = cvec.shape[0]
    tn = 1536
    return pl.pallas_call(
        _mod_kernel,
        grid=(depth, n6 // tn),
        in_specs=[
            pl.BlockSpec((rows, d), lambda l, j: (0, 0)),
            pl.BlockSpec((1, d, tn), lambda l, j: (l, 0, j)),
            pl.BlockSpec((1, 1, tn), lambda l, j: (l, 0, j)),
        ],
        out_specs=pl.BlockSpec((1, rows, tn), lambda l, j: (l, 0, j)),
        out_shape=jax.ShapeDtypeStruct((depth, rows, n6), F32),
        compiler_params=_cparams(("arbitrary", "arbitrary")),
        name="mod_vectors",
    )(cvec, mod_w, mod_b.reshape(depth, 1, n6))


def _proj_kernel(x_ref, g_ref, sh_ref, sc_ref, w_ref, z_ref):
    h = _rms_modulate(x_ref[0], g_ref[...], sh_ref[0], sc_ref[0])
    z_ref[0] = _dot(h.astype(BF16), w_ref[...])


def _row_spec(tm, d):
    return pl.BlockSpec((1, tm, d), lambda b, i: (b, i, 0))


def _vec_spec(d):
    return pl.BlockSpec((1, 1, d), lambda b, i: (b, 0, 0))


def _const_spec(shape):
    nd = len(shape)
    return pl.BlockSpec(shape, lambda b, i: (0,) * nd)


def _proj(x, g, shift, scale, w):
    B, L, D = x.shape
    tm = _row_tile(L)
    n = w.shape[1]
    return pl.pallas_call(
        _proj_kernel,
        grid=(B, L // tm),
        in_specs=[_row_spec(tm, D), _const_spec((1, D)), _vec_spec(D), _vec_spec(D), _const_spec((D, n))],
        out_specs=_row_spec(tm, n),
        out_shape=jax.ShapeDtypeStruct((B, L, n), F32),
        compiler_params=_cparams(("parallel", "parallel")),
        name="norm_mod_proj",
    )(x, g.reshape(1, D), shift, scale, w)


def _sgu_rows(zu, zv, w_ref, b_ref, o_ref, rows):
    u = _gelu(zu)
    v = _gelu(zv)
    for g in range(SGU_GROUPS):
        cols = slice(g * SGU_GROUP_DIM, (g + 1) * SGU_GROUP_DIM)
        vg = v[:, cols]
        mu = jnp.mean(vg, axis=-1, keepdims=True)
        vc = vg - mu
        var = jnp.mean(vc * vc, axis=-1, keepdims=True)
        vn = (vc * lax.rsqrt(var + EPS)).astype(BF16)
        s = _dot(w_ref[g], vn) + b_ref[:, g:g + 1]
        o_ref[0, rows, cols] = (u[:, cols] * s).astype(BF16)


def _proj_sgu_kernel(x_ref, g_ref, sh_ref, sc_ref, w_ref, sw_ref, sb_ref, ya_ref, zr_ref, *, tm):
    h = _rms_modulate(x_ref[0], g_ref[...], sh_ref[0], sc_ref[0]).astype(BF16)
    z_uv = _dot(h, w_ref[:, :2 * SGU_DIM])
    zr_ref[0] = _dot(h, w_ref[:, 2 * SGU_DIM:])
    for ch in range(tm // SGU_CHUNK):
        rows = slice(ch * SGU_CHUNK, (ch + 1) * SGU_CHUNK)
        _sgu_rows(z_uv[rows, :SGU_DIM], z_uv[rows, SGU_DIM:], sw_ref, sb_ref, ya_ref, rows)


def _proj_sgu(x, g, shift, scale, w, sgu_w_bf16, sgu_b_t):
    B, L, D = x.shape
    tm = _row_tile(L)
    n_rest = w.shape[1] - 2 * SGU_DIM
    return pl.pallas_call(
        functools.partial(_proj_sgu_kernel, tm=tm),
        grid=(B, L // tm),
        in_specs=[_row_spec(tm, D), _const_spec((1, D)), _vec_spec(D), _vec_spec(D), _const_spec((D, w.shape[1])),
                  _const_spec((SGU_GROUPS, SGU_CHUNK, SGU_CHUNK)), _const_spec((SGU_CHUNK, SGU_GROUPS))],
        out_specs=[_row_spec(tm, SGU_DIM), _row_spec(tm, n_rest)],
        out_shape=[jax.ShapeDtypeStruct((B, L, SGU_DIM), BF16), jax.ShapeDtypeStruct((B, L, n_rest), F32)],
        compiler_params=_cparams(("parallel", "parallel")),
        name="norm_mod_proj_sgu",
    )(x, g.reshape(1, D), shift, scale, w, sgu_w_bf16, sgu_b_t)


def _hgrn_level_matrix(reverse):
    C = HGRN_CHUNK
    t = np.arange(C)[:, None]
    s = np.arange(C)[None, :]
    blocks = []
    blocks.append(((s <= t) if not reverse else (s >= t)).astype(np.float32))
    for l in range(HGRN_COARSE, HGRN_MXU_LEVELS):
        m = C >> (l + 1)
        base = (t // (2 * m)) * (2 * m)
        if not reverse:
            r = base + m - 1
            qrow = (t % (2 * m)) >= m
            mq = (s > r) & (s <= t)
            mk = (s > t) & (s <= r)
        else:
            r = base + m
            qrow = (t % (2 * m)) < m
            mq = (s >= t) & (s < r)
            mk = (s >= r) & (s < t)
        blocks.append(np.where(qrow, mq, mk).astype(np.float32))
    return np.concatenate(blocks, axis=0)


def _hgrn_masks(reverse):
    C = HGRN_CHUNK
    t = lax.broadcasted_iota(jnp.int32, (C, C), 0)
    s = lax.broadcasted_iota(jnp.int32, (C, C), 1)
    r = lax.broadcasted_iota(jnp.int32, (C, 1), 0)
    masks, qrows = [], []
    for l in range(HGRN_LEVELS):
        m = C >> (l + 1)
        same = (t // (2 * m)) == (s // (2 * m))
        if not reverse:
            mk = same & ((t % (2 * m)) >= m) & ((s % (2 * m)) < m)
            qr = (r % (2 * m)) >= m
        else:
            mk = same & ((t % (2 * m)) < m) & ((s % (2 * m)) >= m)
            qr = (r % (2 * m)) < m
        masks.append(mk)
        qrows.append(qr)
    return masks, qrows


def _hgrn_chunk(zq, zf, zi, lb, st, lmat, masks, qrows, reverse):
    C = HGRN_CHUNK
    f = lb + (1.0 - lb) * jax.nn.sigmoid(zf)
    g = jnp.log(f)
    kk = 1.0 - f
    q = _silu(zq)
    vb = zi.astype(BF16)
    g_hi, g_lo = _split2(g)
    d2 = _dot(lmat, jnp.concatenate([g_hi, g_lo], axis=1))
    d = d2[:, :HGRN_KDIM] + d2[:, HGRN_KDIM:]
    b = d[0:C]
    b_end = b[0:1] if reverse else b[C - 1:C]
    dec = jnp.exp(b_end)
    qb = (q * jnp.exp(b)).astype(BF16)
    kb = (kk * jnp.exp(b_end - b)).astype(BF16)
    o = _dot_nt(qb, st.astype(BF16))
    o = o + jnp.sum(q * kk, axis=1, keepdims=True) * zi
    shift = C - 1 if reverse else 1
    near = jnp.sum(q * f * pltpu.roll(kk, shift, 0), axis=1, keepdims=True)
    o = o + jnp.where(qrows[HGRN_LEVELS - 1], near, 0.0) * pltpu.roll(zi, shift, 0)
    a = None
    for l in range(HGRN_MXU_LEVELS):
        if l < HGRN_COARSE:
            m = C >> (l + 1)
            r = m if reverse else m - 1
            ref = jnp.broadcast_to(b.reshape(C // (2 * m), 2 * m, HGRN_KDIM)[:, r:r + 1, :],
                                   (C // (2 * m), 2 * m, HGRN_KDIM)).reshape(C, HGRN_KDIM)
            dl = jnp.where(qrows[l], b - ref, ref - b)
        else:
            dl = d[(1 + l - HGRN_COARSE) * C:(2 + l - HGRN_COARSE) * C]
        mix = (jnp.where(qrows[l], q, kk) * jnp.exp(dl)).astype(BF16)
        p = jnp.where(masks[l], _dot_nt(mix, mix), 0.0)
        a = p if a is None else a + p
    o = o + _dot(a.astype(BF16), vb)
    st_new = st * dec + _dot_tn(vb, kb)
    return o, st_new


def _hgrn_kernel(zqf_ref, zff_ref, zif_ref, zqb_ref, zfb_ref, zib_ref, lb_ref, s0_ref, lf_ref, lr_ref,
                 of_ref, ob_ref, sfin_ref, st_ref, *, tb):
    i = pl.program_id(1)
    nb = pl.num_programs(1)

    @pl.when(i == 0)
    def _():
        st_ref[...] = s0_ref[:, 0]

    C = HGRN_CHUNK
    nch = tb // C
    for d in range(2):
        reverse = d == 1
        masks, qrows = _hgrn_masks(reverse)
        lmat = (lr_ref if reverse else lf_ref)[...]
        zq_ref, zf_ref, zi_ref, o_ref = ((zqb_ref, zfb_ref, zib_ref, ob_ref) if reverse
                                         else (zqf_ref, zff_ref, zif_ref, of_ref))
        for h in range(HGRN_HEADS):
            cols = slice(h * HGRN_KDIM, (h + 1) * HGRN_KDIM)
            lb = lb_ref[d:d + 1, cols]
            st = st_ref[d, h]
            order = range(nch - 1, -1, -1) if reverse else range(nch)
            for c in order:
                rows = slice(c * C, (c + 1) * C)
                o, st = _hgrn_chunk(zq_ref[0, rows, cols], zf_ref[0, rows, cols], zi_ref[0, rows, cols],
                                    lb, st, lmat, masks, qrows, reverse)
                o_ref[0, rows, cols] = o.astype(BF16)
            st_ref[d, h] = st

    @pl.when(i == nb - 1)
    def _():
        sfin_ref[:, 0] = st_ref[...]


def _hgrn(z, lb, s0, lmat_f, lmat_r):
    B, L, _ = z.shape
    tb = min(HGRN_TBLK, L)
    nb = L // tb
    nl = lmat_f.shape[0]

    def zspec(col, rev):
        if rev:
            return pl.BlockSpec((1, tb, HGRN_DIM), lambda b, i: (b, nb - 1 - i, col))
        return pl.BlockSpec((1, tb, HGRN_DIM), lambda b, i: (b, i, col))

    st_spec = pl.BlockSpec((2, 1, HGRN_HEADS, HGRN_KDIM, HGRN_KDIM), lambda b, i: (0, b, 0, 0, 0))
    return pl.pallas_call(
        functools.partial(_hgrn_kernel, tb=tb),
        grid=(B, nb),
        in_specs=[zspec(0, False), zspec(1, False), zspec(3, False),
                  zspec(0, True), zspec(2, True), zspec(3, True),
                  _const_spec((2, HGRN_DIM)), st_spec,
                  _const_spec((nl, HGRN_CHUNK)), _const_spec((nl, HGRN_CHUNK))],
        out_specs=[pl.BlockSpec((1, tb, HGRN_DIM), lambda b, i: (b, i, 0)),
                   pl.BlockSpec((1, tb, HGRN_DIM), lambda b, i: (b, nb - 1 - i, 0)),
                   st_spec],
        out_shape=[jax.ShapeDtypeStruct((B, L, HGRN_DIM), BF16),
                   jax.ShapeDtypeStruct((B, L, HGRN_DIM), BF16),
                   jax.ShapeDtypeStruct(s0.shape, F32)],
        scratch_shapes=[pltpu.VMEM((2, HGRN_HEADS, HGRN_KDIM, HGRN_KDIM), F32)],
        compiler_params=_cparams(("parallel", "arbitrary")),
        name="hgrn2_scan",
    )(z, z, z, z, z, z, lb, s0, lmat_f, lmat_r)


def _residual_router_tail(x, y, g1, n2g, sh2, sc2, wr, xo_ref, h2_ref, aff_ref):
    xn = x + g1 * y
    xo_ref[0] = xn
    h2 = _rms_modulate(xn, n2g, sh2, sc2)
    h2_ref[0] = h2.astype(BF16)
    w_hi, w_lo = _split2(wr)
    h_hi, h_lo = _split2(h2)
    logits = _dot_nt(w_hi, h_hi) + (_dot_nt(w_hi, h_lo) + _dot_nt(w_lo, h_hi))
    mx = jnp.max(logits, axis=0, keepdims=True)
    p = jnp.exp(logits - mx)
    aff_ref[0] = p / jnp.sum(p, axis=0, keepdims=True)


def _even_post_kernel(ya_ref, of_ref, ob_ref, zg_ref, hg_ref, w_ref, x_ref, g1_ref, n2g_ref, sh2_ref,
                      sc2_ref, wr_ref, xo_ref, h2_ref, aff_ref):
    o = of_ref[0].astype(F32) + ob_ref[0].astype(F32)
    zg = zg_ref[0]
    parts = []
    for h in range(HGRN_HEADS):
        cols = slice(h * HGRN_KDIM, (h + 1) * HGRN_KDIM)
        oh = o[:, cols]
        ms = jnp.mean(oh * oh, axis=-1, keepdims=True)
        parts.append(oh * lax.rsqrt(ms + EPS) * hg_ref[:, cols])
    yb = (jnp.concatenate(parts, axis=1) * _silu(zg)).astype(BF16)
    y = _dot(ya_ref[0], w_ref[:SGU_DIM, :]) + _dot(yb, w_ref[SGU_DIM:, :])
    _residual_router_tail(x_ref[0], y, g1_ref[0], n2g_ref[...], sh2_ref[0], sc2_ref[0], wr_ref[...],
                          xo_ref, h2_ref, aff_ref)


def _post_out_specs(B, L, tm):
    D = D_MODEL
    specs = [_row_spec(tm, D), _row_spec(tm, D), pl.BlockSpec((1, N_EXPERTS, tm), lambda b, i: (b, 0, i))]
    shapes = [jax.ShapeDtypeStruct((B, L, D), F32), jax.ShapeDtypeStruct((B, L, D), BF16),
              jax.ShapeDtypeStruct((B, N_EXPERTS, L), F32)]
    return specs, shapes


def _even_post(ya, o_f, o_b, z, hg, w_out, x, g1, n2g, sh2, sc2, wr):
    B, L, D = x.shape
    tm = _row_tile(L, ROW_TILE_WIDE)
    specs, shapes = _post_out_specs(B, L, tm)
    return pl.pallas_call(
        _even_post_kernel,
        grid=(B, L // tm),
        in_specs=[_row_spec(tm, SGU_DIM), _row_spec(tm, HGRN_DIM), _row_spec(tm, HGRN_DIM),
                  pl.BlockSpec((1, tm, HGRN_DIM), lambda b, i: (b, i, 4)),
                  _const_spec((1, HGRN_DIM)), _const_spec((D, D)), _row_spec(tm, D), _vec_spec(D),
                  _const_spec((1, D)), _vec_spec(D), _vec_spec(D), _const_spec((N_EXPERTS, D))],
        out_specs=specs,
        out_shape=shapes,
        compiler_params=_cparams(("parallel", "parallel")),
        name="even_post_router",
    )(ya, o_f, o_b, z, hg, w_out, x, g1, n2g.reshape(1, D), sh2, sc2, wr)


def _rope(x, cos, sin_signed):
    lane = lax.broadcasted_iota(jnp.int32, (x.shape[0], 128), 1)
    first = (lane % 32) < 16
    outs = []
    for j in range(x.shape[1] // 128):
        xb = x[:, j * 128:(j + 1) * 128]
        sw = jnp.where(first, pltpu.roll(xb, 112, 1), pltpu.roll(xb, 16, 1))
        outs.append(xb * cos + sw * sin_signed)
    return jnp.concatenate(outs, axis=1)


def _qkv_lat_kernel(x_ref, g_ref, sh_ref, sc_ref, w_ref, cos_ref, sin_ref, q_ref, k_ref, v_ref):
    h = _rms_modulate(x_ref[0], g_ref[...], sh_ref[0], sc_ref[0])
    z = _dot(h.astype(BF16), w_ref[...])
    cos = cos_ref[...]
    sin = sin_ref[...]
    scale = np.float32(HEAD_DIM ** -0.5 * LOG2_E)
    q_ref[0] = (_rope(z[:, :ATTN_Q_DIM], cos, sin) * scale).astype(BF16)
    k_ref[0] = _rope(z[:, ATTN_Q_DIM:ATTN_Q_DIM + ATTN_KV_DIM], cos, sin).astype(BF16)
    v_ref[0] = z[:, ATTN_Q_DIM + ATTN_KV_DIM:].astype(BF16)


def _qkv_lat(x, g, shift, scale, w, cos, sin):
    B, L, D = x.shape
    tm = _row_tile(L, ROW_TILE_WIDE)
    n = w.shape[1]
    tab = pl.BlockSpec((tm, 128), lambda b, i: (i, 0))
    return pl.pallas_call(
        _qkv_lat_kernel,
        grid=(B, L // tm),
        in_specs=[_row_spec(tm, D), _const_spec((1, D)), _vec_spec(D), _vec_spec(D), _const_spec((D, n)), tab, tab],
        out_specs=[_row_spec(tm, ATTN_Q_DIM), _row_spec(tm, ATTN_KV_DIM), _row_spec(tm, ATTN_KV_DIM)],
        out_shape=[jax.ShapeDtypeStruct((B, L, ATTN_Q_DIM), BF16),
                   jax.ShapeDtypeStruct((B, L, ATTN_KV_DIM), BF16),
                   jax.ShapeDtypeStruct((B, L, ATTN_KV_DIM), BF16)],
        compiler_params=_cparams(("parallel", "parallel")),
        name="norm_qkv_rope",
    )(x, g.reshape(1, D), shift, scale, w, cos, sin)


def _attn_block(q_ref, k_parts, v_parts, kx_ref, vx_ref, sink_ref, o_ref, rows, bi, n_tokens):
    blk = ATTN_BLOCK
    n_loc = 3 * blk
    nk = n_loc + kx_ref.shape[1]
    n_pairs = ATTN_GROUP // 2
    nq = n_pairs * blk
    pad = 16
    s_idx = lax.broadcasted_iota(jnp.int32, (n_loc, nq), 0)
    r_idx = lax.broadcasted_iota(jnp.int32, (n_loc, nq), 1) % blk
    j = bi * blk - blk + s_idx
    valid = (jnp.abs(s_idx - r_idx - blk) <= WINDOW) & (j >= 0) & (j < n_tokens)
    lane = lax.broadcasted_iota(jnp.int32, (nk, 128), 1)
    qcol = lax.broadcasted_iota(jnp.int32, (1, nq), 1)
    prow = lax.broadcasted_iota(jnp.int32, (pad, nq), 0)
    er = lax.broadcasted_iota(jnp.int32, (2 * nk + pad, 128), 0)
    el = lax.broadcasted_iota(jnp.int32, (2 * nk + pad, 128), 1)
    lo_rows = (er < nk) | (er == 2 * nk)
    hi_rows = ((er >= nk) & (er < 2 * nk)) | (er == 2 * nk + 1)
    ind = jnp.where((lo_rows & (el < HEAD_DIM)) | (hi_rows & (el >= HEAD_DIM)), 1.0, 0.0).astype(BF16)
    zpad = jnp.zeros((pad, 128), BF16)
    scores, values = [], []
    for kvh in range(ATTN_KV_HEADS):
        cols = slice((kvh // 2) * 128, (kvh // 2 + 1) * 128)
        own = (lane < HEAD_DIM) if kvh % 2 == 0 else (lane >= HEAD_DIM)
        k_all = jnp.concatenate([r[0, rs, cols] for r, rs in k_parts] + [kx_ref[0, :, cols]], axis=0)
        v_all = jnp.concatenate([r[0, rs, cols] for r, rs in v_parts] + [vx_ref[0, :, cols]], axis=0)
        k_own = jnp.where(own, k_all, jnp.zeros_like(k_all))
        v_own = jnp.where(own, v_all, jnp.zeros_like(v_all))
        k_oth = pltpu.roll(k_own, HEAD_DIM, 1)
        v_oth = pltpu.roll(v_own, HEAD_DIM, 1)
        if kvh % 2 == 0:
            k_cat = jnp.concatenate([k_own, k_oth], axis=0)
            v_cat = jnp.concatenate([v_own, v_oth, zpad], axis=0)
        else:
            k_cat = jnp.concatenate([k_oth, k_own], axis=0)
            v_cat = jnp.concatenate([v_oth, v_own, zpad], axis=0)
        q2 = jnp.concatenate([q_ref[0, rows, (kvh * n_pairs + p) * 128:(kvh * n_pairs + p + 1) * 128]
                              for p in range(n_pairs)], axis=0)
        scores.append(_dot_nt(k_cat, q2))
        values.append(jnp.concatenate([v_cat, ind], axis=1))
    probs = []
    for kvh in range(ATTN_KV_HEADS):
        st = scores[kvh]
        ps, es = [], []
        for sub in range(2):
            sh = st[sub * nk:(sub + 1) * nk]
            sh = jnp.concatenate([jnp.where(valid, sh[:n_loc], NEG_BIG), sh[n_loc:]], axis=0)
            hd0 = kvh * ATTN_GROUP + sub
            sink = sink_ref[:, hd0:hd0 + 1]
            for p in range(1, n_pairs):
                sink = jnp.where(qcol < p * blk, sink, sink_ref[:, hd0 + 2 * p:hd0 + 2 * p + 1])
            m = jnp.maximum(jnp.max(sh, axis=0, keepdims=True), sink)
            ps.append(jnp.exp2(sh - m).astype(BF16))
            es.append(jnp.exp2(sink - m))
        sink_rows = jnp.where(prow == 0, es[0], jnp.where(prow == 1, es[1], 0.0)).astype(BF16)
        probs.append(jnp.concatenate(ps + [sink_rows], axis=0))
    for kvh in range(ATTN_KV_HEADS):
        od = _dot_tn(probs[kvh], values[kvh])
        o = (od[:, :128] / od[:, 128:]).astype(BF16)
        for p in range(n_pairs):
            o_ref[0, rows, (kvh * n_pairs + p) * 128:(kvh * n_pairs + p + 1) * 128] = o[p * blk:(p + 1) * blk]


def _attn_kernel(q_ref, kp_ref, kc_ref, kn_ref, vp_ref, vc_ref, vn_ref, kx_ref, vx_ref, sink_ref,
                 w_ref, x_ref, g1_ref, n2g_ref, sh2_ref, sc2_ref, wr_ref, xo_ref, h2_ref, aff_ref, o_ref,
                 *, n_tokens, nblk):
    blk = ATTN_BLOCK
    first = nblk * pl.program_id(1)

    def part(prev_ref, cur_ref, next_ref, idx):
        if idx < 0:
            return prev_ref, slice(0, blk)
        if idx >= nblk:
            return next_ref, slice(0, blk)
        return cur_ref, slice(idx * blk, (idx + 1) * blk)

    for j in range(nblk):
        _attn_block(q_ref, [part(kp_ref, kc_ref, kn_ref, j + d) for d in (-1, 0, 1)],
                    [part(vp_ref, vc_ref, vn_ref, j + d) for d in (-1, 0, 1)],
                    kx_ref, vx_ref, sink_ref, o_ref, slice(j * blk, (j + 1) * blk), first + j, n_tokens)
    y = _dot(o_ref[0], w_ref[...])
    _residual_router_tail(x_ref[0], y, g1_ref[0], n2g_ref[...], sh2_ref[0], sc2_ref[0], wr_ref[...],
                          xo_ref, h2_ref, aff_ref)


def _attention_post(q, k, v, k_ctx, v_ctx, sink, w_out, x, g1, n2g, sh2, sc2, wr):
    B, N, _ = q.shape
    D = x.shape[2]
    Lc = k_ctx.shape[1]
    blk = ATTN_BLOCK
    nb = N // blk
    nblk = ATTN_STEP_BLOCKS if nb % ATTN_STEP_BLOCKS == 0 else 1

    def edge_spec(off):
        return pl.BlockSpec((1, blk, ATTN_KV_DIM), lambda b, i: (b, jnp.clip(nblk * i + off, 0, nb - 1), 0))

    cur_spec = pl.BlockSpec((1, nblk * blk, ATTN_KV_DIM), lambda b, i: (b, i, 0))
    ctx_spec = pl.BlockSpec((1, Lc, ATTN_KV_DIM), lambda b, i: (b, 0, 0))
    specs, shapes = _post_out_specs(B, N, nblk * blk)
    return pl.pallas_call(
        functools.partial(_attn_kernel, n_tokens=N, nblk=nblk),
        grid=(B, nb // nblk),
        in_specs=[_row_spec(nblk * blk, ATTN_Q_DIM), edge_spec(-1), cur_spec, edge_spec(nblk),
                  edge_spec(-1), cur_spec, edge_spec(nblk), ctx_spec, ctx_spec,
                  _const_spec((1, ATTN_HEADS)),
                  _const_spec((ATTN_Q_DIM, D)), _row_spec(nblk * blk, D), _vec_spec(D), _const_spec((1, D)),
                  _vec_spec(D), _vec_spec(D), _const_spec((N_EXPERTS, D))],
        out_specs=specs,
        out_shape=shapes,
        scratch_shapes=[pltpu.VMEM((1, nblk * blk, ATTN_Q_DIM), BF16)],
        compiler_params=_cparams(("parallel", "parallel")),
        name="window_attention_post_router",
    )(q, k, k, k, v, v, v, k_ctx, v_ctx, (sink * np.float32(LOG2_E)).reshape(1, ATTN_HEADS),
      w_out, x, g1, n2g.reshape(1, D), sh2, sc2, wr)


MOE_TILE = 256
MOE_WIN = 64
MOE_TILES_PER_STEP = 2
SLOT_ALIGN = 16
EXPERT_FF_CHUNK = 256


def _select_kernel(aff_ref, slot_ref, s0_ref, *, cap):
    x = aff_ref[...]
    E, L = x.shape
    T = min(MOE_TILE, L)

    def count(mask):
        return jnp.sum(jnp.where(mask, 1.0, 0.0), axis=1, keepdims=True)

    def as_f32(bits):
        return lax.bitcast_convert_type(bits, F32)

    def body(_, c):
        lo, hi = c
        mid = lo + lax.shift_right_logical(hi - lo, 1)
        ok = count(x >= as_f32(mid)) >= cap
        return jnp.where(ok, mid, lo), jnp.where(ok, hi, mid)

    lo0 = jnp.zeros((E, 1), jnp.int32)
    hi0 = jnp.full((E, 1), 0x7F800000, jnp.int32)
    lo, hi = lax.fori_loop(0, 31, body, (lo0, hi0))
    gt = x >= as_f32(hi)
    eq = (x >= as_f32(lo)) & jnp.logical_not(gt)
    need = cap - count(gt)
    ti = lax.broadcasted_iota(jnp.int32, (T, T), 0)
    tj = lax.broadcasted_iota(jnp.int32, (T, T), 1)
    upper = jnp.where(ti < tj, 1.0, 0.0).astype(BF16)

    def excl_cumsum(mask):
        carry = jnp.zeros((E, 1), F32)
        outs, starts = [], []
        for t in range(L // T):
            v = jnp.where(mask[:, t * T:(t + 1) * T], 1.0, 0.0)
            starts.append(carry)
            outs.append(_dot(v.astype(BF16), upper) + carry)
            carry = carry + jnp.sum(v, axis=1, keepdims=True)
        starts.append(carry)
        return jnp.concatenate(outs, axis=1), starts

    eq_rank, _ = excl_cumsum(eq)
    sel = gt | (eq & (eq_rank < need))
    slot, starts = excl_cumsum(sel)
    slot_ref[...] = jnp.where(sel, slot, -1.0).astype(jnp.int32)
    lane = lax.broadcasted_iota(jnp.int32, (E, 128), 1)
    s0 = jnp.zeros((E, 128), F32)
    for t, st in enumerate(starts):
        s0 = jnp.where(lane == t, st, s0)
    s0_ref[...] = s0.astype(jnp.int32)


def _select(aff_t, cap):
    B, E, L = aff_t.shape
    assert L // min(MOE_TILE, L) + 1 <= 128
    rows = B * E
    slot, s0 = pl.pallas_call(
        functools.partial(_select_kernel, cap=cap),
        grid=(1,),
        in_specs=[pl.BlockSpec((rows, L), lambda i: (0, 0))],
        out_specs=[pl.BlockSpec((rows, L), lambda i: (0, 0)), pl.BlockSpec((rows, 128), lambda i: (0, 0))],
        out_shape=[jax.ShapeDtypeStruct((rows, L), jnp.int32), jax.ShapeDtypeStruct((rows, 128), jnp.int32)],
        compiler_params=_cparams(("arbitrary",)),
        name="moe_select",
    )(aff_t.reshape(rows, L))
    return slot.reshape(B, E, L), s0.reshape(B, E, 128)


def _tile_group(nt):
    return MOE_TILES_PER_STEP if nt % MOE_TILES_PER_STEP == 0 else 1


def _window_start(s0_ref, b, e, i, nt, cap, win):
    base = (b * N_EXPERTS + e) * (nt + 1) + i
    a = (s0_ref[base] // SLOT_ALIGN) * SLOT_ALIGN
    return a, s0_ref[base + 1], jnp.minimum(a, cap - win)


def _n_windows(a, end, cap, win):
    return jnp.where(a + win >= cap, 1, jnp.maximum((end - a + win - 1) // win, 1))


def _slot_hits(slot_ref, cols, e, a, w, cap, win, jrow):
    lo_s = a + w * win
    r0 = jnp.minimum(lo_s, cap - win)
    se = slot_ref[0, e:e + 1, cols]
    return r0, ((se - r0) == jrow) & (se >= lo_s)


def _first_windows(s0_ref, slot_ref, cols, b, i, nt, cap, win):
    T = cols.stop - cols.start
    jrow = lax.broadcasted_iota(jnp.int32, (win, T), 0)
    first = []
    n_extra = 0
    for e in range(N_EXPERTS):
        a, end, _ = _window_start(s0_ref, b, e, i, nt, cap, win)
        r0, p = _slot_hits(slot_ref, cols, e, a, 0, cap, win, jrow)
        first.append((a, end, r0, p))
        n_extra = n_extra + (_n_windows(a, end, cap, win) - 1)
    onehot_t = jnp.where(jnp.concatenate([f[3] for f in first], axis=0), 1.0, 0.0).astype(BF16)
    return first, onehot_t, n_extra, jrow


def _dispatch_kernel(s0_ref, h_ref, slot_ref, aff_ref, *rest, cap, win, nt, row_off, group):
    xe_ref, ge_ref = rest[-2:]
    b = pl.program_id(0)
    i = pl.program_id(1)

    @pl.when(i == 0)
    def _():
        xe_ref[...] = jnp.zeros_like(xe_ref)
        ge_ref[...] = jnp.zeros_like(ge_ref)

    T = h_ref.shape[1] // group
    for j in range(group):
        cols = slice(j * T, (j + 1) * T)
        h = h_ref[0, cols, :]

        def place(e, r0, p, rows, cols=cols):
            r0 = pl.multiple_of(row_off + r0, SLOT_ALIGN)
            gate = jnp.sum(jnp.where(p, aff_ref[0, e:e + 1, cols], 0.0), axis=1, keepdims=True)
            xe_ref[e, pl.ds(r0, win), :] += rows.astype(BF16)
            ge_ref[e, pl.ds(r0, win), :] += gate

        first, onehot_t, n_extra, jrow = _first_windows(s0_ref, slot_ref, cols, b, i * group + j, nt, cap, win)
        rows = _dot(onehot_t, h)
        for e in range(N_EXPERTS):
            _, _, r0, p = first[e]
            place(e, r0, p, rows[e * win:(e + 1) * win])

        @pl.when(n_extra > 0)
        def _(first=first, cols=cols, h=h, jrow=jrow, place=place):
            for e in range(N_EXPERTS):
                a, end, _, _ = first[e]

                def extra(w, carry, e=e, a=a):
                    r0w, pw = _slot_hits(slot_ref, cols, e, a, w, cap, win, jrow)
                    place(e, r0w, pw, _dot(jnp.where(pw, 1.0, 0.0).astype(BF16), h))
                    return carry

                lax.fori_loop(1, _n_windows(a, end, cap, win), extra, 0)


def _dispatch(s0, h2, slot_t, aff_t, cap, win, stride, off, prev=None):
    B, L, D = h2.shape
    E = N_EXPERTS
    T = min(MOE_TILE, L)
    nt = L // T
    group = _tile_group(nt)
    if prev is None:
        blk_rows, row_off, per_sample, blk0 = stride, off, 1, 0
    else:
        assert stride % cap == 0 and off % cap == 0
        blk_rows, row_off, per_sample, blk0 = cap, 0, stride // cap, off // cap
    in_specs = [pl.BlockSpec((1, group * T, D), lambda b, i, s: (b, i, 0)),
                pl.BlockSpec((1, E, group * T), lambda b, i, s: (b, 0, i)),
                pl.BlockSpec((1, E, group * T), lambda b, i, s: (b, 0, i))]
    args = [s0, h2, slot_t, aff_t]
    aliases = {}
    if prev is not None:
        in_specs += [pl.BlockSpec(memory_space=pl.ANY), pl.BlockSpec(memory_space=pl.ANY)]
        aliases = {len(args): 0, len(args) + 1: 1}
        args += list(prev)
    grid_spec = pltpu.PrefetchScalarGridSpec(
        num_scalar_prefetch=1,
        grid=(B, nt // group),
        in_specs=in_specs,
        out_specs=[pl.BlockSpec((E, blk_rows, D), lambda b, i, s: (0, blk0 + b * per_sample, 0)),
                   pl.BlockSpec((E, blk_rows, 1), lambda b, i, s: (0, blk0 + b * per_sample, 0))],
    )
    return pl.pallas_call(
        functools.partial(_dispatch_kernel, cap=cap, win=win, nt=nt, row_off=row_off, group=group),
        grid_spec=grid_spec,
        out_shape=[jax.ShapeDtypeStruct((E, B * stride, D), BF16), jax.ShapeDtypeStruct((E, B * stride, 1), F32)],
        input_output_aliases=aliases,
        compiler_params=_cparams(("parallel", "arbitrary")),
        name="moe_dispatch",
    )(*args)


def _expert_kernel(x_ref, w1_ref, w3_ref, w2_ref, gate_ref, y_ref, wb1, wb3, wb2, *, n_experts, n_chunks):
    e = pl.program_id(0)
    m = pl.program_id(1)

    @pl.when(e < n_experts)
    def _():
        slot = e % 2
        wb1[slot, m] = w1_ref[0, 0].astype(BF16)
        wb3[slot, m] = w3_ref[0, 0].astype(BF16)
        wb2[slot, m] = w2_ref[0, 0].astype(BF16)

    @pl.when(e > 0)
    def _():
        slot = (e + 1) % 2
        x = x_ref[0]
        acts = []
        for c in range(n_chunks):
            a = _dot(x, wb1[slot, c])
            u = _dot(x, wb3[slot, c])
            acts.append((_silu(a) * u).astype(BF16))
        w2 = wb2[slot].reshape(n_chunks * wb2.shape[2], wb2.shape[3])
        y = _dot(jnp.concatenate(acts, axis=1), w2)
        y_ref[0] = (y * gate_ref[0]).astype(BF16)


def _experts(xe, w1, w3, w2, layer, gate):
    E, M, D = xe.shape
    F = w1.shape[3]
    tf = EXPERT_FF_CHUNK
    n_chunks = F // tf
    assert M % (n_chunks * SLOT_ALIGN) == 0
    tm = M // n_chunks
    cur = lambda e: jnp.minimum(e, E - 1)
    prv = lambda e: jnp.maximum(e - 1, 0)
    row = lambda e, m: jnp.where(e == 0, 0, m)
    return pl.pallas_call(
        functools.partial(_expert_kernel, n_experts=E, n_chunks=n_chunks),
        grid=(E + 1, n_chunks),
        in_specs=[pl.BlockSpec((1, tm, D), lambda e, m: (prv(e), row(e, m), 0)),
                  pl.BlockSpec((1, 1, D, tf), lambda e, m: (layer, cur(e), 0, m)),
                  pl.BlockSpec((1, 1, D, tf), lambda e, m: (layer, cur(e), 0, m)),
                  pl.BlockSpec((1, 1, tf, D), lambda e, m: (layer, cur(e), m, 0)),
                  pl.BlockSpec((1, tm, 1), lambda e, m: (prv(e), row(e, m), 0))],
        out_specs=pl.BlockSpec((1, tm, D), lambda e, m: (prv(e), row(e, m), 0)),
        out_shape=jax.ShapeDtypeStruct((E, M, D), BF16),
        scratch_shapes=[pltpu.VMEM((2, n_chunks, D, tf), BF16), pltpu.VMEM((2, n_chunks, D, tf), BF16),
                        pltpu.VMEM((2, n_chunks, tf, D), BF16)],
        compiler_params=_cparams(("arbitrary", "arbitrary")),
        name="expert_ffn",
    )(xe, w1, w3, w2, gate)


def _combine_kernel(s0_ref, slot_ref, x_ref, g2_ref, fin_ref, y_hbm, o_ref, stage, extra_stage, acc_ref,
                    sem, *, cap, win, nt, stride, off, group, final_norm):
    E = N_EXPERTS
    b = pl.program_id(0)
    i = pl.program_id(1)
    steps = nt // group
    step = b * steps + i
    total = pl.num_programs(0) * steps
    buf = step % 2
    T = x_ref.shape[1] // group

    def window_copies(src_row, dst, e, s):
        src_row = pl.multiple_of(src_row, SLOT_ALIGN)
        return (pltpu.make_async_copy(y_hbm.at[e, pl.ds(src_row, win)], dst, s),)

    def step_copies(bb, ii, bf):
        cps = []
        for j in range(group):
            for e in range(E):
                _, _, r0 = _window_start(s0_ref, bb, e, ii * group + j, nt, cap, win)
                cps.extend(window_copies(bb * stride + off + r0, stage.at[bf, j, pl.ds(e * win, win)], e, sem.at[bf]))
        return cps

    @pl.when(step == 0)
    def _():
        for cp in step_copies(b, i, 0):
            cp.start()

    nxt = step + 1

    @pl.when(nxt < total)
    def _():
        for cp in step_copies(nxt // steps, nxt % steps, 1 - buf):
            cp.start()

    for cp in step_copies(b, i, buf):
        cp.wait()

    for j in range(group):
        cols = slice(j * T, (j + 1) * T)
        first, onehot_t, n_extra, jrow = _first_windows(s0_ref, slot_ref, cols, b, i * group + j, nt, cap, win)
        acc_ref[...] = _dot_tn(onehot_t, stage[buf, j])

        @pl.when(n_extra > 0)
        def _(first=first, cols=cols, jrow=jrow):
            for e in range(E):
                a, end, _, _ = first[e]

                def extra(w, carry, e=e, a=a):
                    r0w, pw = _slot_hits(slot_ref, cols, e, a, w, cap, win, jrow)
                    cps = window_copies(b * stride + off + r0w, extra_stage, e, sem.at[2])
                    for cp in cps:
                        cp.start()
                    for cp in cps:
                        cp.wait()
                    acc_ref[...] += _dot_tn(jnp.where(pw, 1.0, 0.0).astype(BF16), extra_stage[...])
                    return carry

                lax.fori_loop(1, _n_windows(a, end, cap, win), extra, 0)

        out = x_ref[0, cols, :] + g2_ref[0] * acc_ref[...]
        if final_norm:
            ms = jnp.mean(out * out, axis=-1, keepdims=True)
            out = out * lax.rsqrt(ms + EPS) * fin_ref[...]
        o_ref[0, cols, :] = out


def _combine(s0, slot_t, x, g2, y, cap, win, stride, off, final_g=None):
    B, L, D = x.shape
    E = N_EXPERTS
    T = min(MOE_TILE, L)
    nt = L // T
    group = _tile_group(nt)
    grid_spec = pltpu.PrefetchScalarGridSpec(
        num_scalar_prefetch=1,
        grid=(B, nt // group),
        in_specs=[pl.BlockSpec((1, E, group * T), lambda b, i, s: (b, 0, i)),
                  pl.BlockSpec((1, group * T, D), lambda b, i, s: (b, i, 0)),
                  pl.BlockSpec((1, 1, D), lambda b, i, s: (b, 0, 0)),
                  pl.BlockSpec((1, D), lambda b, i, s: (0, 0)),
                  pl.BlockSpec(memory_space=pl.ANY)],
        out_specs=pl.BlockSpec((1, group * T, D), lambda b, i, s: (b, i, 0)),
        scratch_shapes=[pltpu.VMEM((2, group, E * win, D), BF16), pltpu.VMEM((win, D), BF16),
                        pltpu.VMEM((T, D), F32), pltpu.SemaphoreType.DMA((3,))],
    )
    return pl.pallas_call(
        functools.partial(_combine_kernel, cap=cap, win=win, nt=nt, stride=stride, off=off, group=group,
                          final_norm=final_g is not None),
        grid_spec=grid_spec,
        out_shape=jax.ShapeDtypeStruct((B, L, D), F32),
        compiler_params=_cparams(("arbitrary", "arbitrary")),
        name="moe_combine",
    )(s0, slot_t, x, g2, (jnp.ones((D,), F32) if final_g is None else final_g).reshape(1, D), y)


def _moe(streams, w1, w3, w2, layer, final_g=None):
    E = N_EXPERTS
    B = streams[0][0].shape[0]
    plans, stride = [], 0
    for x, g2, h2, aff_t in streams:
        L = h2.shape[1]
        cap = CAPACITY_FACTOR * L // E
        plans.append((cap, min(MOE_WIN, cap), L // min(MOE_TILE, L), stride))
        stride += cap
    routed, prev = [], None
    for (x, g2, h2, aff_t), (cap, win, nt, off) in zip(streams, plans):
        slot_t, s0_pad = _select(aff_t, cap)
        s0 = s0_pad[:, :, :nt + 1].reshape(-1)
        prev = _dispatch(s0, h2, slot_t, aff_t, cap, win, stride, off, prev)
        routed.append((s0, slot_t))
    xe, ge = prev
    y = _experts(xe, w1, w3, w2, layer, ge)
    outs = []
    for k, ((x, g2, h2, aff_t), (cap, win, nt, off), (s0, slot_t)) in enumerate(zip(streams, plans, routed)):
        outs.append(_combine(s0, slot_t, x, g2, y, cap, win, stride, off, final_g if k == 0 else None))
    return outs


def _rope_tables(n):
    half = HEAD_DIM // 2
    quarter = half // 2
    inv = ROPE_BASE ** (-jnp.arange(quarter, dtype=F32) / quarter)
    t = jnp.arange(n)
    row = (t // GRID_W).astype(F32)
    col = (t % GRID_W).astype(F32)
    lane = np.arange(128)
    within = lane % HEAD_DIM
    use_col = jnp.asarray((within >= half)[None, :])
    freq = inv[jnp.asarray(within % quarter)][None, :]
    ang = jnp.where(use_col, col[:, None], row[:, None]) * freq
    sign = jnp.asarray(np.where((lane % half) < quarter, -1.0, 1.0).astype(np.float32))[None, :]
    return jnp.cos(ang), jnp.sin(ang) * sign


def kernel(x, c, ctx, c_ctx, mod_w, mod_b, norm1_g, norm2_g, ab_w_in, ab_w_out, sgu_w, sgu_b, hgrn_lb_logits,
           hgrn_norm_g, attn_w_qkv, attn_w_out, attn_sink, router_w, expert_w1, expert_w3, expert_w2, final_g):
    B, N, D = x.shape
    Lc = ctx.shape[1]
    depth = mod_w.shape[0]

    rows = ((B + 1 + 7) // 8) * 8
    cvec = jnp.zeros((rows, D), F32).at[:B].set(c).at[B].set(c_ctx)
    mod = _mod_vectors(cvec, mod_w, mod_b)

    def mods(l):
        lat = [mod[l, :B, k * D:(k + 1) * D].reshape(B, 1, D) for k in range(6)]
        cx = [jnp.broadcast_to(mod[l, B, k * D:(k + 1) * D].reshape(1, 1, D), (B, 1, D)) for k in range(6)]
        return lat, cx

    lb_all = jnp.cumsum(jax.nn.softmax(hgrn_lb_logits.astype(F32), axis=0), axis=0)
    lmat_f = jnp.asarray(_hgrn_level_matrix(False), BF16)
    lmat_r = jnp.asarray(_hgrn_level_matrix(True), BF16)

    x_lat, x_ctx = x, ctx
    for l in range(depth):
        last = l == depth - 1
        (sh1, sc1, g1, sh2, sc2, g2), (csh1, csc1, cg1, csh2, csc2, cg2) = mods(l)
        wr_t = router_w[l].T
        if l % 2 == 0:
            e = l // 2
            w_in = ab_w_in[e].astype(BF16)
            w_out = ab_w_out[e].astype(BF16)
            sw = sgu_w[e].astype(BF16)
            sb_t = sgu_b[e].T
            hg = hgrn_norm_g[e].reshape(1, HGRN_DIM)
            s0 = jnp.zeros((2, B, HGRN_HEADS, HGRN_KDIM, HGRN_KDIM), F32)

            def even(xs, shift, scale, gate1, shift2, scale2, s_in):
                ya, z = _proj_sgu(xs, norm1_g[l], shift, scale, w_in, sw, sb_t)
                o_f, o_b, s_fin = _hgrn(z, lb_all[e], s_in, lmat_f, lmat_r)
                xn, h2, aff = _even_post(ya, o_f, o_b, z, hg, w_out, xs, gate1, norm2_g[l], shift2, scale2, wr_t)
                return xn, h2, aff, s_fin

            x_ctx, h2c, affc, s_ctx = even(x_ctx, csh1, csc1, cg1, csh2, csc2, s0)
            x_lat, h2l, affl, _ = even(x_lat, sh1, sc1, g1, sh2, sc2, s_ctx)
        else:
            if not last:
                raise NotImplementedError("context update after an attention layer")
            o_idx = l // 2
            w_qkv = attn_w_qkv[o_idx].astype(BF16)
            w_out = attn_w_out[o_idx].astype(BF16)
            cos, sin = _rope_tables(N)
            zc = _proj(x_ctx, norm1_g[l], csh1, csc1, w_qkv[:, ATTN_Q_DIM:])
            k_ctx = zc[..., :ATTN_KV_DIM].astype(BF16)
            v_ctx = zc[..., ATTN_KV_DIM:].astype(BF16)
            q, k, v = _qkv_lat(x_lat, norm1_g[l], sh1, sc1, w_qkv, cos, sin)
            x_lat, h2l, affl = _attention_post(q, k, v, k_ctx, v_ctx, attn_sink[o_idx], w_out, x_lat, g1,
                                               norm2_g[l], sh2, sc2, wr_t)
        streams = [(x_lat, g2, h2l, affl)]
        if not last:
            streams.append((x_ctx, cg2, h2c, affc))
        outs = _moe(streams, expert_w1, expert_w3, expert_w2, l, final_g if last else None)
        x_lat = outs[0]
        if not last:
            x_ctx = outs[1]
    return x_lat
```

```python
import functools

import numpy as np
import jax
import jax.numpy as jnp
from jax import lax
from jax.experimental import pallas as pl
from jax.experimental.pallas import tpu as pltpu

F32 = jnp.float32
BF16 = jnp.bfloat16

D_MODEL = 1024
GRID_W = 64
EPS = 1e-6
SGU_GROUPS = 4
SGU_GROUP_DIM = 128
SGU_DIM = SGU_GROUPS * SGU_GROUP_DIM
SGU_CHUNK = 128
HGRN_HEADS = 4
HGRN_KDIM = 128
HGRN_DIM = HGRN_HEADS * HGRN_KDIM
HGRN_CHUNK = 64
HGRN_LEVELS = 6
HGRN_COARSE = 3
HGRN_MXU_LEVELS = 5
HGRN_TBLK = 512
AB_IN = 2 * SGU_DIM + 5 * HGRN_DIM
ATTN_HEADS = 16
ATTN_KV_HEADS = 4
ATTN_GROUP = ATTN_HEADS // ATTN_KV_HEADS
HEAD_DIM = 64
ATTN_Q_DIM = ATTN_HEADS * HEAD_DIM
ATTN_KV_DIM = ATTN_KV_HEADS * HEAD_DIM
WINDOW = 128
ATTN_BLOCK = 128
ATTN_STEP_BLOCKS = 8
ROPE_BASE = 10000.0
N_EXPERTS = 16
CAPACITY_FACTOR = 2
EXPERT_FF = 2048
NEG_BIG = -1e30
LOG2_E = 1.4426950408889634

VMEM_LIMIT_BYTES = 56 * 1024 * 1024


ROW_TILE = 512
ROW_TILE_WIDE = 1024


def _row_tile(n_rows, limit=ROW_TILE):
    return min(limit, n_rows)


def _cparams(sem):
    return pltpu.CompilerParams(dimension_semantics=sem, vmem_limit_bytes=VMEM_LIMIT_BYTES)


def _dot(a, b):
    return jnp.dot(a, b, preferred_element_type=F32)


def _dot_nt(a, b):
    return lax.dot_general(a, b, (((1,), (1,)), ((), ())), preferred_element_type=F32)


def _dot_tn(a, b):
    return lax.dot_general(a, b, (((0,), (0,)), ((), ())), preferred_element_type=F32)


def _split2(a):
    hi = a.astype(BF16)
    lo = (a - hi.astype(F32)).astype(BF16)
    return hi, lo


def _dot_f32x3(a, b):
    a_hi, a_lo = _split2(a)
    b_hi, b_lo = _split2(b)
    return _dot(a_hi, b_hi) + (_dot(a_hi, b_lo) + _dot(a_lo, b_hi))


def _silu(x):
    return x * jax.nn.sigmoid(x)


def _gelu(x):
    return 0.5 * x * (1.0 + lax.erf(x * np.float32(1.0 / np.sqrt(2.0))))


def _rms_modulate(x, g, shift, scale):
    ms = jnp.mean(x * x, axis=-1, keepdims=True)
    return (x * lax.rsqrt(ms + EPS) * g) * (1.0 + scale) + shift


def _mod_kernel(c_ref, w_ref, b_ref, o_ref):
    s = _silu(c_ref[...])
    o_ref[0] = _dot_f32x3(s, w_ref[0]) + b_ref[0]


def _mod_vectors(cvec, mod_w, mod_b):
    depth, d, n6 = mod_w.shape
    rows = cvec.shape[0]
    tn = 1536
    return pl.pallas_call(
        _mod_kernel,
        grid=(depth, n6 // tn),
        in_specs=[
            pl.BlockSpec((rows, d), lambda l, j: (0, 0)),
            pl.BlockSpec((1, d, tn), lambda l, j: (l, 0, j)),
            pl.BlockSpec((1, 1, tn), lambda l, j: (l, 0, j)),
        ],
        out_specs=pl.BlockSpec((1, rows, tn), lambda l, j: (l, 0, j)),
        out_shape=jax.ShapeDtypeStruct((depth, rows, n6), F32),
        compiler_params=_cparams(("arbitrary", "arbitrary")),
        name="mod_vectors",
    )(cvec, mod_w, mod_b.reshape(depth, 1, n6))


def _proj_kernel(x_ref, g_ref, sh_ref, sc_ref, w_ref, z_ref):
    h = _rms_modulate(x_ref[0], g_ref[...], sh_ref[0], sc_ref[0])
    z_ref[0] = _dot(h.astype(BF16), w_ref[...])


def _row_spec(tm, d):
    return pl.BlockSpec((1, tm, d), lambda b, i: (b, i, 0))


def _vec_spec(d):
    return pl.BlockSpec((1, 1, d), lambda b, i: (b, 0, 0))


def _const_spec(shape):
    nd = len(shape)
    return pl.BlockSpec(shape, lambda b, i: (0,) * nd)


def _proj(x, g, shift, scale, w):
    B, L, D = x.shape
    tm = _row_tile(L)
    n = w.shape[1]
    return pl.pallas_call(
        _proj_kernel,
        grid=(B, L // tm),
        in_specs=[_row_spec(tm, D), _const_spec((1, D)), _vec_spec(D), _vec_spec(D), _const_spec((D, n))],
        out_specs=_row_spec(tm, n),
        out_shape=jax.ShapeDtypeStruct((B, L, n), F32),
        compiler_params=_cparams(("parallel", "parallel")),
        name="norm_mod_proj",
    )(x, g.reshape(1, D), shift, scale, w)


def _sgu_rows(zu, zv, w_ref, b_ref, o_ref, rows):
    u = _gelu(zu)
    v = _gelu(zv)
    for g in range(SGU_GROUPS):
        cols = slice(g * SGU_GROUP_DIM, (g + 1) * SGU_GROUP_DIM)
        vg = v[:, cols]
        mu = jnp.mean(vg, axis=-1, keepdims=True)
        vc = vg - mu
        var = jnp.mean(vc * vc, axis=-1, keepdims=True)
        vn = (vc * lax.rsqrt(var + EPS)).astype(BF16)
        s = _dot(w_ref[g], vn) + b_ref[:, g:g + 1]
        o_ref[0, rows, cols] = (u[:, cols] * s).astype(BF16)


def _proj_sgu_kernel(x_ref, g_ref, sh_ref, sc_ref, w_ref, sw_ref, sb_ref, ya_ref, zr_ref, *, tm):
    h = _rms_modulate(x_ref[0], g_ref[...], sh_ref[0], sc_ref[0]).astype(BF16)
    z_uv = _dot(h, w_ref[:, :2 * SGU_DIM])
    zr_ref[0] = _dot(h, w_ref[:, 2 * SGU_DIM:])
    for ch in range(tm // SGU_CHUNK):
        rows = slice(ch * SGU_CHUNK, (ch + 1) * SGU_CHUNK)
        _sgu_rows(z_uv[rows, :SGU_DIM], z_uv[rows, SGU_DIM:], sw_ref, sb_ref, ya_ref, rows)


def _proj_sgu(x, g, shift, scale, w, sgu_w_bf16, sgu_b_t):
    B, L, D = x.shape
    tm = _row_tile(L)
    n_rest = w.shape[1] - 2 * SGU_DIM
    return pl.pallas_call(
        functools.partial(_proj_sgu_kernel, tm=tm),
        grid=(B, L // tm),
        in_specs=[_row_spec(tm, D), _const_spec((1, D)), _vec_spec(D), _vec_spec(D), _const_spec((D, w.shape[1])),
                  _const_spec((SGU_GROUPS, SGU_CHUNK, SGU_CHUNK)), _const_spec((SGU_CHUNK, SGU_GROUPS))],
        out_specs=[_row_spec(tm, SGU_DIM), _row_spec(tm, n_rest)],
        out_shape=[jax.ShapeDtypeStruct((B, L, SGU_DIM), BF16), jax.ShapeDtypeStruct((B, L, n_rest), F32)],
        compiler_params=_cparams(("parallel", "parallel")),
        name="norm_mod_proj_sgu",
    )(x, g.reshape(1, D), shift, scale, w, sgu_w_bf16, sgu_b_t)


def _hgrn_level_matrix(reverse):
    C = HGRN_CHUNK
    t = np.arange(C)[:, None]
    s = np.arange(C)[None, :]
    blocks = []
    blocks.append(((s <= t) if not reverse else (s >= t)).astype(np.float32))
    for l in range(HGRN_COARSE, HGRN_MXU_LEVELS):
        m = C >> (l + 1)
        base = (t // (2 * m)) * (2 * m)
        if not reverse:
            r = base + m - 1
            qrow = (t % (2 * m)) >= m
            mq = (s > r) & (s <= t)
            mk = (s > t) & (s <= r)
        else:
            r = base + m
            qrow = (t % (2 * m)) < m
            mq = (s >= t) & (s < r)
            mk = (s >= r) & (s < t)
        blocks.append(np.where(qrow, mq, mk).astype(np.float32))
    return np.concatenate(blocks, axis=0)


def _hgrn_masks(reverse):
    C = HGRN_CHUNK
    t = lax.broadcasted_iota(jnp.int32, (C, C), 0)
    s = lax.broadcasted_iota(jnp.int32, (C, C), 1)
    r = lax.broadcasted_iota(jnp.int32, (C, 1), 0)
    masks, qrows = [], []
    for l in range(HGRN_LEVELS):
        m = C >> (l + 1)
        same = (t // (2 * m)) == (s // (2 * m))
        if not reverse:
            mk = same & ((t % (2 * m)) >= m) & ((s % (2 * m)) < m)
            qr = (r % (2 * m)) >= m
        else:
            mk = same & ((t % (2 * m)) < m) & ((s % (2 * m)) >= m)
            qr = (r % (2 * m)) < m
        masks.append(mk)
        qrows.append(qr)
    return masks, qrows


def _hgrn_chunk(zq, zf, zi, lb, st, lmat, masks, qrows, reverse):
    C = HGRN_CHUNK
    f = lb + (1.0 - lb) * jax.nn.sigmoid(zf)
    g = jnp.log(f)
    kk = 1.0 - f
    q = _silu(zq)
    vb = zi.astype(BF16)
    g_hi, g_lo = _split2(g)
    d2 = _dot(lmat, jnp.concatenate([g_hi, g_lo], axis=1))
    d = d2[:, :HGRN_KDIM] + d2[:, HGRN_KDIM:]
    b = d[0:C]
    b_end = b[0:1] if reverse else b[C - 1:C]
    dec = jnp.exp(b_end)
    qb = (q * jnp.exp(b)).astype(BF16)
    kb = (kk * jnp.exp(b_end - b)).astype(BF16)
    o = _dot_nt(qb, st.astype(BF16))
    o = o + jnp.sum(q * kk, axis=1, keepdims=True) * zi
    shift = C - 1 if reverse else 1
    near = jnp.sum(q * f * pltpu.roll(kk, shift, 0), axis=1, keepdims=True)
    o = o + jnp.where(qrows[HGRN_LEVELS - 1], near, 0.0) * pltpu.roll(zi, shift, 0)
    a = None
    for l in range(HGRN_MXU_LEVELS):
        if l < HGRN_COARSE:
            m = C >> (l + 1)
            r = m if reverse else m - 1
            ref = jnp.broadcast_to(b.reshape(C // (2 * m), 2 * m, HGRN_KDIM)[:, r:r + 1, :],
                                   (C // (2 * m), 2 * m, HGRN_KDIM)).reshape(C, HGRN_KDIM)
            dl = jnp.where(qrows[l], b - ref, ref - b)
        else:
            dl = d[(1 + l - HGRN_COARSE) * C:(2 + l - HGRN_COARSE) * C]
        mix = (jnp.where(qrows[l], q, kk) * jnp.exp(dl)).astype(BF16)
        p = jnp.where(masks[l], _dot_nt(mix, mix), 0.0)
        a = p if a is None else a + p
    o = o + _dot(a.astype(BF16), vb)
    st_new = st * dec + _dot_tn(vb, kb)
    return o, st_new


def _hgrn_kernel(zqf_ref, zff_ref, zif_ref, zqb_ref, zfb_ref, zib_ref, lb_ref, s0_ref, lf_ref, lr_ref,
                 of_ref, ob_ref, sfin_ref, st_ref, *, tb):
    i = pl.program_id(1)
    nb = pl.num_programs(1)

    @pl.when(i == 0)
    def _():
        st_ref[...] = s0_ref[:, 0]

    C = HGRN_CHUNK
    nch = tb // C
    for d in range(2):
        reverse = d == 1
        masks, qrows = _hgrn_masks(reverse)
        lmat = (lr_ref if reverse else lf_ref)[...]
        zq_ref, zf_ref, zi_ref, o_ref = ((zqb_ref, zfb_ref, zib_ref, ob_ref) if reverse
                                         else (zqf_ref, zff_ref, zif_ref, of_ref))
        for h in range(HGRN_HEADS):
            cols = slice(h * HGRN_KDIM, (h + 1) * HGRN_KDIM)
            lb = lb_ref[d:d + 1, cols]
            st = st_ref[d, h]
            order = range(nch - 1, -1, -1) if reverse else range(nch)
            for c in order:
                rows = slice(c * C, (c + 1) * C)
                o, st = _hgrn_chunk(zq_ref[0, rows, cols], zf_ref[0, rows, cols], zi_ref[0, rows, cols],
                                    lb, st, lmat, masks, qrows, reverse)
                o_ref[0, rows, cols] = o.astype(BF16)
            st_ref[d, h] = st

    @pl.when(i == nb - 1)
    def _():
        sfin_ref[:, 0] = st_ref[...]


def _hgrn(z, lb, s0, lmat_f, lmat_r):
    B, L, _ = z.shape
    tb = min(HGRN_TBLK, L)
    nb = L // tb
    nl = lmat_f.shape[0]

    def zspec(col, rev):
        if rev:
            return pl.BlockSpec((1, tb, HGRN_DIM), lambda b, i: (b, nb - 1 - i, col))
        return pl.BlockSpec((1, tb, HGRN_DIM), lambda b, i: (b, i, col))

    st_spec = pl.BlockSpec((2, 1, HGRN_HEADS, HGRN_KDIM, HGRN_KDIM), lambda b, i: (0, b, 0, 0, 0))
    return pl.pallas_call(
        functools.partial(_hgrn_kernel, tb=tb),
        grid=(B, nb),
        in_specs=[zspec(0, False), zspec(1, False), zspec(3, False),
                  zspec(0, True), zspec(2, True), zspec(3, True),
                  _const_spec((2, HGRN_DIM)), st_spec,
                  _const_spec((nl, HGRN_CHUNK)), _const_spec((nl, HGRN_CHUNK))],
        out_specs=[pl.BlockSpec((1, tb, HGRN_DIM), lambda b, i: (b, i, 0)),
                   pl.BlockSpec((1, tb, HGRN_DIM), lambda b, i: (b, nb - 1 - i, 0)),
                   st_spec],
        out_shape=[jax.ShapeDtypeStruct((B, L, HGRN_DIM), BF16),
                   jax.ShapeDtypeStruct((B, L, HGRN_DIM), BF16),
                   jax.ShapeDtypeStruct(s0.shape, F32)],
        scratch_shapes=[pltpu.VMEM((2, HGRN_HEADS, HGRN_KDIM, HGRN_KDIM), F32)],
        compiler_params=_cparams(("parallel", "arbitrary")),
        name="hgrn2_scan",
    )(z, z, z, z, z, z, lb, s0, lmat_f, lmat_r)


def _residual_router_tail(x, y, g1, n2g, sh2, sc2, wr, xo_ref, h2_ref, aff_ref):
    xn = x + g1 * y
    xo_ref[0] = xn
    h2 = _rms_modulate(xn, n2g, sh2, sc2)
    h2_ref[0] = h2.astype(BF16)
    w_hi, w_lo = _split2(wr)
    h_hi, h_lo = _split2(h2)
    logits = _dot_nt(w_hi, h_hi) + (_dot_nt(w_hi, h_lo) + _dot_nt(w_lo, h_hi))
    mx = jnp.max(logits, axis=0, keepdims=True)
    p = jnp.exp(logits - mx)
    aff_ref[0] = p / jnp.sum(p, axis=0, keepdims=True)


def _even_post_kernel(ya_ref, of_ref, ob_ref, zg_ref, hg_ref, w_ref, x_ref, g1_ref, n2g_ref, sh2_ref,
                      sc2_ref, wr_ref, xo_ref, h2_ref, aff_ref):
    o = of_ref[0].astype(F32) + ob_ref[0].astype(F32)
    zg = zg_ref[0]
    parts = []
    for h in range(HGRN_HEADS):
        cols = slice(h * HGRN_KDIM, (h + 1) * HGRN_KDIM)
        oh = o[:, cols]
        ms = jnp.mean(oh * oh, axis=-1, keepdims=True)
        parts.append(oh * lax.rsqrt(ms + EPS) * hg_ref[:, cols])
    yb = (jnp.concatenate(parts, axis=1) * _silu(zg)).astype(BF16)
    y = _dot(ya_ref[0], w_ref[:SGU_DIM, :]) + _dot(yb, w_ref[SGU_DIM:, :])
    _residual_router_tail(x_ref[0], y, g1_ref[0], n2g_ref[...], sh2_ref[0], sc2_ref[0], wr_ref[...],
                          xo_ref, h2_ref, aff_ref)


def _post_out_specs(B, L, tm):
    D = D_MODEL
    specs = [_row_spec(tm, D), _row_spec(tm, D), pl.BlockSpec((1, N_EXPERTS, tm), lambda b, i: (b, 0, i))]
    shapes = [jax.ShapeDtypeStruct((B, L, D), F32), jax.ShapeDtypeStruct((B, L, D), BF16),
              jax.ShapeDtypeStruct((B, N_EXPERTS, L), F32)]
    return specs, shapes


def _even_post(ya, o_f, o_b, z, hg, w_out, x, g1, n2g, sh2, sc2, wr):
    B, L, D = x.shape
    tm = _row_tile(L, ROW_TILE_WIDE)
    specs, shapes = _post_out_specs(B, L, tm)
    return pl.pallas_call(
        _even_post_kernel,
        grid=(B, L // tm),
        in_specs=[_row_spec(tm, SGU_DIM), _row_spec(tm, HGRN_DIM), _row_spec(tm, HGRN_DIM),
                  pl.BlockSpec((1, tm, HGRN_DIM), lambda b, i: (b, i, 4)),
                  _const_spec((1, HGRN_DIM)), _const_spec((D, D)), _row_spec(tm, D), _vec_spec(D),
                  _const_spec((1, D)), _vec_spec(D), _vec_spec(D), _const_spec((N_EXPERTS, D))],
        out_specs=specs,
        out_shape=shapes,
        compiler_params=_cparams(("parallel", "parallel")),
        name="even_post_router",
    )(ya, o_f, o_b, z, hg, w_out, x, g1, n2g.reshape(1, D), sh2, sc2, wr)


def _rope(x, cos, sin_signed):
    lane = lax.broadcasted_iota(jnp.int32, (x.shape[0], 128), 1)
    first = (lane % 32) < 16
    outs = []
    for j in range(x.shape[1] // 128):
        xb = x[:, j * 128:(j + 1) * 128]
        sw = jnp.where(first, pltpu.roll(xb, 112, 1), pltpu.roll(xb, 16, 1))
        outs.append(xb * cos + sw * sin_signed)
    return jnp.concatenate(outs, axis=1)


def _qkv_lat_kernel(x_ref, g_ref, sh_ref, sc_ref, w_ref, cos_ref, sin_ref, q_ref, k_ref, v_ref):
    h = _rms_modulate(x_ref[0], g_ref[...], sh_ref[0], sc_ref[0])
    z = _dot(h.astype(BF16), w_ref[...])
    cos = cos_ref[...]
    sin = sin_ref[...]
    scale = np.float32(HEAD_DIM ** -0.5 * LOG2_E)
    q_ref[0] = (_rope(z[:, :ATTN_Q_DIM], cos, sin) * scale).astype(BF16)
    k_ref[0] = _rope(z[:, ATTN_Q_DIM:ATTN_Q_DIM + ATTN_KV_DIM], cos, sin).astype(BF16)
    v_ref[0] = z[:, ATTN_Q_DIM + ATTN_KV_DIM:].astype(BF16)


def _qkv_lat(x, g, shift, scale, w, cos, sin):
    B, L, D = x.shape
    tm = _row_tile(L, ROW_TILE_WIDE)
    n = w.shape[1]
    tab = pl.BlockSpec((tm, 128), lambda b, i: (i, 0))
    return pl.pallas_call(
        _qkv_lat_kernel,
        grid=(B, L // tm),
        in_specs=[_row_spec(tm, D), _const_spec((1, D)), _vec_spec(D), _vec_spec(D), _const_spec((D, n)), tab, tab],
        out_specs=[_row_spec(tm, ATTN_Q_DIM), _row_spec(tm, ATTN_KV_DIM), _row_spec(tm, ATTN_KV_DIM)],
        out_shape=[jax.ShapeDtypeStruct((B, L, ATTN_Q_DIM), BF16),
                   jax.ShapeDtypeStruct((B, L, ATTN_KV_DIM), BF16),
                   jax.ShapeDtypeStruct((B, L, ATTN_KV_DIM), BF16)],
        compiler_params=_cparams(("parallel", "parallel")),
        name="norm_qkv_rope",
    )(x, g.reshape(1, D), shift, scale, w, cos, sin)


def _attn_block(q_ref, k_parts, v_parts, kx_ref, vx_ref, sink_ref, o_ref, rows, bi, n_tokens):
    blk = ATTN_BLOCK
    n_loc = 3 * blk
    nk = n_loc + kx_ref.shape[1]
    n_pairs = ATTN_GROUP // 2
    nq = n_pairs * blk
    pad = 16
    s_idx = lax.broadcasted_iota(jnp.int32, (n_loc, nq), 0)
    r_idx = lax.broadcasted_iota(jnp.int32, (n_loc, nq), 1) % blk
    j = bi * blk - blk + s_idx
    valid = (jnp.abs(s_idx - r_idx - blk) <= WINDOW) & (j >= 0) & (j < n_tokens)
    lane = lax.broadcasted_iota(jnp.int32, (nk, 128), 1)
    qcol = lax.broadcasted_iota(jnp.int32, (1, nq), 1)
    prow = lax.broadcasted_iota(jnp.int32, (pad, nq), 0)
    er = lax.broadcasted_iota(jnp.int32, (2 * nk + pad, 128), 0)
    el = lax.broadcasted_iota(jnp.int32, (2 * nk + pad, 128), 1)
    lo_rows = (er < nk) | (er == 2 * nk)
    hi_rows = ((er >= nk) & (er < 2 * nk)) | (er == 2 * nk + 1)
    ind = jnp.where((lo_rows & (el < HEAD_DIM)) | (hi_rows & (el >= HEAD_DIM)), 1.0, 0.0).astype(BF16)
    zpad = jnp.zeros((pad, 128), BF16)
    scores, values = [], []
    for kvh in range(ATTN_KV_HEADS):
        cols = slice((kvh // 2) * 128, (kvh // 2 + 1) * 128)
        own = (lane < HEAD_DIM) if kvh % 2 == 0 else (lane >= HEAD_DIM)
        k_all = jnp.concatenate([r[0, rs, cols] for r, rs in k_parts] + [kx_ref[0, :, cols]], axis=0)
        v_all = jnp.concatenate([r[0, rs, cols] for r, rs in v_parts] + [vx_ref[0, :, cols]], axis=0)
        k_own = jnp.where(own, k_all, jnp.zeros_like(k_all))
        v_own = jnp.where(own, v_all, jnp.zeros_like(v_all))
        k_oth = pltpu.roll(k_own, HEAD_DIM, 1)
        v_oth = pltpu.roll(v_own, HEAD_DIM, 1)
        if kvh % 2 == 0:
            k_cat = jnp.concatenate([k_own, k_oth], axis=0)
            v_cat = jnp.concatenate([v_own, v_oth, zpad], axis=0)
        else:
            k_cat = jnp.concatenate([k_oth, k_own], axis=0)
            v_cat = jnp.concatenate([v_oth, v_own, zpad], axis=0)
        q2 = jnp.concatenate([q_ref[0, rows, (kvh * n_pairs + p) * 128:(kvh * n_pairs + p + 1) * 128]
                              for p in range(n_pairs)], axis=0)
        scores.append(_dot_nt(k_cat, q2))
        values.append(jnp.concatenate([v_cat, ind], axis=1))
    probs = []
    for kvh in range(ATTN_KV_HEADS):
        st = scores[kvh]
        ps, es = [], []
        for sub in range(2):
            sh = st[sub * nk:(sub + 1) * nk]
            sh = jnp.concatenate([jnp.where(valid, sh[:n_loc], NEG_BIG), sh[n_loc:]], axis=0)
            hd0 = kvh * ATTN_GROUP + sub
            sink = sink_ref[:, hd0:hd0 + 1]
            for p in range(1, n_pairs):
                sink = jnp.where(qcol < p * blk, sink, sink_ref[:, hd0 + 2 * p:hd0 + 2 * p + 1])
            m = jnp.maximum(jnp.max(sh, axis=0, keepdims=True), sink)
            ps.append(jnp.exp2(sh - m).astype(BF16))
            es.append(jnp.exp2(sink - m))
        sink_rows = jnp.where(prow == 0, es[0], jnp.where(prow == 1, es[1], 0.0)).astype(BF16)
        probs.append(jnp.concatenate(ps + [sink_rows], axis=0))
    for kvh in range(ATTN_KV_HEADS):
        od = _dot_tn(probs[kvh], values[kvh])
        o = (od[:, :128] / od[:, 128:]).astype(BF16)
        for p in range(n_pairs):
            o_ref[0, rows, (kvh * n_pairs + p) * 128:(kvh * n_pairs + p + 1) * 128] = o[p * blk:(p + 1) * blk]


def _attn_kernel(q_ref, kp_ref, kc_ref, kn_ref, vp_ref, vc_ref, vn_ref, kx_ref, vx_ref, sink_ref,
                 w_ref, x_ref, g1_ref, n2g_ref, sh2_ref, sc2_ref, wr_ref, xo_ref, h2_ref, aff_ref, o_ref,
                 *, n_tokens, nblk):
    blk = ATTN_BLOCK
    first = nblk * pl.program_id(1)

    def part(prev_ref, cur_ref, next_ref, idx):
        if idx < 0:
            return prev_ref, slice(0, blk)
        if idx >= nblk:
            return next_ref, slice(0, blk)
        return cur_ref, slice(idx * blk, (idx + 1) * blk)

    for j in range(nblk):
        _attn_block(q_ref, [part(kp_ref, kc_ref, kn_ref, j + d) for d in (-1, 0, 1)],
                    [part(vp_ref, vc_ref, vn_ref, j + d) for d in (-1, 0, 1)],
                    kx_ref, vx_ref, sink_ref, o_ref, slice(j * blk, (j + 1) * blk), first + j, n_tokens)
    y = _dot(o_ref[0], w_ref[...])
    _residual_router_tail(x_ref[0], y, g1_ref[0], n2g_ref[...], sh2_ref[0], sc2_ref[0], wr_ref[...],
                          xo_ref, h2_ref, aff_ref)


def _attention_post(q, k, v, k_ctx, v_ctx, sink, w_out, x, g1, n2g, sh2, sc2, wr):
    B, N, _ = q.shape
    D = x.shape[2]
    Lc = k_ctx.shape[1]
    blk = ATTN_BLOCK
    nb = N // blk
    nblk = ATTN_STEP_BLOCKS if nb % ATTN_STEP_BLOCKS == 0 else 1

    def edge_spec(off):
        return pl.BlockSpec((1, blk, ATTN_KV_DIM), lambda b, i: (b, jnp.clip(nblk * i + off, 0, nb - 1), 0))

    cur_spec = pl.BlockSpec((1, nblk * blk, ATTN_KV_DIM), lambda b, i: (b, i, 0))
    ctx_spec = pl.BlockSpec((1, Lc, ATTN_KV_DIM), lambda b, i: (b, 0, 0))
    specs, shapes = _post_out_specs(B, N, nblk * blk)
    return pl.pallas_call(
        functools.partial(_attn_kernel, n_tokens=N, nblk=nblk),
        grid=(B, nb // nblk),
        in_specs=[_row_spec(nblk * blk, ATTN_Q_DIM), edge_spec(-1), cur_spec, edge_spec(nblk),
                  edge_spec(-1), cur_spec, edge_spec(nblk), ctx_spec, ctx_spec,
                  _const_spec((1, ATTN_HEADS)),
                  _const_spec((ATTN_Q_DIM, D)), _row_spec(nblk * blk, D), _vec_spec(D), _const_spec((1, D)),
                  _vec_spec(D), _vec_spec(D), _const_spec((N_EXPERTS, D))],
        out_specs=specs,
        out_shape=shapes,
        scratch_shapes=[pltpu.VMEM((1, nblk * blk, ATTN_Q_DIM), BF16)],
        compiler_params=_cparams(("parallel", "parallel")),
        name="window_attention_post_router",
    )(q, k, k, k, v, v, v, k_ctx, v_ctx, (sink * np.float32(LOG2_E)).reshape(1, ATTN_HEADS),
      w_out, x, g1, n2g.reshape(1, D), sh2, sc2, wr)


MOE_TILE = 256
MOE_WIN = 64
MOE_DISPATCH_TILES = 2
MOE_COMBINE_TILES = 4
SLOT_ALIGN = 16
EXPERT_FF_CHUNK = 256


def _select_kernel(aff_ref, slot_ref, s0_ref, *, cap):
    x = aff_ref[...]
    E, L = x.shape
    T = min(MOE_TILE, L)

    def count(mask):
        return jnp.sum(jnp.where(mask, 1.0, 0.0), axis=1, keepdims=True)

    def as_f32(bits):
        return lax.bitcast_convert_type(bits, F32)

    def body(_, c):
        lo, hi = c
        mid = lo + lax.shift_right_logical(hi - lo, 1)
        ok = count(x >= as_f32(mid)) >= cap
        return jnp.where(ok, mid, lo), jnp.where(ok, hi, mid)

    lo0 = jnp.zeros((E, 1), jnp.int32)
    hi0 = jnp.full((E, 1), 0x7F800000, jnp.int32)
    lo, hi = lax.fori_loop(0, 31, body, (lo0, hi0))
    gt = x >= as_f32(hi)
    eq = (x >= as_f32(lo)) & jnp.logical_not(gt)
    need = cap - count(gt)
    ti = lax.broadcasted_iota(jnp.int32, (T, T), 0)
    tj = lax.broadcasted_iota(jnp.int32, (T, T), 1)
    upper = jnp.where(ti < tj, 1.0, 0.0).astype(BF16)

    def excl_cumsum(mask):
        carry = jnp.zeros((E, 1), F32)
        outs, starts = [], []
        for t in range(L // T):
            v = jnp.where(mask[:, t * T:(t + 1) * T], 1.0, 0.0)
            starts.append(carry)
            outs.append(_dot(v.astype(BF16), upper) + carry)
            carry = carry + jnp.sum(v, axis=1, keepdims=True)
        starts.append(carry)
        return jnp.concatenate(outs, axis=1), starts

    eq_rank, _ = excl_cumsum(eq)
    sel = gt | (eq & (eq_rank < need))
    slot, starts = excl_cumsum(sel)
    slot_ref[...] = jnp.where(sel, slot, -1.0).astype(jnp.int32)
    lane = lax.broadcasted_iota(jnp.int32, (E, 128), 1)
    s0 = jnp.zeros((E, 128), F32)
    for t, st in enumerate(starts):
        s0 = jnp.where(lane == t, st, s0)
    s0_ref[...] = s0.astype(jnp.int32)


def _select(aff_t, cap):
    B, E, L = aff_t.shape
    assert L // min(MOE_TILE, L) + 1 <= 128
    rows = B * E
    slot, s0 = pl.pallas_call(
        functools.partial(_select_kernel, cap=cap),
        grid=(1,),
        in_specs=[pl.BlockSpec((rows, L), lambda i: (0, 0))],
        out_specs=[pl.BlockSpec((rows, L), lambda i: (0, 0)), pl.BlockSpec((rows, 128), lambda i: (0, 0))],
        out_shape=[jax.ShapeDtypeStruct((rows, L), jnp.int32), jax.ShapeDtypeStruct((rows, 128), jnp.int32)],
        compiler_params=_cparams(("arbitrary",)),
        name="moe_select",
    )(aff_t.reshape(rows, L))
    return slot.reshape(B, E, L), s0.reshape(B, E, 128)


def _tile_group(nt, limit):
    return limit if nt % limit == 0 else 1


def _window_start(s0_ref, b, e, i, nt, cap, win):
    base = (b * N_EXPERTS + e) * (nt + 1) + i
    a = (s0_ref[base] // SLOT_ALIGN) * SLOT_ALIGN
    return a, s0_ref[base + 1], jnp.minimum(a, cap - win)


def _n_windows(a, end, cap, win):
    return jnp.where(a + win >= cap, 1, jnp.maximum((end - a + win - 1) // win, 1))


def _slot_hits(slot_ref, cols, e, a, w, cap, win, jrow):
    lo_s = a + w * win
    r0 = jnp.minimum(lo_s, cap - win)
    se = slot_ref[0, e:e + 1, cols]
    return r0, ((se - r0) == jrow) & (se >= lo_s)


def _first_windows(s0_ref, slot_ref, cols, b, i, nt, cap, win):
    T = cols.stop - cols.start
    jrow = lax.broadcasted_iota(jnp.int32, (win, T), 0)
    first = []
    n_extra = 0
    for e in range(N_EXPERTS):
        a, end, _ = _window_start(s0_ref, b, e, i, nt, cap, win)
        r0, p = _slot_hits(slot_ref, cols, e, a, 0, cap, win, jrow)
        first.append((a, end, r0, p))
        n_extra = n_extra + (_n_windows(a, end, cap, win) - 1)
    onehot_t = jnp.where(jnp.concatenate([f[3] for f in first], axis=0), 1.0, 0.0).astype(BF16)
    return first, onehot_t, n_extra, jrow


def _dispatch_kernel(s0_ref, h_ref, slot_ref, aff_ref, *rest, cap, win, nt, row_off, group):
    xe_ref, ge_ref = rest[-2:]
    b = pl.program_id(0)
    i = pl.program_id(1)

    @pl.when(i == 0)
    def _():
        xe_ref[...] = jnp.zeros_like(xe_ref)
        ge_ref[...] = jnp.zeros_like(ge_ref)

    T = h_ref.shape[1] // group
    for j in range(group):
        cols = slice(j * T, (j + 1) * T)
        h = h_ref[0, cols, :]

        def place(e, r0, p, rows, cols=cols):
            r0 = pl.multiple_of(row_off + r0, SLOT_ALIGN)
            gate = jnp.sum(jnp.where(p, aff_ref[0, e:e + 1, cols], 0.0), axis=1, keepdims=True)
            xe_ref[e, pl.ds(r0, win), :] += rows.astype(BF16)
            ge_ref[e, pl.ds(r0, win), :] += gate

        first, onehot_t, n_extra, jrow = _first_windows(s0_ref, slot_ref, cols, b, i * group + j, nt, cap, win)
        rows = _dot(onehot_t, h)
        for e in range(N_EXPERTS):
            _, _, r0, p = first[e]
            place(e, r0, p, rows[e * win:(e + 1) * win])

        @pl.when(n_extra > 0)
        def _(first=first, cols=cols, h=h, jrow=jrow, place=place):
            for e in range(N_EXPERTS):
                a, end, _, _ = first[e]

                def extra(w, carry, e=e, a=a):
                    r0w, pw = _slot_hits(slot_ref, cols, e, a, w, cap, win, jrow)
                    place(e, r0w, pw, _dot(jnp.where(pw, 1.0, 0.0).astype(BF16), h))
                    return carry

                lax.fori_loop(1, _n_windows(a, end, cap, win), extra, 0)


def _dispatch(s0, h2, slot_t, aff_t, cap, win, stride, off, prev=None):
    B, L, D = h2.shape
    E = N_EXPERTS
    T = min(MOE_TILE, L)
    nt = L // T
    group = _tile_group(nt, MOE_DISPATCH_TILES)
    if prev is None:
        blk_rows, row_off, per_sample, blk0 = stride, off, 1, 0
    else:
        assert stride % cap == 0 and off % cap == 0
        blk_rows, row_off, per_sample, blk0 = cap, 0, stride // cap, off // cap
    in_specs = [pl.BlockSpec((1, group * T, D), lambda b, i, s: (b, i, 0)),
                pl.BlockSpec((1, E, group * T), lambda b, i, s: (b, 0, i)),
                pl.BlockSpec((1, E, group * T), lambda b, i, s: (b, 0, i))]
    args = [s0, h2, slot_t, aff_t]
    aliases = {}
    if prev is not None:
        in_specs += [pl.BlockSpec(memory_space=pl.ANY), pl.BlockSpec(memory_space=pl.ANY)]
        aliases = {len(args): 0, len(args) + 1: 1}
        args += list(prev)
    grid_spec = pltpu.PrefetchScalarGridSpec(
        num_scalar_prefetch=1,
        grid=(B, nt // group),
        in_specs=in_specs,
        out_specs=[pl.BlockSpec((E, blk_rows, D), lambda b, i, s: (0, blk0 + b * per_sample, 0)),
                   pl.BlockSpec((E, blk_rows, 1), lambda b, i, s: (0, blk0 + b * per_sample, 0))],
    )
    return pl.pallas_call(
        functools.partial(_dispatch_kernel, cap=cap, win=win, nt=nt, row_off=row_off, group=group),
        grid_spec=grid_spec,
        out_shape=[jax.ShapeDtypeStruct((E, B * stride, D), BF16), jax.ShapeDtypeStruct((E, B * stride, 1), F32)],
        input_output_aliases=aliases,
        compiler_params=_cparams(("parallel", "arbitrary")),
        name="moe_dispatch",
    )(*args)


def _expert_kernel(x_ref, w1_ref, w3_ref, w2_ref, gate_ref, y_ref, wb1, wb3, wb2, *, n_experts, n_chunks):
    e = pl.program_id(0)
    m = pl.program_id(1)

    @pl.when(e < n_experts)
    def _():
        slot = e % 2
        wb1[slot, m] = w1_ref[0, 0].astype(BF16)
        wb3[slot, m] = w3_ref[0, 0].astype(BF16)
        wb2[slot, m] = w2_ref[0, 0].astype(BF16)

    @pl.when(e > 0)
    def _():
        slot = (e + 1) % 2
        x = x_ref[0]
        acts = []
        for c in range(n_chunks):
            a = _dot(x, wb1[slot, c])
            u = _dot(x, wb3[slot, c])
            acts.append((_silu(a) * u).astype(BF16))
        w2 = wb2[slot].reshape(n_chunks * wb2.shape[2], wb2.shape[3])
        y = _dot(jnp.concatenate(acts, axis=1), w2)
        y_ref[0] = (y * gate_ref[0]).astype(BF16)


def _experts(xe, w1, w3, w2, layer, gate):
    E, M, D = xe.shape
    F = w1.shape[3]
    tf = EXPERT_FF_CHUNK
    n_chunks = F // tf
    assert M % (n_chunks * SLOT_ALIGN) == 0
    tm = M // n_chunks
    cur = lambda e: jnp.minimum(e, E - 1)
    prv = lambda e: jnp.maximum(e - 1, 0)
    row = lambda e, m: jnp.where(e == 0, 0, m)
    return pl.pallas_call(
        functools.partial(_expert_kernel, n_experts=E, n_chunks=n_chunks),
        grid=(E + 1, n_chunks),
        in_specs=[pl.BlockSpec((1, tm, D), lambda e, m: (prv(e), row(e, m), 0)),
                  pl.BlockSpec((1, 1, D, tf), lambda e, m: (layer, cur(e), 0, m)),
                  pl.BlockSpec((1, 1, D, tf), lambda e, m: (layer, cur(e), 0, m)),
                  pl.BlockSpec((1, 1, tf, D), lambda e, m: (layer, cur(e), m, 0)),
                  pl.BlockSpec((1, tm, 1), lambda e, m: (prv(e), row(e, m), 0))],
        out_specs=pl.BlockSpec((1, tm, D), lambda e, m: (prv(e), row(e, m), 0)),
        out_shape=jax.ShapeDtypeStruct((E, M, D), BF16),
        scratch_shapes=[pltpu.VMEM((2, n_chunks, D, tf), BF16), pltpu.VMEM((2, n_chunks, D, tf), BF16),
                        pltpu.VMEM((2, n_chunks, tf, D), BF16)],
        compiler_params=_cparams(("arbitrary", "arbitrary")),
        name="expert_ffn",
    )(xe, w1, w3, w2, gate)


def _combine_kernel(s0_ref, slot_ref, x_ref, g2_ref, fin_ref, y_hbm, o_ref, stage, extra_stage, acc_ref,
                    sem, *, cap, win, nt, stride, off, group, final_norm):
    E = N_EXPERTS
    b = pl.program_id(0)
    i = pl.program_id(1)
    steps = nt // group
    step = b * steps + i
    total = pl.num_programs(0) * steps
    buf = step % 2
    T = x_ref.shape[1] // group

    def window_copies(src_row, dst, e, s):
        src_row = pl.multiple_of(src_row, SLOT_ALIGN)
        return (pltpu.make_async_copy(y_hbm.at[e, pl.ds(src_row, win)], dst, s),)

    def step_copies(bb, ii, bf):
        cps = []
        for j in range(group):
            for e in range(E):
                _, _, r0 = _window_start(s0_ref, bb, e, ii * group + j, nt, cap, win)
                cps.extend(window_copies(bb * stride + off + r0, stage.at[bf, j, pl.ds(e * win, win)], e, sem.at[bf]))
        return cps

    @pl.when(step == 0)
    def _():
        for cp in step_copies(b, i, 0):
            cp.start()

    nxt = step + 1

    @pl.when(nxt < total)
    def _():
        for cp in step_copies(nxt // steps, nxt % steps, 1 - buf):
            cp.start()

    for cp in step_copies(b, i, buf):
        cp.wait()

    for j in range(group):
        cols = slice(j * T, (j + 1) * T)
        first, onehot_t, n_extra, jrow = _first_windows(s0_ref, slot_ref, cols, b, i * group + j, nt, cap, win)
        acc_ref[...] = _dot_tn(onehot_t, stage[buf, j])

        @pl.when(n_extra > 0)
        def _(first=first, cols=cols, jrow=jrow):
            for e in range(E):
                a, end, _, _ = first[e]

                def extra(w, carry, e=e, a=a):
                    r0w, pw = _slot_hits(slot_ref, cols, e, a, w, cap, win, jrow)
                    cps = window_copies(b * stride + off + r0w, extra_stage, e, sem.at[2])
                    for cp in cps:
                        cp.start()
                    for cp in cps:
                        cp.wait()
                    acc_ref[...] += _dot_tn(jnp.where(pw, 1.0, 0.0).astype(BF16), extra_stage[...])
                    return carry

                lax.fori_loop(1, _n_windows(a, end, cap, win), extra, 0)

        out = x_ref[0, cols, :] + g2_ref[0] * acc_ref[...]
        if final_norm:
            ms = jnp.mean(out * out, axis=-1, keepdims=True)
            out = out * lax.rsqrt(ms + EPS) * fin_ref[...]
        o_ref[0, cols, :] = out


def _combine(s0, slot_t, x, g2, y, cap, win, stride, off, final_g=None):
    B, L, D = x.shape
    E = N_EXPERTS
    T = min(MOE_TILE, L)
    nt = L // T
    group = _tile_group(nt, MOE_COMBINE_TILES)
    grid_spec = pltpu.PrefetchScalarGridSpec(
        num_scalar_prefetch=1,
        grid=(B, nt // group),
        in_specs=[pl.BlockSpec((1, E, group * T), lambda b, i, s: (b, 0, i)),
                  pl.BlockSpec((1, group * T, D), lambda b, i, s: (b, i, 0)),
                  pl.BlockSpec((1, 1, D), lambda b, i, s: (b, 0, 0)),
                  pl.BlockSpec((1, D), lambda b, i, s: (0, 0)),
                  pl.BlockSpec(memory_space=pl.ANY)],
        out_specs=pl.BlockSpec((1, group * T, D), lambda b, i, s: (b, i, 0)),
        scratch_shapes=[pltpu.VMEM((2, group, E * win, D), BF16), pltpu.VMEM((win, D), BF16),
                        pltpu.VMEM((T, D), F32), pltpu.SemaphoreType.DMA((3,))],
    )
    return pl.pallas_call(
        functools.partial(_combine_kernel, cap=cap, win=win, nt=nt, stride=stride, off=off, group=group,
                          final_norm=final_g is not None),
        grid_spec=grid_spec,
        out_shape=jax.ShapeDtypeStruct((B, L, D), F32),
        compiler_params=_cparams(("arbitrary", "arbitrary")),
        name="moe_combine",
    )(s0, slot_t, x, g2, (jnp.ones((D,), F32) if final_g is None else final_g).reshape(1, D), y)


def _moe(streams, w1, w3, w2, layer, final_g=None):
    E = N_EXPERTS
    B = streams[0][0].shape[0]
    plans, stride = [], 0
    for x, g2, h2, aff_t in streams:
        L = h2.shape[1]
        cap = CAPACITY_FACTOR * L // E
        plans.append((cap, min(MOE_WIN, cap), L // min(MOE_TILE, L), stride))
        stride += cap
    routed, prev = [], None
    for (x, g2, h2, aff_t), (cap, win, nt, off) in zip(streams, plans):
        slot_t, s0_pad = _select(aff_t, cap)
        s0 = s0_pad[:, :, :nt + 1].reshape(-1)
        prev = _dispatch(s0, h2, slot_t, aff_t, cap, win, stride, off, prev)
        routed.append((s0, slot_t))
    xe, ge = prev
    y = _experts(xe, w1, w3, w2, layer, ge)
    outs = []
    for k, ((x, g2, h2, aff_t), (cap, win, nt, off), (s0, slot_t)) in enumerate(zip(streams, plans, routed)):
        outs.append(_combine(s0, slot_t, x, g2, y, cap, win, stride, off, final_g if k == 0 else None))
    return outs


def _rope_tables(n):
    half = HEAD_DIM // 2
    quarter = half // 2
    inv = ROPE_BASE ** (-jnp.arange(quarter, dtype=F32) / quarter)
    t = jnp.arange(n)
    row = (t // GRID_W).astype(F32)
    col = (t % GRID_W).astype(F32)
    lane = np.arange(128)
    within = lane % HEAD_DIM
    use_col = jnp.asarray((within >= half)[None, :])
    freq = inv[jnp.asarray(within % quarter)][None, :]
    ang = jnp.where(use_col, col[:, None], row[:, None]) * freq
    sign = jnp.asarray(np.where((lane % half) < quarter, -1.0, 1.0).astype(np.float32))[None, :]
    return jnp.cos(ang), jnp.sin(ang) * sign


def kernel(x, c, ctx, c_ctx, mod_w, mod_b, norm1_g, norm2_g, ab_w_in, ab_w_out, sgu_w, sgu_b, hgrn_lb_logits,
           hgrn_norm_g, attn_w_qkv, attn_w_out, attn_sink, router_w, expert_w1, expert_w3, expert_w2, final_g):
    B, N, D = x.shape
    Lc = ctx.shape[1]
    depth = mod_w.shape[0]

    rows = ((B + 1 + 7) // 8) * 8
    cvec = jnp.zeros((rows, D), F32).at[:B].set(c).at[B].set(c_ctx)
    mod = _mod_vectors(cvec, mod_w, mod_b)

    def mods(l):
        lat = [mod[l, :B, k * D:(k + 1) * D].reshape(B, 1, D) for k in range(6)]
        cx = [jnp.broadcast_to(mod[l, B, k * D:(k + 1) * D].reshape(1, 1, D), (B, 1, D)) for k in range(6)]
        return lat, cx

    lb_all = jnp.cumsum(jax.nn.softmax(hgrn_lb_logits.astype(F32), axis=0), axis=0)
    lmat_f = jnp.asarray(_hgrn_level_matrix(False), BF16)
    lmat_r = jnp.asarray(_hgrn_level_matrix(True), BF16)

    x_lat, x_ctx = x, ctx
    for l in range(depth):
        last = l == depth - 1
        (sh1, sc1, g1, sh2, sc2, g2), (csh1, csc1, cg1, csh2, csc2, cg2) = mods(l)
        wr_t = router_w[l].T
        if l % 2 == 0:
            e = l // 2
            w_in = ab_w_in[e].astype(BF16)
            w_out = ab_w_out[e].astype(BF16)
            sw = sgu_w[e].astype(BF16)
            sb_t = sgu_b[e].T
            hg = hgrn_norm_g[e].reshape(1, HGRN_DIM)
            s0 = jnp.zeros((2, B, HGRN_HEADS, HGRN_KDIM, HGRN_KDIM), F32)

            def even(xs, shift, scale, gate1, shift2, scale2, s_in):
                ya, z = _proj_sgu(xs, norm1_g[l], shift, scale, w_in, sw, sb_t)
                o_f, o_b, s_fin = _hgrn(z, lb_all[e], s_in, lmat_f, lmat_r)
                xn, h2, aff = _even_post(ya, o_f, o_b, z, hg, w_out, xs, gate1, norm2_g[l], shift2, scale2, wr_t)
                return xn, h2, aff, s_fin

            x_ctx, h2c, affc, s_ctx = even(x_ctx, csh1, csc1, cg1, csh2, csc2, s0)
            x_lat, h2l, affl, _ = even(x_lat, sh1, sc1, g1, sh2, sc2, s_ctx)
        else:
            if not last:
                raise NotImplementedError("context update after an attention layer")
            o_idx = l // 2
            w_qkv = attn_w_qkv[o_idx].astype(BF16)
            w_out = attn_w_out[o_idx].astype(BF16)
            cos, sin = _rope_tables(N)
            zc = _proj(x_ctx, norm1_g[l], csh1, csc1, w_qkv[:, ATTN_Q_DIM:])
            k_ctx = zc[..., :ATTN_KV_DIM].astype(BF16)
            v_ctx = zc[..., ATTN_KV_DIM:].astype(BF16)
            q, k, v = _qkv_lat(x_lat, norm1_g[l], sh1, sc1, w_qkv, cos, sin)
            x_lat, h2l, affl = _attention_post(q, k, v, k_ctx, v_ctx, attn_sink[o_idx], w_out, x_lat, g1,
                                               norm2_g[l], sh2, sc2, wr_t)
        streams = [(x_lat, g2, h2l, affl)]
        if not last:
            streams.append((x_ctx, cg2, h2c, affc))
        outs = _moe(streams, expert_w1, expert_w3, expert_w2, l, final_g if last else None)
        x_lat = outs[0]
        if not last:
            x_ctx = outs[1]
    return x_lat
```
